```python
import jax, jax.numpy as jnp
from jax import lax
import numpy as np

D_MODEL = 2048
BATCH = 32
SEQ = 256
DEPTH = 2
DEC_BATCH = 2
DEC_SEQ = 2048
PAST_LEN = 512

GRID_W = 64
LRU_WIDTH = 1024
LRU_HEADS = 8
LRU_HEAD_DIM = LRU_WIDTH // LRU_HEADS
LRU_CONV = 4
LRU_C = 8.0
CONV_WIDTH = 1024
SHORT_CONV = 3
MIX_WIDTH = LRU_WIDTH + CONV_WIDTH
D_IN = 2 * LRU_WIDTH + 3 * CONV_WIDTH
D_FF = ((8 * D_MODEL // 3 + 255) // 256) * 256
N_MOD = 6
EPS = 1e-6
POS_BASE = 10000.0

kernel_name = "hybrid_rglru_shortconv_diffusion_step"


def rmsnorm(x, g):
    xf = x.astype(jnp.float32)
    y = xf * lax.rsqrt(jnp.mean(xf * xf, axis=-1, keepdims=True) + EPS)
    return (y * g.astype(jnp.float32)).astype(x.dtype)


def modulation(cvec, w_mod, b_mod):
    m = jax.nn.silu(cvec) @ w_mod + b_mod
    return m.reshape(cvec.shape[0], N_MOD, 1, D_MODEL)


def dwconv(x, w, pad):
    return lax.conv_general_dilated(
        x, w[:, None, :].astype(x.dtype), window_strides=(1,), padding=[pad],
        dimension_numbers=("NWC", "WIO", "NWC"), feature_group_count=x.shape[-1])


def block_diag(x, w, b):
    xh = x.reshape(x.shape[0], x.shape[1], LRU_HEADS, LRU_HEAD_DIM)
    y = jnp.einsum("blhd,hde->blhe", xh, w.astype(jnp.float32)) + b.astype(jnp.float32)
    return y.reshape(x.shape)


def rglru_dir(x, w_a, b_a, w_i, b_i, lam, h0, reverse):
    r = jax.nn.sigmoid(block_diag(x, w_a, b_a))
    i = jax.nn.sigmoid(block_diag(x, w_i, b_i))
    log_a = -LRU_C * r * jax.nn.softplus(-lam.astype(jnp.float32))
    a = jnp.exp(log_a)
    u = jnp.sqrt(jnp.maximum(-jnp.expm1(2.0 * log_a), 0.0)) * (i * x)

    def step(h, au):
        a_t, u_t = au
        h = a_t * h + u_t
        return h, h

    h_last, hs = lax.scan(step, h0, (jnp.swapaxes(a, 0, 1), jnp.swapaxes(u, 0, 1)), reverse=reverse)
    return jnp.swapaxes(hs, 0, 1), h_last


def grid_pos_embed(n_tok, dtype):
    rows = n_tok // GRID_W
    r = jnp.repeat(jnp.arange(rows, dtype=jnp.float32), GRID_W)
    col = jnp.tile(jnp.arange(GRID_W, dtype=jnp.float32), rows)
    n_freq = D_MODEL // 4
    freqs = 1.0 / (POS_BASE ** (jnp.arange(n_freq, dtype=jnp.float32) / n_freq))
    er = r[:, None] * freqs
    ec = col[:, None] * freqs
    return jnp.concatenate([jnp.sin(er), jnp.cos(er), jnp.sin(ec), jnp.cos(ec)], axis=-1).astype(dtype)


def trunk_layer(x, mod, h0, g, w_in, cw, cb, w_a, b_a, w_i, b_i, lam, sc_w, w_out, w_gate, w_up, w_down):
    mod = mod.astype(x.dtype)
    hn = rmsnorm(x, g[0]) * (1.0 + mod[:, 1]) + mod[:, 0]
    z = hn @ w_in
    x_lru, y_gate, b_g, c_g, v = jnp.split(
        z, [LRU_WIDTH, 2 * LRU_WIDTH, 2 * LRU_WIDTH + CONV_WIDTH, 2 * LRU_WIDTH + 2 * CONV_WIDTH], axis=-1)
    xc = (dwconv(x_lru, cw, (2, 1)) + cb).astype(jnp.float32)
    h0f = h0.astype(jnp.float32)
    o_f, s_f = rglru_dir(xc, w_a[0], b_a[0], w_i[0], b_i[0], lam[0], h0f[:, 0], False)
    o_b, s_b = rglru_dir(xc, w_a[1], b_a[1], w_i[1], b_i[1], lam[1], h0f[:, 1], True)
    lru_out = (o_f + o_b).astype(x.dtype) * jax.nn.gelu(y_gate)
    sc_out = b_g * dwconv(c_g * v, sc_w, (1, 1))
    m = jnp.concatenate([lru_out, sc_out], axis=-1) @ w_out
    x = x + mod[:, 2] * m
    hn = rmsnorm(x, g[1]) * (1.0 + mod[:, 4]) + mod[:, 3]
    f = (jax.nn.silu(hn @ w_gate) * (hn @ w_up)) @ w_down
    x = x + mod[:, 5] * f
    return x, jnp.stack([s_f, s_b], axis=1)


def setup_inputs(seed: int = 0) -> dict:
    key = jax.random.key(seed)
    ks = jax.random.split(key, 24)
    f32 = jnp.float32
    nrm = lambda k, shape, s: (jax.random.normal(k, shape, f32) * s)
    hd = LRU_HEAD_DIM
    u = jax.random.uniform(ks[14], (DEPTH, 2, LRU_WIDTH), f32, minval=0.9, maxval=0.999)
    a0 = u ** (1.0 / LRU_C)
    lru_lambda = jnp.log(a0) - jnp.log1p(-a0)
    return {
        "x_prompt": nrm(ks[0], (BATCH, SEQ, D_MODEL), 1.0),
        "x_sample": nrm(ks[1], (DEC_BATCH, DEC_SEQ, D_MODEL), 1.0),
        "state_rglru": nrm(ks[2], (DEC_BATCH, DEPTH, 2, LRU_WIDTH), 1.0),
        "c": nrm(ks[3], (DEC_BATCH, D_MODEL), 1.0),
        "c_ctx": nrm(ks[4], (D_MODEL,), 1.0),
        "w_mod": nrm(ks[5], (DEPTH, D_MODEL, N_MOD * D_MODEL), 0.5 * D_MODEL ** -0.5),
        "b_mod": nrm(ks[6], (DEPTH, N_MOD * D_MODEL), 0.02),
        "norm_g": 1.0 + nrm(ks[7], (DEPTH, 2, D_MODEL), 0.01),
        "w_in": nrm(ks[8], (DEPTH, D_MODEL, D_IN), D_MODEL ** -0.5),
        "lru_conv_w": nrm(ks[9], (DEPTH, LRU_CONV, LRU_WIDTH), LRU_CONV ** -0.5),
        "lru_conv_b": nrm(ks[10], (DEPTH, LRU_WIDTH), 0.01),
        "lru_w_a": nrm(ks[11], (DEPTH, 2, LRU_HEADS, hd, hd), hd ** -0.5),
        "lru_b_a": nrm(ks[12], (DEPTH, 2, LRU_HEADS, hd), 0.01),
        "lru_w_i": nrm(ks[13], (DEPTH, 2, LRU_HEADS, hd, hd), hd ** -0.5),
        "lru_b_i": nrm(ks[15], (DEPTH, 2, LRU_HEADS, hd), 0.01),
        "lru_lambda": lru_lambda,
        "sc_conv_w": nrm(ks[16], (DEPTH, SHORT_CONV, CONV_WIDTH), SHORT_CONV ** -0.5),
        "w_out": nrm(ks[17], (DEPTH, MIX_WIDTH, D_MODEL), MIX_WIDTH ** -0.5),
        "w_gate": nrm(ks[18], (DEPTH, D_MODEL, D_FF), D_MODEL ** -0.5),
        "w_up": nrm(ks[19], (DEPTH, D_MODEL, D_FF), D_MODEL ** -0.5),
        "w_down": nrm(ks[20], (DEPTH, D_FF, D_MODEL), D_FF ** -0.5),
        "final_g": 1.0 + nrm(ks[21], (D_MODEL,), 0.01),
    }


def reference(x_prompt, x_sample, state_rglru, c, c_ctx, w_mod, b_mod, norm_g, w_in, lru_conv_w, lru_conv_b,
              lru_w_a, lru_b_a, lru_w_i, lru_b_i, lru_lambda, sc_conv_w, w_out, w_gate, w_up, w_down, final_g):
    xc = x_prompt
    h_zero = jnp.zeros((x_prompt.shape[0], 2, LRU_WIDTH), jnp.float32)
    ctx_states = []
    for l in range(DEPTH):
        mod_ctx = modulation(c_ctx[None, :], w_mod[l], b_mod[l])
        xc, st = trunk_layer(xc, mod_ctx, h_zero, norm_g[l], w_in[l], lru_conv_w[l], lru_conv_b[l],
                             lru_w_a[l], lru_b_a[l], lru_w_i[l], lru_b_i[l], lru_lambda[l], sc_conv_w[l],
                             w_out[l], w_gate[l], w_up[l], w_down[l])
        ctx_states.append(st)
    y_prompt = rmsnorm(xc, final_g)
    new_state = jnp.stack(ctx_states, axis=1).astype(x_prompt.dtype)

    xs = x_sample + grid_pos_embed(x_sample.shape[1], x_sample.dtype)[None]
    for l in range(DEPTH):
        mod_lat = modulation(c, w_mod[l], b_mod[l])
        xs, _ = trunk_layer(xs, mod_lat, state_rglru[:, l], norm_g[l], w_in[l], lru_conv_w[l], lru_conv_b[l],
                            lru_w_a[l], lru_b_a[l], lru_w_i[l], lru_b_i[l], lru_lambda[l], sc_conv_w[l],
                            w_out[l], w_gate[l], w_up[l], w_down[l])
    y_sample = rmsnorm(xs, final_g)
    return (y_prompt, y_sample, new_state)
```

```python
import functools
import math

import numpy as np
import jax
import jax.numpy as jnp
from jax import lax
from jax.experimental import pallas as pl
from jax.experimental.pallas import tpu as pltpu

D_MODEL = 2048
BATCH = 32
SEQ = 256
DEPTH = 2
DEC_BATCH = 2
DEC_SEQ = 2048
GRID_W = 64
LRU_WIDTH = 1024
LRU_HEADS = 8
HEAD_DIM = 128
LRU_CONV = 4
LRU_C = 8.0
CONV_WIDTH = 1024
SHORT_CONV = 3
D_IN = 2 * LRU_WIDTH + 3 * CONV_WIDTH
D_FF = 5632
N_MOD = 6
EPS = 1e-6
POS_BASE = 10000.0

CTX_ROWS = BATCH * SEQ
LAT_ROWS = DEC_BATCH * DEC_SEQ
ROWS = CTX_ROWS + LAT_ROWS
N_SLAB = D_IN // HEAD_DIM

GROUP_ROWS = 2048
CHUNK = 256
N_CHUNK = GROUP_ROWS // CHUNK
N_GROUPS = ROWS // GROUP_ROWS
N_CTX_GROUPS = CTX_ROWS // GROUP_ROWS
PITCH = CHUNK + 8

V7X_VMEM_LIMIT = 56 * 1024 * 1024

_BF = jnp.bfloat16
_F32 = jnp.float32


def _cparams(sem):
    return pltpu.CompilerParams(dimension_semantics=sem, vmem_limit_bytes=V7X_VMEM_LIMIT)


def _sigmoid(x):
    return 1.0 / (1.0 + jnp.exp(-x))


def _row_group(tile, rows_per_tile):
    return jnp.maximum((tile * rows_per_tile) // DEC_SEQ - CTX_ROWS // DEC_SEQ + 1, 0)


def _mod_row(mod_ref, j, grp):
    return mod_ref[j, pl.ds(grp, 1), :]


def _rms(x):
    return x * lax.rsqrt(jnp.mean(x * x, axis=-1, keepdims=True) + EPS)


MOD_TN = 1024


def _mod_kernel(c_ref, w_ref, b_ref, o_ref):
    c = c_ref[...]
    s = (c * _sigmoid(c)).astype(_BF)
    o_ref[...] = jnp.dot(s, w_ref[...].astype(_BF), preferred_element_type=_F32) + b_ref[...]


def _modulation(c8, w_mod, b_mod):
    per = D_MODEL // MOD_TN
    b4 = b_mod.reshape(DEPTH, N_MOD, 1, D_MODEL)
    return pl.pallas_call(
        _mod_kernel,
        grid=(DEPTH, N_MOD * per),
        in_specs=[
            pl.BlockSpec((8, D_MODEL), lambda l, n: (0, 0)),
            pl.BlockSpec((None, D_MODEL, MOD_TN), lambda l, n: (l, 0, n)),
            pl.BlockSpec((None, None, 1, MOD_TN), lambda l, n: (l, n // per, 0, n % per)),
        ],
        out_specs=pl.BlockSpec((None, None, 8, MOD_TN), lambda l, n: (l, n // per, 0, n % per)),
        out_shape=jax.ShapeDtypeStruct((DEPTH, N_MOD, 8, D_MODEL), _F32),
        compiler_params=_cparams(("arbitrary", "arbitrary")),
        name="modulation",
    )(c8, w_mod, b4)


PRE_TM = 512


def _prenorm_kernel(xp_ref, xs_ref, pos_ref, mod_ref, g_ref, hn_ref, x0_ref):
    i = pl.program_id(0)
    is_lat = i >= CTX_ROWS // PRE_TM
    grp = _row_group(i, PRE_TM)
    x = jnp.where(is_lat, xs_ref[...] + pos_ref[...], xp_ref[...])
    x0_ref[...] = x
    y = _rms(x) * g_ref[0:1, :]
    hn = y * (1.0 + _mod_row(mod_ref, 1, grp)) + _mod_row(mod_ref, 0, grp)
    hn_ref[...] = hn.astype(_BF)


def _prenorm(xp, xs, pos, mod, norm_g):
    nct = CTX_ROWS // PRE_TM
    npos = DEC_SEQ // PRE_TM
    return pl.pallas_call(
        _prenorm_kernel,
        grid=(ROWS // PRE_TM,),
        in_specs=[
            pl.BlockSpec((PRE_TM, D_MODEL), lambda i: (jnp.minimum(i, nct - 1), 0)),
            pl.BlockSpec((PRE_TM, D_MODEL), lambda i: (jnp.maximum(i - nct, 0), 0)),
            pl.BlockSpec((PRE_TM, D_MODEL), lambda i: (jnp.maximum(i - nct, 0) % npos, 0)),
            pl.BlockSpec((None, N_MOD, 8, D_MODEL), lambda i: (0, 0, 0, 0)),
            pl.BlockSpec((None, 2, D_MODEL), lambda i: (0, 0, 0)),
        ],
        out_specs=[
            pl.BlockSpec((PRE_TM, D_MODEL), lambda i: (i, 0)),
            pl.BlockSpec((PRE_TM, D_MODEL), lambda i: (i, 0)),
        ],
        out_shape=[
            jax.ShapeDtypeStruct((ROWS, D_MODEL), _BF),
            jax.ShapeDtypeStruct((ROWS, D_MODEL), _F32),
        ],
        compiler_params=_cparams(("arbitrary",)),
        name="prenorm",
    )(xp, xs, pos, mod, norm_g)


IN_TM = 1024
IN_TN = 1024


def _inproj_kernel(hn_ref, w_ref, z_ref):
    acc = jnp.dot(hn_ref[...], w_ref[...], preferred_element_type=_F32)
    for j in range(IN_TN // HEAD_DIM):
        z_ref[j] = acc[:, j * HEAD_DIM:(j + 1) * HEAD_DIM]


def _inproj(layer, hn, w_in_b):
    return pl.pallas_call(
        _inproj_kernel,
        grid=(ROWS // IN_TM, D_IN // IN_TN),
        in_specs=[
            pl.BlockSpec((IN_TM, D_MODEL), lambda i, n: (i, 0)),
            pl.BlockSpec((None, D_MODEL, IN_TN), lambda i, n: (layer, 0, n)),
        ],
        out_specs=pl.BlockSpec((IN_TN // HEAD_DIM, IN_TM, HEAD_DIM), lambda i, n: (n, i, 0)),
        out_shape=jax.ShapeDtypeStruct((N_SLAB, ROWS, HEAD_DIM), _F32),
        compiler_params=_cparams(("arbitrary", "arbitrary")),
        name=f"inproj{layer}",
    )(hn, w_in_b)


def _mixer_kernel(xl_ref, yg_ref, bg_ref, cg_ref, v_ref, cw_ref, cb_ref, wa_ref, wi_ref, ba_ref, bi_ref,
                  lam_ref, scw_ref, h0_ref, lru_ref, sc_ref, st_ref, a_scr, u_scr, pe_scr, hc_scr):
    g = pl.program_id(0)
    is_lat = g >= N_CTX_GROUPS
    seq_mask = jnp.where(is_lat, GROUP_ROWS - 1, CHUNK - 1)
    tin = lax.broadcasted_iota(jnp.int32, (GROUP_ROWS, HEAD_DIM), 0) & seq_mask

    def shifted(val, k):
        rolled = pltpu.roll(val, (-k) % GROUP_ROWS, axis=0)
        ok = (tin + k >= 0) & (tin + k <= seq_mask)
        return jnp.where(ok, rolled, 0.0)

    x = xl_ref[0]
    xc = cw_ref[2:3, :] * x + cb_ref[...]
    xc = xc + cw_ref[0:1, :] * shifted(x, -2)
    xc = xc + cw_ref[1:2, :] * shifted(x, -1)
    xc = xc + cw_ref[3:4, :] * shifted(x, 1)

    wcat = jnp.concatenate([wa_ref[0, 0], wi_ref[0, 0], wa_ref[1, 0], wi_ref[1, 0]], axis=1).astype(_BF)
    gates = jnp.dot(xc.astype(_BF), wcat, preferred_element_type=_F32)

    nlam = -lam_ref[...]
    softplus = jnp.maximum(nlam, 0.0) + jnp.log1p(jnp.exp(-jnp.abs(nlam)))

    for d in range(2):
        ra = gates[:, d * 256:d * 256 + HEAD_DIM] + ba_ref[d:d + 1, :]
        ia = gates[:, d * 256 + HEAD_DIM:(d + 1) * 256] + bi_ref[d:d + 1, :]
        r = _sigmoid(ra)
        ig = _sigmoid(ia)
        log_a = -LRU_C * r * softplus[d:d + 1, :]
        a = jnp.exp(log_a)
        th = jnp.tanh(log_a)
        q = jnp.maximum(-2.0 * th / (1.0 - th), 0.0)
        u = jnp.sqrt(q) * (ig * xc)
        for s in range(N_CHUNK):
            a_scr[d, s * PITCH:s * PITCH + CHUNK, :] = a[s * CHUNK:(s + 1) * CHUNK]
            u_scr[d, s * PITCH:s * PITCH + CHUNK, :] = u[s * CHUNK:(s + 1) * CHUNK]

    def strided(t):
        return pl.ds(t, N_CHUNK, stride=PITCH)

    @pl.when(jnp.logical_not(is_lat))
    def _():
        hc_scr[...] = jnp.zeros_like(hc_scr)

    @pl.when(is_lat)
    def _():
        def body(t, c):
            pf, ef, pb, eb = c
            af = a_scr[0, strided(t), :]
            ef = af * ef + u_scr[0, strided(t), :]
            pf = pf * af
            tb = CHUNK - 1 - t
            ab = a_scr[1, strided(tb), :]
            eb = ab * eb + u_scr[1, strided(tb), :]
            pb = pb * ab
            return pf, ef, pb, eb

        one = jnp.ones((N_CHUNK, HEAD_DIM), _F32)
        zero = jnp.zeros((N_CHUNK, HEAD_DIM), _F32)
        pf, ef, pb, eb = lax.fori_loop(0, CHUNK, body, (one, zero, one, zero), unroll=8)
        pe_scr[0] = pf
        pe_scr[1] = ef
        pe_scr[2] = pb
        pe_scr[3] = eb
        seq = g - N_CTX_GROUPS
        h = h0_ref[0, pl.ds(seq, 1), :]
        for c in range(N_CHUNK):
            hc_scr[0, c:c + 1, :] = h
            h = pe_scr[0, c:c + 1, :] * h + pe_scr[1, c:c + 1, :]
        h = h0_ref[1, pl.ds(seq, 1), :]
        for c in reversed(range(N_CHUNK)):
            hc_scr[1, c:c + 1, :] = h
            h = pe_scr[2, c:c + 1, :] * h + pe_scr[3, c:c + 1, :]

    def scan_body(t, c):
        hf, hb = c
        hf = a_scr[0, strided(t), :] * hf + u_scr[0, strided(t), :]
        u_scr[0, strided(t), :] = hf
        tb = CHUNK - 1 - t
        hb = a_scr[1, strided(tb), :] * hb + u_scr[1, strided(tb), :]
        u_scr[1, strided(tb), :] = hb
        return hf, hb

    hf, hb = lax.fori_loop(0, CHUNK, scan_body, (hc_scr[0], hc_scr[1]), unroll=8)
    st_ref[0] = hf
    st_ref[1] = hb

    y = yg_ref[0]
    gelu = 0.5 * y * (1.0 + jnp.tanh(math.sqrt(2.0 / math.pi) * (y + 0.044715 * (y * y * y))))
    for s in range(N_CHUNK):
        o = u_scr[0, s * PITCH:s * PITCH + CHUNK, :] + u_scr[1, s * PITCH:s * PITCH + CHUNK, :]
        lru_ref[s * CHUNK:(s + 1) * CHUNK, :] = (o * gelu[s * CHUNK:(s + 1) * CHUNK]).astype(_BF)

    cv = cg_ref[0] * v_ref[0]
    conv = scw_ref[1:2, :] * cv
    conv = conv + scw_ref[0:1, :] * shifted(cv, -1)
    conv = conv + scw_ref[2:3, :] * shifted(cv, 1)
    sc_ref[...] = (bg_ref[0] * conv).astype(_BF)


def _mixer(layer, z, cw, cb, w_a, w_i, b_a, b_i, lam, scw, h0):
    def zspec(off):
        return pl.BlockSpec((1, GROUP_ROWS, HEAD_DIM), lambda g, h: (off + h, g, 0))

    def vec(rows):
        return pl.BlockSpec((None, rows, HEAD_DIM), lambda g, h: (layer, 0, h))

    wspec = pl.BlockSpec((None, 2, 1, HEAD_DIM, HEAD_DIM), lambda g, h: (layer, 0, h, 0, 0))
    return pl.pallas_call(
        _mixer_kernel,
        grid=(N_GROUPS, LRU_HEADS),
        in_specs=[
            zspec(0), zspec(8), zspec(16), zspec(24), zspec(32),
            vec(LRU_CONV), vec(1), wspec, wspec, vec(2), vec(2), vec(2), vec(SHORT_CONV),
            pl.BlockSpec((2, 8, HEAD_DIM), lambda g, h: (0, 0, h)),
        ],
        out_specs=[
            pl.BlockSpec((GROUP_ROWS, HEAD_DIM), lambda g, h: (g, h)),
            pl.BlockSpec((GROUP_ROWS, HEAD_DIM), lambda g, h: (g, h)),
            pl.BlockSpec((2, N_CHUNK, HEAD_DIM), lambda g, h: (0, g, h)),
        ],
        out_shape=[
            jax.ShapeDtypeStruct((ROWS, LRU_WIDTH), _BF),
            jax.ShapeDtypeStruct((ROWS, CONV_WIDTH), _BF),
            jax.ShapeDtypeStruct((2, N_GROUPS * N_CHUNK, LRU_WIDTH), _F32),
        ],
        scratch_shapes=[
            pltpu.VMEM((2, N_CHUNK * PITCH, HEAD_DIM), _F32),
            pltpu.VMEM((2, N_CHUNK * PITCH, HEAD_DIM), _F32),
            pltpu.VMEM((4, N_CHUNK, HEAD_DIM), _F32),
            pltpu.VMEM((2, N_CHUNK, HEAD_DIM), _F32),
        ],
        compiler_params=_cparams(("arbitrary", "arbitrary")),
        name=f"mixer{layer}",
    )(z, z, z, z, z, cw, cb, w_a, w_i, b_a, b_i, lam, scw, h0)


OUT_TM = 512


def _outproj_kernel(lru_ref, sc_ref, x_ref, w_ref, mod_ref, g_ref, x1_ref, hn2_ref):
    grp = _row_group(pl.program_id(0), OUT_TM)
    m = jnp.dot(lru_ref[...], w_ref[0:LRU_WIDTH, :], preferred_element_type=_F32)
    m = m + jnp.dot(sc_ref[...], w_ref[LRU_WIDTH:, :], preferred_element_type=_F32)
    x1 = x_ref[...] + _mod_row(mod_ref, 2, grp) * m
    x1_ref[...] = x1
    y = _rms(x1) * g_ref[1:2, :]
    hn2_ref[...] = (y * (1.0 + _mod_row(mod_ref, 4, grp)) + _mod_row(mod_ref, 3, grp)).astype(_BF)


def _outproj(layer, lru, sc, x, w_out_b, mod, norm_g):
    row = lambda i: (i, 0)
    return pl.pallas_call(
        _outproj_kernel,
        grid=(ROWS // OUT_TM,),
        in_specs=[
            pl.BlockSpec((OUT_TM, LRU_WIDTH), row),
            pl.BlockSpec((OUT_TM, CONV_WIDTH), row),
            pl.BlockSpec((OUT_TM, D_MODEL), row),
            pl.BlockSpec((None, D_MODEL, D_MODEL), lambda i: (layer, 0, 0)),
            pl.BlockSpec((None, N_MOD, 8, D_MODEL), lambda i: (layer, 0, 0, 0)),
            pl.BlockSpec((None, 2, D_MODEL), lambda i: (layer, 0, 0)),
        ],
        out_specs=[pl.BlockSpec((OUT_TM, D_MODEL), row), pl.BlockSpec((OUT_TM, D_MODEL), row)],
        out_shape=[
            jax.ShapeDtypeStruct((ROWS, D_MODEL), _F32),
            jax.ShapeDtypeStruct((ROWS, D_MODEL), _BF),
        ],
        compiler_params=_cparams(("arbitrary",)),
        name=f"outproj{layer}",
    )(lru, sc, x, w_out_b, mod, norm_g)


FFN_TM = 512
FFN_TF = 512
FFN_STEPS = D_FF // FFN_TF


def _ffn_accumulate(hn_ref, wg_ref, wu_ref, wd_ref, acc_ref):
    @pl.when(pl.program_id(1) == 0)
    def _():
        acc_ref[...] = jnp.zeros_like(acc_ref)

    h = hn_ref[...]
    gate = jnp.dot(h, wg_ref[...], preferred_element_type=_F32)
    up = jnp.dot(h, wu_ref[...], preferred_element_type=_F32)
    act = (gate * _sigmoid(gate) * up).astype(_BF)
    acc_ref[...] += jnp.dot(act, wd_ref[...], preferred_element_type=_F32)


def _ffn_mid_kernel(hn_ref, x1_ref, wg_ref, wu_ref, wd_ref, mod_ref, modn_ref, gn_ref, x2_ref, hnn_ref, acc_ref):
    _ffn_accumulate(hn_ref, wg_ref, wu_ref, wd_ref, acc_ref)

    @pl.when(pl.program_id(1) == FFN_STEPS - 1)
    def _():
        grp = _row_group(pl.program_id(0), FFN_TM)
        x2 = x1_ref[...] + _mod_row(mod_ref, 5, grp) * acc_ref[...]
        x2_ref[...] = x2
        y = _rms(x2) * gn_ref[0:1, :]
        hnn_ref[...] = (y * (1.0 + _mod_row(modn_ref, 1, grp)) + _mod_row(modn_ref, 0, grp)).astype(_BF)


def _ffn_last_kernel(hn_ref, x1_ref, wg_ref, wu_ref, wd_ref, mod_ref, fg_ref, yp_ref, ys_ref, acc_ref):
    _ffn_accumulate(hn_ref, wg_ref, wu_ref, wd_ref, acc_ref)
    i = pl.program_id(0)
    last = pl.program_id(1) == FFN_STEPS - 1
    is_lat = i >= CTX_ROWS // FFN_TM

    def result():
        grp = _row_group(i, FFN_TM)
        x2 = x1_ref[...] + _mod_row(mod_ref, 5, grp) * acc_ref[...]
        return _rms(x2) * fg_ref[...]

    @pl.when(last & jnp.logical_not(is_lat))
    def _():
        yp_ref[...] = result()

    @pl.when(last & is_lat)
    def _():
        ys_ref[...] = result()


def _ffn_specs(layer):
    return [
        pl.BlockSpec((FFN_TM, D_MODEL), lambda i, f: (i, 0)),
        pl.BlockSpec((FFN_TM, D_MODEL), lambda i, f: (i, 0)),
        pl.BlockSpec((None, D_MODEL, FFN_TF), lambda i, f: (layer, 0, f)),
        pl.BlockSpec((None, D_MODEL, FFN_TF), lambda i, f: (layer, 0, f)),
        pl.BlockSpec((None, FFN_TF, D_MODEL), lambda i, f: (layer, f, 0)),
        pl.BlockSpec((None, N_MOD, 8, D_MODEL), lambda i, f: (layer, 0, 0, 0)),
    ]


def _ffn_mid(layer, hn2, x1, wg_b, wu_b, wd_b, mod, norm_g):
    row = lambda i, f: (i, 0)
    return pl.pallas_call(
        _ffn_mid_kernel,
        grid=(ROWS // FFN_TM, FFN_STEPS),
        in_specs=_ffn_specs(layer) + [
            pl.BlockSpec((None, N_MOD, 8, D_MODEL), lambda i, f: (layer + 1, 0, 0, 0)),
            pl.BlockSpec((None, 2, D_MODEL), lambda i, f: (layer + 1, 0, 0)),
        ],
        out_specs=[pl.BlockSpec((FFN_TM, D_MODEL), row), pl.BlockSpec((FFN_TM, D_MODEL), row)],
        out_shape=[
            jax.ShapeDtypeStruct((ROWS, D_MODEL), _F32),
            jax.ShapeDtypeStruct((ROWS, D_MODEL), _BF),
        ],
        scratch_shapes=[pltpu.VMEM((FFN_TM, D_MODEL), _F32)],
        compiler_params=_cparams(("arbitrary", "arbitrary")),
        name=f"ffn{layer}",
    )(hn2, x1, wg_b, wu_b, wd_b, mod, mod, norm_g)


def _ffn_last(layer, hn2, x1, wg_b, wu_b, wd_b, mod, final_g):
    nct = CTX_ROWS // FFN_TM
    return pl.pallas_call(
        _ffn_last_kernel,
        grid=(ROWS // FFN_TM, FFN_STEPS),
        in_specs=_ffn_specs(layer) + [pl.BlockSpec((1, D_MODEL), lambda i, f: (0, 0))],
        out_specs=[
            pl.BlockSpec((FFN_TM, D_MODEL), lambda i, f: (jnp.minimum(i, nct - 1), 0)),
            pl.BlockSpec((FFN_TM, D_MODEL), lambda i, f: (jnp.maximum(i - nct, 0), 0)),
        ],
        out_shape=[
            jax.ShapeDtypeStruct((CTX_ROWS, D_MODEL), _F32),
            jax.ShapeDtypeStruct((LAT_ROWS, D_MODEL), _F32),
        ],
        scratch_shapes=[pltpu.VMEM((FFN_TM, D_MODEL), _F32)],
        compiler_params=_cparams(("arbitrary", "arbitrary")),
        name=f"ffn{layer}",
    )(hn2, x1, wg_b, wu_b, wd_b, mod, final_g.reshape(1, D_MODEL))


def _pos_table():
    rows = DEC_SEQ // GRID_W
    r = np.repeat(np.arange(rows, dtype=np.float32), GRID_W)
    col = np.tile(np.arange(GRID_W, dtype=np.float32), rows)
    n_freq = D_MODEL // 4
    freqs = (1.0 / (np.float32(POS_BASE) ** (np.arange(n_freq, dtype=np.float32) / np.float32(n_freq)))).astype(np.float32)
    er = r[:, None] * freqs
    ec = col[:, None] * freqs
    return np.concatenate([np.sin(er), np.cos(er), np.sin(ec), np.cos(ec)], axis=-1).astype(np.float32)


def kernel(x_prompt, x_sample, state_rglru, c, c_ctx, w_mod, b_mod, norm_g, w_in, lru_conv_w, lru_conv_b,
           lru_w_a, lru_b_a, lru_w_i, lru_b_i, lru_lambda, sc_conv_w, w_out, w_gate, w_up, w_down, final_g):
    w_in_b = w_in.astype(_BF)
    w_out_b = w_out.astype(_BF)
    wg_b = w_gate.astype(_BF)
    wu_b = w_up.astype(_BF)
    wd_b = w_down.astype(_BF)

    c8 = jnp.concatenate([c_ctx[None, :], c, jnp.zeros((8 - 1 - DEC_BATCH, D_MODEL), _F32)], axis=0)
    mod = _modulation(c8, w_mod, b_mod)

    pos = jnp.asarray(_pos_table())
    hn, x = _prenorm(x_prompt.reshape(CTX_ROWS, D_MODEL), x_sample.reshape(LAT_ROWS, D_MODEL), pos, mod, norm_g)

    cb = lru_conv_b.reshape(DEPTH, 1, LRU_WIDTH)
    b_a = lru_b_a.reshape(DEPTH, 2, LRU_WIDTH)
    b_i = lru_b_i.reshape(DEPTH, 2, LRU_WIDTH)
    h0 = jnp.pad(jnp.transpose(state_rglru, (1, 2, 0, 3)), ((0, 0), (0, 0), (0, 8 - DEC_BATCH), (0, 0)))

    states = []
    for layer in range(DEPTH):
        z = _inproj(layer, hn, w_in_b)
        lru, sc, st = _mixer(layer, z, lru_conv_w, cb, lru_w_a, lru_w_i, b_a, b_i, lru_lambda, sc_conv_w, h0[layer])
        states.append(st[:, :BATCH, :])
        x1, hn2 = _outproj(layer, lru, sc, x, w_out_b, mod, norm_g)
        if layer + 1 < DEPTH:
            x, hn = _ffn_mid(layer, hn2, x1, wg_b, wu_b, wd_b, mod, norm_g)
        else:
            y_p, y_s = _ffn_last(layer, hn2, x1, wg_b, wu_b, wd_b, mod, final_g)

    new_state = jnp.transpose(jnp.stack(states, axis=0), (2, 0, 1, 3))
    return (y_p.reshape(BATCH, SEQ, D_MODEL), y_s.reshape(DEC_BATCH, DEC_SEQ, D_MODEL), new_state)
```

```python
import math

import numpy as np
import jax
import jax.numpy as jnp
from jax import lax
from jax.experimental import pallas as pl
from jax.experimental.pallas import tpu as pltpu

D_MODEL = 2048
BATCH = 32
SEQ = 256
DEPTH = 2
DEC_BATCH = 2
DEC_SEQ = 2048
GRID_W = 64
LRU_WIDTH = 1024
LRU_HEADS = 8
HEAD_DIM = 128
LRU_CONV = 4
LRU_C = 8.0
CONV_WIDTH = 1024
SHORT_CONV = 3
D_IN = 2 * LRU_WIDTH + 3 * CONV_WIDTH
D_FF = 5632
N_MOD = 6
EPS = 1e-6
POS_BASE = 10000.0

CTX_ROWS = BATCH * SEQ
LAT_ROWS = DEC_BATCH * DEC_SEQ
ROWS = CTX_ROWS + LAT_ROWS
N_SLAB = D_IN // HEAD_DIM

GROUP_ROWS = 2048
CHUNK = 256
N_CHUNK = GROUP_ROWS // CHUNK
N_GROUPS = ROWS // GROUP_ROWS
N_CTX_GROUPS = CTX_ROWS // GROUP_ROWS
SUBLANES = 8
PITCH = CHUNK + SUBLANES
XPITCH = CHUNK + 2 * SUBLANES

V7X_VMEM_LIMIT = 56 * 1024 * 1024
F32_TINY = float(np.finfo(np.float32).tiny)
GELU_C = math.sqrt(2.0 / math.pi)
GELU_C3 = 0.044715 * GELU_C

_BF = jnp.bfloat16
_F32 = jnp.float32


def _cparams(sem):
    return pltpu.CompilerParams(dimension_semantics=sem, vmem_limit_bytes=V7X_VMEM_LIMIT)


def _sigmoid(x):
    return 0.5 * jnp.tanh(0.5 * x) + 0.5


def _row_group(tile, rows_per_tile):
    return jnp.maximum((tile * rows_per_tile) // DEC_SEQ - CTX_ROWS // DEC_SEQ + 1, 0)


def _mod_row(mod_ref, j, grp):
    return mod_ref[j, pl.ds(grp, 1), :]


def _rms(x):
    return x * lax.rsqrt(jnp.mean(x * x, axis=-1, keepdims=True) + EPS)


MOD_TN = 1024


def _mod_kernel(c_ref, w_ref, b_ref, o_ref):
    c = c_ref[...]
    s = (c * _sigmoid(c)).astype(_BF)
    o_ref[...] = jnp.dot(s, w_ref[...].astype(_BF), preferred_element_type=_F32) + b_ref[...]


def _modulation(c8, w_mod, b_mod):
    per = D_MODEL // MOD_TN
    b4 = b_mod.reshape(DEPTH, N_MOD, 1, D_MODEL)
    return pl.pallas_call(
        _mod_kernel,
        grid=(DEPTH, N_MOD * per),
        in_specs=[
            pl.BlockSpec((8, D_MODEL), lambda l, n: (0, 0)),
            pl.BlockSpec((None, D_MODEL, MOD_TN), lambda l, n: (l, 0, n)),
            pl.BlockSpec((None, None, 1, MOD_TN), lambda l, n: (l, n // per, 0, n % per)),
        ],
        out_specs=pl.BlockSpec((None, None, 8, MOD_TN), lambda l, n: (l, n // per, 0, n % per)),
        out_shape=jax.ShapeDtypeStruct((DEPTH, N_MOD, 8, D_MODEL), _F32),
        compiler_params=_cparams(("arbitrary", "arbitrary")),
        name="modulation",
    )(c8, w_mod, b4)


PRE_TM = 512


def _prenorm_kernel(xp_ref, xs_ref, pos_ref, mod_ref, g_ref, hn_ref, x0_ref):
    i = pl.program_id(0)
    is_lat = i >= CTX_ROWS // PRE_TM
    grp = _row_group(i, PRE_TM)
    x = jnp.where(is_lat, xs_ref[...] + pos_ref[...], xp_ref[...])
    x0_ref[...] = x
    y = _rms(x) * g_ref[0:1, :]
    hn = y * (1.0 + _mod_row(mod_ref, 1, grp)) + _mod_row(mod_ref, 0, grp)
    hn_ref[...] = hn.astype(_BF)


def _prenorm(xp, xs, pos, mod, norm_g):
    nct = CTX_ROWS // PRE_TM
    npos = DEC_SEQ // PRE_TM
    return pl.pallas_call(
        _prenorm_kernel,
        grid=(ROWS // PRE_TM,),
        in_specs=[
            pl.BlockSpec((PRE_TM, D_MODEL), lambda i: (jnp.minimum(i, nct - 1), 0)),
            pl.BlockSpec((PRE_TM, D_MODEL), lambda i: (jnp.maximum(i - nct, 0), 0)),
            pl.BlockSpec((PRE_TM, D_MODEL), lambda i: (jnp.maximum(i - nct, 0) % npos, 0)),
            pl.BlockSpec((None, N_MOD, 8, D_MODEL), lambda i: (0, 0, 0, 0)),
            pl.BlockSpec((None, 2, D_MODEL), lambda i: (0, 0, 0)),
        ],
        out_specs=[
            pl.BlockSpec((PRE_TM, D_MODEL), lambda i: (i, 0)),
            pl.BlockSpec((PRE_TM, D_MODEL), lambda i: (i, 0)),
        ],
        out_shape=[
            jax.ShapeDtypeStruct((ROWS, D_MODEL), _BF),
            jax.ShapeDtypeStruct((ROWS, D_MODEL), _F32),
        ],
        compiler_params=_cparams(("arbitrary",)),
        name="prenorm",
    )(xp, xs, pos, mod, norm_g)


IN_TM = 1024
IN_TN = 1024


def _inproj_kernel(hn_ref, w_ref, z_ref, wb_scr):
    @pl.when(pl.program_id(1) == 0)
    def _():
        wb_scr[...] = w_ref[...].astype(_BF)

    acc = jnp.dot(hn_ref[...], wb_scr[...], preferred_element_type=_F32)
    for j in range(IN_TN // HEAD_DIM):
        z_ref[j] = acc[:, j * HEAD_DIM:(j + 1) * HEAD_DIM]


def _inproj(layer, hn, w_in):
    return pl.pallas_call(
        _inproj_kernel,
        grid=(D_IN // IN_TN, ROWS // IN_TM),
        in_specs=[
            pl.BlockSpec((IN_TM, D_MODEL), lambda n, i: (i, 0)),
            pl.BlockSpec((None, D_MODEL, IN_TN), lambda n, i: (layer, 0, n)),
        ],
        out_specs=pl.BlockSpec((IN_TN // HEAD_DIM, IN_TM, HEAD_DIM), lambda n, i: (n, i, 0)),
        out_shape=jax.ShapeDtypeStruct((N_SLAB, ROWS, HEAD_DIM), _F32),
        scratch_shapes=[pltpu.VMEM((D_MODEL, IN_TN), _BF)],
        compiler_params=_cparams(("arbitrary", "arbitrary")),
        name=f"inproj{layer}",
    )(hn, w_in)


def _mixer_kernel(xl_ref, yg_ref, bg_ref, cg_ref, v_ref, cw_ref, cb_ref, wa_ref, wi_ref, ba_ref, bi_ref,
                  lam_ref, scw_ref, h0_ref, lru_ref, sc_ref, st_ref,
                  a_scr, u_scr, h_scr, pad_scr, pe_scr, hc_scr):
    g = pl.program_id(0)
    is_lat = g >= N_CTX_GROUPS

    def fill_padded(rows):
        zeros = jnp.zeros((SUBLANES, HEAD_DIM), _F32)
        for s in range(N_CHUNK):
            base = s * XPITCH
            lo = s * CHUNK
            front = jnp.where(is_lat, rows(lo - SUBLANES, SUBLANES), 0.0) if s > 0 else zeros
            back = jnp.where(is_lat, rows(lo + CHUNK, SUBLANES), 0.0) if s + 1 < N_CHUNK else zeros
            pad_scr[base:base + SUBLANES, :] = front
            pad_scr[base + SUBLANES:base + SUBLANES + CHUNK, :] = rows(lo, CHUNK)
            pad_scr[base + SUBLANES + CHUNK:base + XPITCH, :] = back

    def tap(s, k):
        start = s * XPITCH + SUBLANES + k
        return pad_scr[start:start + CHUNK, :]

    fill_padded(lambda lo, n: xl_ref[0, lo:lo + n, :])
    wcat = (0.5 * jnp.concatenate([wa_ref[0, 0], wi_ref[0, 0], wa_ref[1, 0], wi_ref[1, 0]], axis=1)).astype(_BF)
    half_ba = 0.5 * ba_ref[...]
    half_bi = 0.5 * bi_ref[...]
    nlam = -lam_ref[...]
    softplus = jnp.maximum(nlam, 0.0) + jnp.log1p(jnp.exp(-jnp.abs(nlam)))
    half_rate = (-0.5 * LRU_C) * softplus

    for s in range(N_CHUNK):
        xc = (cw_ref[2:3, :] * xl_ref[0, s * CHUNK:(s + 1) * CHUNK, :] + cb_ref[...]
              + cw_ref[0:1, :] * tap(s, -2) + cw_ref[1:2, :] * tap(s, -1) + cw_ref[3:4, :] * tap(s, 1))
        gates = jnp.dot(xc.astype(_BF), wcat, preferred_element_type=_F32)
        for d in range(2):
            half_ra = gates[:, d * 256:d * 256 + HEAD_DIM] + half_ba[d:d + 1, :]
            half_ia = gates[:, d * 256 + HEAD_DIM:(d + 1) * 256] + half_bi[d:d + 1, :]
            log_a = half_rate[d:d + 1, :] * (1.0 + jnp.tanh(half_ra))
            a = jnp.exp(log_a)
            th = jnp.tanh(log_a)
            q = (-0.5 * th) / (1.0 - th)
            root = jnp.maximum(q, 0.0) * lax.rsqrt(jnp.maximum(q, F32_TINY))
            a_scr[d, s * PITCH:s * PITCH + CHUNK, :] = a
            u_scr[d, s * PITCH:s * PITCH + CHUNK, :] = root * ((1.0 + jnp.tanh(half_ia)) * xc)

    def strided(t):
        return pl.ds(t, N_CHUNK, stride=PITCH)

    @pl.when(jnp.logical_not(is_lat))
    def _():
        hc_scr[...] = jnp.zeros_like(hc_scr)

    @pl.when(is_lat)
    def _():
        def body(t, c):
            pf, ef, pb, eb = c
            af = a_scr[0, strided(t), :]
            ef = af * ef + u_scr[0, strided(t), :]
            pf = pf * af
            tb = CHUNK - 1 - t
            ab = a_scr[1, strided(tb), :]
            eb = ab * eb + u_scr[1, strided(tb), :]
            pb = pb * ab
            return pf, ef, pb, eb

        one = jnp.ones((N_CHUNK, HEAD_DIM), _F32)
        zero = jnp.zeros((N_CHUNK, HEAD_DIM), _F32)
        pf, ef, pb, eb = lax.fori_loop(0, CHUNK, body, (one, zero, one, zero), unroll=8)
        pe_scr[0] = pf
        pe_scr[1] = ef
        pe_scr[2] = pb
        pe_scr[3] = eb
        seq = g - N_CTX_GROUPS
        h = h0_ref[0, pl.ds(seq, 1), :]
        for c in range(N_CHUNK):
            hc_scr[0, c:c + 1, :] = h
            h = pe_scr[0, c:c + 1, :] * h + pe_scr[1, c:c + 1, :]
        h = h0_ref[1, pl.ds(seq, 1), :]
        for c in reversed(range(N_CHUNK)):
            hc_scr[1, c:c + 1, :] = h
            h = pe_scr[2, c:c + 1, :] * h + pe_scr[3, c:c + 1, :]

    def scan_body(t, c):
        hf, hb = c
        hf = a_scr[0, strided(t), :] * hf + u_scr[0, strided(t), :]
        h_scr[0, strided(t), :] = hf
        tb = CHUNK - 1 - t
        hb = a_scr[1, strided(tb), :] * hb + u_scr[1, strided(tb), :]
        h_scr[1, strided(tb), :] = hb
        return hf, hb

    hf, hb = lax.fori_loop(0, CHUNK, scan_body, (hc_scr[0], hc_scr[1]), unroll=8)
    st_ref[0] = hf
    st_ref[1] = hb

    for s in range(N_CHUNK):
        y = yg_ref[0, s * CHUNK:(s + 1) * CHUNK, :]
        gelu = (0.5 * y) * (1.0 + jnp.tanh(y * (GELU_C + GELU_C3 * (y * y))))
        o = h_scr[0, s * PITCH:s * PITCH + CHUNK, :] + h_scr[1, s * PITCH:s * PITCH + CHUNK, :]
        lru_ref[s * CHUNK:(s + 1) * CHUNK, :] = (o * gelu).astype(_BF)

    fill_padded(lambda lo, n: cg_ref[0, lo:lo + n, :] * v_ref[0, lo:lo + n, :])
    for s in range(N_CHUNK):
        conv = scw_ref[1:2, :] * tap(s, 0) + scw_ref[0:1, :] * tap(s, -1) + scw_ref[2:3, :] * tap(s, 1)
        sc_ref[s * CHUNK:(s + 1) * CHUNK, :] = (bg_ref[0, s * CHUNK:(s + 1) * CHUNK, :] * conv).astype(_BF)


def _mixer(layer, z, cw, cb, w_a, w_i, b_a, b_i, lam, scw, h0):
    def zspec(off):
        return pl.BlockSpec((1, GROUP_ROWS, HEAD_DIM), lambda g, h: (off + h, g, 0))

    def vec(rows):
        return pl.BlockSpec((None, rows, HEAD_DIM), lambda g, h: (layer, 0, h))

    wspec = pl.BlockSpec((None, 2, 1, HEAD_DIM, HEAD_DIM), lambda g, h: (layer, 0, h, 0, 0))
    scan_buf = pltpu.VMEM((2, N_CHUNK * PITCH, HEAD_DIM), _F32)
    return pl.pallas_call(
        _mixer_kernel,
        grid=(N_GROUPS, LRU_HEADS),
        in_specs=[
            zspec(0), zspec(8), zspec(16), zspec(24), zspec(32),
            vec(LRU_CONV), vec(1), wspec, wspec, vec(2), vec(2), vec(2), vec(SHORT_CONV),
            pl.BlockSpec((2, 8, HEAD_DIM), lambda g, h: (0, 0, h)),
        ],
        out_specs=[
            pl.BlockSpec((GROUP_ROWS, HEAD_DIM), lambda g, h: (g, h)),
            pl.BlockSpec((GROUP_ROWS, HEAD_DIM), lambda g, h: (g, h)),
            pl.BlockSpec((2, N_CHUNK, HEAD_DIM), lambda g, h: (0, g, h)),
        ],
        out_shape=[
            jax.ShapeDtypeStruct((ROWS, LRU_WIDTH), _BF),
            jax.ShapeDtypeStruct((ROWS, CONV_WIDTH), _BF),
            jax.ShapeDtypeStruct((2, N_GROUPS * N_CHUNK, LRU_WIDTH), _F32),
        ],
        scratch_shapes=[
            scan_buf, scan_buf, scan_buf,
            pltpu.VMEM((N_CHUNK * XPITCH, HEAD_DIM), _F32),
            pltpu.VMEM((4, N_CHUNK, HEAD_DIM), _F32),
            pltpu.VMEM((2, N_CHUNK, HEAD_DIM), _F32),
        ],
        compiler_params=_cparams(("arbitrary", "arbitrary")),
        name=f"mixer{layer}",
    )(z, z, z, z, z, cw, cb, w_a, w_i, b_a, b_i, lam, scw, h0)


OUT_TM = 512


def _outproj_kernel(lru_ref, sc_ref, x_ref, w_ref, mod_ref, g_ref, x1_ref, hn2_ref, wb_scr):
    @pl.when(pl.program_id(0) == 0)
    def _():
        wb_scr[...] = w_ref[...].astype(_BF)

    grp = _row_group(pl.program_id(0), OUT_TM)
    m = jnp.dot(lru_ref[...], wb_scr[0:LRU_WIDTH, :], preferred_element_type=_F32)
    m = m + jnp.dot(sc_ref[...], wb_scr[LRU_WIDTH:, :], preferred_element_type=_F32)
    x1 = x_ref[...] + _mod_row(mod_ref, 2, grp) * m
    x1_ref[...] = x1
    y = _rms(x1) * g_ref[1:2, :]
    hn2_ref[...] = (y * (1.0 + _mod_row(mod_ref, 4, grp)) + _mod_row(mod_ref, 3, grp)).astype(_BF)


def _outproj(layer, lru, sc, x, w_out, mod, norm_g):
    row = lambda i: (i, 0)
    return pl.pallas_call(
        _outproj_kernel,
        grid=(ROWS // OUT_TM,),
        in_specs=[
            pl.BlockSpec((OUT_TM, LRU_WIDTH), row),
            pl.BlockSpec((OUT_TM, CONV_WIDTH), row),
            pl.BlockSpec((OUT_TM, D_MODEL), row),
            pl.BlockSpec((None, D_MODEL, D_MODEL), lambda i: (layer, 0, 0), pipeline_mode=pl.Buffered(1)),
            pl.BlockSpec((None, N_MOD, 8, D_MODEL), lambda i: (layer, 0, 0, 0)),
            pl.BlockSpec((None, 2, D_MODEL), lambda i: (layer, 0, 0)),
        ],
        out_specs=[pl.BlockSpec((OUT_TM, D_MODEL), row), pl.BlockSpec((OUT_TM, D_MODEL), row)],
        out_shape=[
            jax.ShapeDtypeStruct((ROWS, D_MODEL), _F32),
            jax.ShapeDtypeStruct((ROWS, D_MODEL), _BF),
        ],
        scratch_shapes=[pltpu.VMEM((D_MODEL, D_MODEL), _BF)],
        compiler_params=_cparams(("arbitrary",)),
        name=f"outproj{layer}",
    )(lru, sc, x, w_out, mod, norm_g)


FFN_TM = 512
FFN_TF = 512
FFN_STEPS = D_FF // FFN_TF


def _ffn_accumulate(hn_ref, wg_ref, wu_ref, wd_ref, acc_ref):
    @pl.when(pl.program_id(1) == 0)
    def _():
        acc_ref[...] = jnp.zeros_like(acc_ref)

    h = hn_ref[...]
    gate = jnp.dot(h, wg_ref[...], preferred_element_type=_F32)
    up = jnp.dot(h, wu_ref[...], preferred_element_type=_F32)
    act = (gate * _sigmoid(gate) * up).astype(_BF)
    acc_ref[...] += jnp.dot(act, wd_ref[...], preferred_element_type=_F32)


def _ffn_mid_kernel(hn_ref, x1_ref, wg_ref, wu_ref, wd_ref, mod_ref, modn_ref, gn_ref, x2_ref, hnn_ref, acc_ref):
    _ffn_accumulate(hn_ref, wg_ref, wu_ref, wd_ref, acc_ref)

    @pl.when(pl.program_id(1) == FFN_STEPS - 1)
    def _():
        grp = _row_group(pl.program_id(0), FFN_TM)
        x2 = x1_ref[...] + _mod_row(mod_ref, 5, grp) * acc_ref[...]
        x2_ref[...] = x2
        y = _rms(x2) * gn_ref[0:1, :]
        hnn_ref[...] = (y * (1.0 + _mod_row(modn_ref, 1, grp)) + _mod_row(modn_ref, 0, grp)).astype(_BF)


def _ffn_last_kernel(hn_ref, x1_ref, wg_ref, wu_ref, wd_ref, mod_ref, fg_ref, yp_ref, ys_ref, acc_ref):
    _ffn_accumulate(hn_ref, wg_ref, wu_ref, wd_ref, acc_ref)
    i = pl.program_id(0)
    last = pl.program_id(1) == FFN_STEPS - 1
    is_lat = i >= CTX_ROWS // FFN_TM

    def result():
        grp = _row_group(i, FFN_TM)
        x2 = x1_ref[...] + _mod_row(mod_ref, 5, grp) * acc_ref[...]
        return _rms(x2) * fg_ref[...]

    @pl.when(last & jnp.logical_not(is_lat))
    def _():
        yp_ref[...] = result()

    @pl.when(last & is_lat)
    def _():
        ys_ref[...] = result()


def _ffn_specs(layer):
    return [
        pl.BlockSpec((FFN_TM, D_MODEL), lambda i, f: (i, 0)),
        pl.BlockSpec((FFN_TM, D_MODEL), lambda i, f: (i, 0)),
        pl.BlockSpec((None, D_MODEL, FFN_TF), lambda i, f: (layer, 0, f)),
        pl.BlockSpec((None, D_MODEL, FFN_TF), lambda i, f: (layer, 0, f)),
        pl.BlockSpec((None, FFN_TF, D_MODEL), lambda i, f: (layer, f, 0)),
        pl.BlockSpec((None, N_MOD, 8, D_MODEL), lambda i, f: (layer, 0, 0, 0)),
    ]


def _ffn_mid(layer, hn2, x1, wg_b, wu_b, wd_b, mod, norm_g):
    row = lambda i, f: (i, 0)
    return pl.pallas_call(
        _ffn_mid_kernel,
        grid=(ROWS // FFN_TM, FFN_STEPS),
        in_specs=_ffn_specs(layer) + [
            pl.BlockSpec((None, N_MOD, 8, D_MODEL), lambda i, f: (layer + 1, 0, 0, 0)),
            pl.BlockSpec((None, 2, D_MODEL), lambda i, f: (layer + 1, 0, 0)),
        ],
        out_specs=[pl.BlockSpec((FFN_TM, D_MODEL), row), pl.BlockSpec((FFN_TM, D_MODEL), row)],
        out_shape=[
            jax.ShapeDtypeStruct((ROWS, D_MODEL), _F32),
            jax.ShapeDtypeStruct((ROWS, D_MODEL), _BF),
        ],
        scratch_shapes=[pltpu.VMEM((FFN_TM, D_MODEL), _F32)],
        compiler_params=_cparams(("arbitrary", "arbitrary")),
        name=f"ffn{layer}",
    )(hn2, x1, wg_b, wu_b, wd_b, mod, mod, norm_g)


def _ffn_last(layer, hn2, x1, wg_b, wu_b, wd_b, mod, final_g):
    nct = CTX_ROWS // FFN_TM
    return pl.pallas_call(
        _ffn_last_kernel,
        grid=(ROWS // FFN_TM, FFN_STEPS),
        in_specs=_ffn_specs(layer) + [pl.BlockSpec((1, D_MODEL), lambda i, f: (0, 0))],
        out_specs=[
            pl.BlockSpec((FFN_TM, D_MODEL), lambda i, f: (jnp.minimum(i, nct - 1), 0)),
            pl.BlockSpec((FFN_TM, D_MODEL), lambda i, f: (jnp.maximum(i - nct, 0), 0)),
        ],
        out_shape=[
            jax.ShapeDtypeStruct((CTX_ROWS, D_MODEL), _F32),
            jax.ShapeDtypeStruct((LAT_ROWS, D_MODEL), _F32),
        ],
        scratch_shapes=[pltpu.VMEM((FFN_TM, D_MODEL), _F32)],
        compiler_params=_cparams(("arbitrary", "arbitrary")),
        name=f"ffn{layer}",
    )(hn2, x1, wg_b, wu_b, wd_b, mod, final_g.reshape(1, D_MODEL))


def _pos_table():
    rows = DEC_SEQ // GRID_W
    r = np.repeat(np.arange(rows, dtype=np.float32), GRID_W)
    col = np.tile(np.arange(GRID_W, dtype=np.float32), rows)
    n_freq = D_MODEL // 4
    freqs = (1.0 / (np.float32(POS_BASE) ** (np.arange(n_freq, dtype=np.float32) / np.float32(n_freq)))).astype(np.float32)
    er = r[:, None] * freqs
    ec = col[:, None] * freqs
    return np.concatenate([np.sin(er), np.cos(er), np.sin(ec), np.cos(ec)], axis=-1).astype(np.float32)


def kernel(x_prompt, x_sample, state_rglru, c, c_ctx, w_mod, b_mod, norm_g, w_in, lru_conv_w, lru_conv_b,
           lru_w_a, lru_b_a, lru_w_i, lru_b_i, lru_lambda, sc_conv_w, w_out, w_gate, w_up, w_down, final_g):
    wg_b = w_gate.astype(_BF)
    wu_b = w_up.astype(_BF)
    wd_b = w_down.astype(_BF)

    c8 = jnp.concatenate([c_ctx[None, :], c, jnp.zeros((8 - 1 - DEC_BATCH, D_MODEL), _F32)], axis=0)
    mod = _modulation(c8, w_mod, b_mod)

    pos = jnp.asarray(_pos_table())
    hn, x = _prenorm(x_prompt.reshape(CTX_ROWS, D_MODEL), x_sample.reshape(LAT_ROWS, D_MODEL), pos, mod, norm_g)

    cb = lru_conv_b.reshape(DEPTH, 1, LRU_WIDTH)
    b_a = lru_b_a.reshape(DEPTH, 2, LRU_WIDTH)
    b_i = lru_b_i.reshape(DEPTH, 2, LRU_WIDTH)
    h0 = jnp.pad(jnp.transpose(state_rglru, (1, 2, 0, 3)), ((0, 0), (0, 0), (0, 8 - DEC_BATCH), (0, 0)))

    states = []
    for layer in range(DEPTH):
        z = _inproj(layer, hn, w_in)
        lru, sc, st = _mixer(layer, z, lru_conv_w, cb, lru_w_a, lru_w_i, b_a, b_i, lru_lambda, sc_conv_w, h0[layer])
        states.append(st[:, :BATCH, :])
        x1, hn2 = _outproj(layer, lru, sc, x, w_out, mod, norm_g)
        if layer + 1 < DEPTH:
            x, hn = _ffn_mid(layer, hn2, x1, wg_b, wu_b, wd_b, mod, norm_g)
        else:
            y_p, y_s = _ffn_last(layer, hn2, x1, wg_b, wu_b, wd_b, mod, final_g)

    new_state = jnp.transpose(jnp.stack(states, axis=0), (2, 0, 1, 3))
    return (y_p.reshape(BATCH, SEQ, D_MODEL), y_s.reshape(DEC_BATCH, DEC_SEQ, D_MODEL), new_state)
```

```python
import functools
import math

import numpy as np
import jax
import jax.numpy as jnp
from jax import lax
from jax.experimental import pallas as pl
from jax.experimental.pallas import tpu as pltpu

D_MODEL = 2048
BATCH = 32
SEQ = 256
DEPTH = 2
DEC_BATCH = 2
DEC_SEQ = 2048
GRID_W = 64
LRU_WIDTH = 1024
LRU_HEADS = 8
HEAD_DIM = 128
LRU_CONV = 4
LRU_C = 8.0
CONV_WIDTH = 1024
SHORT_CONV = 3
D_IN = 2 * LRU_WIDTH + 3 * CONV_WIDTH
D_FF = 5632
N_MOD = 6
EPS = 1e-6
POS_BASE = 10000.0

CTX_ROWS = BATCH * SEQ
LAT_ROWS = DEC_BATCH * DEC_SEQ
ROWS = CTX_ROWS + LAT_ROWS
N_SLAB = D_IN // HEAD_DIM

GROUP_ROWS = 2048
CHUNK = 256
N_CHUNK = GROUP_ROWS // CHUNK
N_GROUPS = ROWS // GROUP_ROWS
N_CTX_GROUPS = CTX_ROWS // GROUP_ROWS
SUBLANES = 8
PITCH = CHUNK + SUBLANES
XPITCH = CHUNK + 2 * SUBLANES

V7X_VMEM_LIMIT = 56 * 1024 * 1024
F32_TINY = float(np.finfo(np.float32).tiny)
GELU_C = math.sqrt(2.0 / math.pi)
GELU_C3 = 0.044715 * GELU_C

_BF = jnp.bfloat16
_F32 = jnp.float32


def _cparams(sem):
    return pltpu.CompilerParams(dimension_semantics=sem, vmem_limit_bytes=V7X_VMEM_LIMIT)


def _sigmoid(x):
    return 0.5 * jnp.tanh(0.5 * x) + 0.5


def _row_group(tile, rows_per_tile):
    return jnp.maximum((tile * rows_per_tile) // DEC_SEQ - CTX_ROWS // DEC_SEQ + 1, 0)


def _mod_row(mod_ref, j, grp):
    return mod_ref[j, pl.ds(grp, 1), :]


def _rms(x):
    return x * lax.rsqrt(jnp.mean(x * x, axis=-1, keepdims=True) + EPS)


MOD_TN = 1024


def _mod_kernel(c_ref, w_ref, b_ref, o_ref):
    c = c_ref[...]
    s = (c * _sigmoid(c)).astype(_BF)
    o_ref[...] = jnp.dot(s, w_ref[...].astype(_BF), preferred_element_type=_F32) + b_ref[...]


def _modulation(c8, w_mod, b_mod):
    per = D_MODEL // MOD_TN
    b4 = b_mod.reshape(DEPTH, N_MOD, 1, D_MODEL)
    return pl.pallas_call(
        _mod_kernel,
        grid=(DEPTH, N_MOD * per),
        in_specs=[
            pl.BlockSpec((8, D_MODEL), lambda l, n: (0, 0)),
            pl.BlockSpec((None, D_MODEL, MOD_TN), lambda l, n: (l, 0, n)),
            pl.BlockSpec((None, None, 1, MOD_TN), lambda l, n: (l, n // per, 0, n % per)),
        ],
        out_specs=pl.BlockSpec((None, None, 8, MOD_TN), lambda l, n: (l, n // per, 0, n % per)),
        out_shape=jax.ShapeDtypeStruct((DEPTH, N_MOD, 8, D_MODEL), _F32),
        compiler_params=_cparams(("arbitrary", "arbitrary")),
        name="modulation",
    )(c8, w_mod, b4)


PRE_TM = 512


def _prenorm_kernel(xp_ref, xs_ref, pos_ref, mod_ref, g_ref, hn_ref, x0_ref):
    i = pl.program_id(0)
    is_lat = i >= CTX_ROWS // PRE_TM
    grp = _row_group(i, PRE_TM)
    x = jnp.where(is_lat, xs_ref[...] + pos_ref[...], xp_ref[...])
    x0_ref[...] = x
    y = _rms(x) * g_ref[0:1, :]
    hn = y * (1.0 + _mod_row(mod_ref, 1, grp)) + _mod_row(mod_ref, 0, grp)
    hn_ref[...] = hn.astype(_BF)


def _prenorm(xp, xs, pos, mod, norm_g):
    nct = CTX_ROWS // PRE_TM
    npos = DEC_SEQ // PRE_TM
    return pl.pallas_call(
        _prenorm_kernel,
        grid=(ROWS // PRE_TM,),
        in_specs=[
            pl.BlockSpec((PRE_TM, D_MODEL), lambda i: (jnp.minimum(i, nct - 1), 0)),
            pl.BlockSpec((PRE_TM, D_MODEL), lambda i: (jnp.maximum(i - nct, 0), 0)),
            pl.BlockSpec((PRE_TM, D_MODEL), lambda i: (jnp.maximum(i - nct, 0) % npos, 0)),
            pl.BlockSpec((None, N_MOD, 8, D_MODEL), lambda i: (0, 0, 0, 0)),
            pl.BlockSpec((None, 2, D_MODEL), lambda i: (0, 0, 0)),
        ],
        out_specs=[
            pl.BlockSpec((PRE_TM, D_MODEL), lambda i: (i, 0)),
            pl.BlockSpec((PRE_TM, D_MODEL), lambda i: (i, 0)),
        ],
        out_shape=[
            jax.ShapeDtypeStruct((ROWS, D_MODEL), _BF),
            jax.ShapeDtypeStruct((ROWS, D_MODEL), _F32),
        ],
        compiler_params=_cparams(("arbitrary",)),
        name="prenorm",
    )(xp, xs, pos, mod, norm_g)


N_ITEMS = N_GROUPS * LRU_HEADS
Z_SLABS = D_IN // LRU_WIDTH
MXU_COLS = 256
MXU_ROWS = 512


def _inmix_kernel(hn_ref, w_ref, cw_ref, cb_ref, wa_ref, wi_ref, ba_ref, bi_ref, lam_ref, scw_ref, h0_ref,
                  lru_ref, sc_ref, st_ref,
                  za_scr, zb_scr, a_scr, u_scr, h_scr, pad_scr, xc_scr, xcb_scr, g_scr, pe_scr, hc_scr):
    k = pl.program_id(0)

    @pl.when(k == 0)
    def _():
        zb_scr[...] = jnp.zeros_like(zb_scr)

    def step(z_write, z_read):
        def project(r0, c0, c1):
            zz = jnp.dot(hn_ref[r0:r0 + MXU_ROWS, :], w_ref[:, c0:c1], preferred_element_type=_F32)
            for j in range(c0 // HEAD_DIM, c1 // HEAD_DIM):
                z_write[j, r0:r0 + MXU_ROWS, :] = zz[:, j * HEAD_DIM - c0:(j + 1) * HEAD_DIM - c0]
            return zz[MXU_ROWS - SUBLANES:, 0:HEAD_DIM]

        projections = []
        for c0 in range(0, Z_SLABS * HEAD_DIM, MXU_COLS):
            c1 = min(c0 + MXU_COLS, Z_SLABS * HEAD_DIM)
            for r0 in range(0, GROUP_ROWS, MXU_ROWS):
                projections.append((1, functools.partial(project, r0, c0, c1)))

        group = jnp.maximum(k - 1, 0) // LRU_HEADS
        before_gates, gates_matmul, after_gates = _mixer_tasks(
            k < 0, group, z_read, cw_ref, cb_ref, wa_ref, wi_ref, ba_ref, bi_ref, lam_ref, scw_ref, h0_ref,
            lru_ref, sc_ref, st_ref, a_scr, u_scr, h_scr, pad_scr, xc_scr, xcb_scr, g_scr, pe_scr, hc_scr)
        projections[0][1]()
        for _, task in before_gates:
            task(None)
        gates_matmul()
        _emit_interleaved(projections[1:], after_gates)

    @pl.when(k % 2 == 0)
    def _():
        step(za_scr, zb_scr)

    @pl.when(k % 2 == 1)
    def _():
        step(zb_scr, za_scr)


def _emit_interleaved(producers, consumers):
    totals = [sum(w for w, _ in tasks) for tasks in (producers, consumers)]
    totals[1] /= CONSUMER_SPAN
    streams = [list(producers), list(consumers)]
    done = [0.0, 0.0]
    token = None
    while streams[0] or streams[1]:
        pick = 0 if streams[0] and (not streams[1] or done[0] / totals[0] <= done[1] / totals[1]) else 1
        weight, fn = streams[pick].pop(0)
        done[pick] += weight
        if pick == 0:
            token = fn()
        else:
            fn(token)


SCAN_PIECE = 32
CONSUMER_SPAN = 1.0


def _mixer_tasks(never, g, z_ref, cw_ref, cb_ref, wa_ref, wi_ref, ba_ref, bi_ref, lam_ref, scw_ref, h0_ref,
                 lru_ref, sc_ref, st_ref, a_scr, u_scr, h_scr, pad_scr, xc_scr, xcb_scr, g_scr, pe_scr, hc_scr):
    is_lat = g >= N_CTX_GROUPS
    env = {}
    before = []
    tasks = []

    def held(x, token):
        return x if token is None else jnp.where(never, token[0:x.shape[0], :], x)

    def fill_padded(rows, token=None):
        zeros = jnp.zeros((SUBLANES, HEAD_DIM), _F32)
        for s in range(N_CHUNK):
            base = s * XPITCH
            lo = s * CHUNK
            front = jnp.where(is_lat, rows(lo - SUBLANES, SUBLANES), 0.0) if s > 0 else zeros
            back = jnp.where(is_lat, rows(lo + CHUNK, SUBLANES), 0.0) if s + 1 < N_CHUNK else zeros
            pad_scr[base:base + SUBLANES, :] = front
            pad_scr[base + SUBLANES:base + SUBLANES + CHUNK, :] = rows(lo, CHUNK)
            pad_scr[base + SUBLANES + CHUNK:base + XPITCH, :] = back

    def tap(s, k):
        start = s * XPITCH + SUBLANES + k
        return pad_scr[start:start + CHUNK, :]

    def gate_constants(token=None):
        wcat = 0.5 * jnp.concatenate([wa_ref[0, 0], wi_ref[0, 0], wa_ref[1, 0], wi_ref[1, 0]], axis=1)
        env["wcat"] = wcat.astype(_BF)
        env["half_ba"] = 0.5 * ba_ref[...]
        env["half_bi"] = 0.5 * bi_ref[...]
        nlam = -lam_ref[...]
        softplus = jnp.maximum(nlam, 0.0) + jnp.log1p(jnp.exp(-jnp.abs(nlam)))
        env["half_rate"] = (-0.5 * LRU_C) * softplus

    def conv_of_chunk(s, token=None):
        xc = (cw_ref[2:3, :] * z_ref[0, s * CHUNK:(s + 1) * CHUNK, :] + cb_ref[...]
              + cw_ref[0:1, :] * tap(s, -2) + cw_ref[1:2, :] * tap(s, -1) + cw_ref[3:4, :] * tap(s, 1))
        xc_scr[s * CHUNK:(s + 1) * CHUNK, :] = xc
        xcb_scr[s * CHUNK:(s + 1) * CHUNK, :] = xc.astype(_BF)

    def gates_matmul():
        for r0 in range(0, GROUP_ROWS, MXU_ROWS):
            g_scr[r0:r0 + MXU_ROWS, :] = jnp.dot(xcb_scr[r0:r0 + MXU_ROWS, :], env["wcat"],
                                                 preferred_element_type=_F32)

    def gates_of_chunk(s, token=None):
        xc = xc_scr[s * CHUNK:(s + 1) * CHUNK, :]
        gates = g_scr[s * CHUNK:(s + 1) * CHUNK, :]
        half_ba = held(env["half_ba"], token)
        half_bi = held(env["half_bi"], token)
        for d in range(2):
            half_ra = gates[:, d * 256:d * 256 + HEAD_DIM] + half_ba[d:d + 1, :]
            half_ia = gates[:, d * 256 + HEAD_DIM:(d + 1) * 256] + half_bi[d:d + 1, :]
            log_a = env["half_rate"][d:d + 1, :] * (1.0 + jnp.tanh(half_ra))
            a = jnp.exp(log_a)
            th = jnp.tanh(log_a)
            q = (-0.5 * th) / (1.0 - th)
            root = jnp.maximum(q, 0.0) * lax.rsqrt(jnp.maximum(q, F32_TINY))
            a_scr[d, s * PITCH:s * PITCH + CHUNK, :] = a
            u_scr[d, s * PITCH:s * PITCH + CHUNK, :] = root * ((1.0 + jnp.tanh(half_ia)) * xc)

    before.append((1, gate_constants))
    before.append((6, functools.partial(fill_padded, lambda lo, n: z_ref[0, lo:lo + n, :])))
    for s in range(N_CHUNK):
        before.append((2, functools.partial(conv_of_chunk, s)))

    def conv_out_of_chunk(s, token=None):
        scw = held(scw_ref[...], token)
        conv = scw[1:2, :] * tap(s, 0) + scw[0:1, :] * tap(s, -1) + scw[2:3, :] * tap(s, 1)
        sc_ref[s * CHUNK:(s + 1) * CHUNK, :] = (z_ref[2, s * CHUNK:(s + 1) * CHUNK, :] * conv).astype(_BF)

    tasks.append((7, functools.partial(fill_padded, lambda lo, n: z_ref[3, lo:lo + n, :] * z_ref[4, lo:lo + n, :])))
    for s in range(N_CHUNK):
        tasks.append((2, functools.partial(conv_out_of_chunk, s)))
    for s in range(N_CHUNK):
        tasks.append((8, functools.partial(gates_of_chunk, s)))

    def strided(t):
        return pl.ds(t, N_CHUNK, stride=PITCH)

    def probe_init(token=None):
        env["pf"] = env["pb"] = jnp.ones((N_CHUNK, HEAD_DIM), _F32)
        env["ef"] = env["eb"] = jnp.zeros((N_CHUNK, HEAD_DIM), _F32)

    def probe_piece(t0, token=None):
        pf, ef, pb, eb = held(env["pf"], token), env["ef"], held(env["pb"], token), env["eb"]
        for t in range(t0, t0 + SCAN_PIECE):
            tb = CHUNK - 1 - t
            af = a_scr[0, strided(t), :]
            ef = af * ef + u_scr[0, strided(t), :]
            pf = pf * af
            ab = a_scr[1, strided(tb), :]
            eb = ab * eb + u_scr[1, strided(tb), :]
            pb = pb * ab
        env["pf"], env["ef"], env["pb"], env["eb"] = pf, ef, pb, eb

    def chain_chunks(token=None):
        pe_scr[0] = env["pf"]
        pe_scr[1] = env["ef"]
        pe_scr[2] = env["pb"]
        pe_scr[3] = env["eb"]
        seq = jnp.maximum(g - N_CTX_GROUPS, 0)
        h = jnp.where(is_lat, h0_ref[0, pl.ds(seq, 1), :], 0.0)
        for c in range(N_CHUNK):
            hc_scr[0, c:c + 1, :] = h
            h = jnp.where(is_lat, pe_scr[0, c:c + 1, :] * h + pe_scr[1, c:c + 1, :], 0.0)
        h = jnp.where(is_lat, h0_ref[1, pl.ds(seq, 1), :], 0.0)
        for c in reversed(range(N_CHUNK)):
            hc_scr[1, c:c + 1, :] = h
            h = jnp.where(is_lat, pe_scr[2, c:c + 1, :] * h + pe_scr[3, c:c + 1, :], 0.0)
        env["hf"] = hc_scr[0]
        env["hb"] = hc_scr[1]

    def scan_piece(t0, token=None):
        hf, hb = held(env["hf"], token), held(env["hb"], token)
        for t in range(t0, t0 + SCAN_PIECE):
            tb = CHUNK - 1 - t
            hf = a_scr[0, strided(t), :] * hf + u_scr[0, strided(t), :]
            h_scr[0, strided(t), :] = hf
            hb = a_scr[1, strided(tb), :] * hb + u_scr[1, strided(tb), :]
            h_scr[1, strided(tb), :] = hb
        env["hf"], env["hb"] = hf, hb

    def store_states(token=None):
        st_ref[0] = env["hf"]
        st_ref[1] = env["hb"]

    def lru_out_of_chunk(s, token=None):
        y = z_ref[1, s * CHUNK:(s + 1) * CHUNK, :]
        half = held(jnp.full((1, HEAD_DIM), 0.5, _F32), token)
        gelu = (half * y) * (1.0 + jnp.tanh(y * (GELU_C + GELU_C3 * (y * y))))
        o = h_scr[0, s * PITCH:s * PITCH + CHUNK, :] + h_scr[1, s * PITCH:s * PITCH + CHUNK, :]
        lru_ref[s * CHUNK:(s + 1) * CHUNK, :] = (o * gelu).astype(_BF)

    tasks.append((0, probe_init))
    for t0 in range(0, CHUNK, SCAN_PIECE):
        tasks.append((4, functools.partial(probe_piece, t0)))
    tasks.append((1, chain_chunks))
    for t0 in range(0, CHUNK, SCAN_PIECE):
        tasks.append((5, functools.partial(scan_piece, t0)))
    tasks.append((0, store_states))
    for s in range(N_CHUNK):
        tasks.append((2, functools.partial(lru_out_of_chunk, s)))
    return before, gates_matmul, tasks


def _inmix(layer, hn, w_in_heads, cw, cb, w_a, w_i, b_a, b_i, lam, scw, h0):
    def in_item(k):
        return jnp.minimum(k, N_ITEMS - 1)

    def out_item(k):
        return jnp.maximum(k - 1, 0)

    def vec(rows):
        return pl.BlockSpec((None, rows, HEAD_DIM), lambda k: (layer, 0, out_item(k) % LRU_HEADS))

    gate_w = pl.BlockSpec((None, 2, 1, HEAD_DIM, HEAD_DIM), lambda k: (layer, 0, out_item(k) % LRU_HEADS, 0, 0))
    out_block = lambda k: (out_item(k) // LRU_HEADS, out_item(k) % LRU_HEADS)
    z_buf = pltpu.VMEM((Z_SLABS, GROUP_ROWS, HEAD_DIM), _F32)
    scan_buf = pltpu.VMEM((2, N_CHUNK * PITCH, HEAD_DIM), _F32)
    return pl.pallas_call(
        _inmix_kernel,
        grid=(N_ITEMS + 1,),
        in_specs=[
            pl.BlockSpec((GROUP_ROWS, D_MODEL), lambda k: (in_item(k) // LRU_HEADS, 0)),
            pl.BlockSpec((None, None, D_MODEL, Z_SLABS * HEAD_DIM),
                         lambda k: (layer, in_item(k) % LRU_HEADS, 0, 0)),
            vec(LRU_CONV), vec(1), gate_w, gate_w, vec(2), vec(2), vec(2), vec(SHORT_CONV),
            pl.BlockSpec((2, 8, HEAD_DIM), lambda k: (0, 0, out_item(k) % LRU_HEADS)),
        ],
        out_specs=[
            pl.BlockSpec((GROUP_ROWS, HEAD_DIM), out_block),
            pl.BlockSpec((GROUP_ROWS, HEAD_DIM), out_block),
            pl.BlockSpec((2, N_CHUNK, HEAD_DIM), lambda k: (0,) + out_block(k)),
        ],
        out_shape=[
            jax.ShapeDtypeStruct((ROWS, LRU_WIDTH), _BF),
            jax.ShapeDtypeStruct((ROWS, CONV_WIDTH), _BF),
            jax.ShapeDtypeStruct((2, N_GROUPS * N_CHUNK, LRU_WIDTH), _F32),
        ],
        scratch_shapes=[
            z_buf, z_buf, scan_buf, scan_buf, scan_buf,
            pltpu.VMEM((N_CHUNK * XPITCH, HEAD_DIM), _F32),
            pltpu.VMEM((GROUP_ROWS, HEAD_DIM), _F32),
            pltpu.VMEM((GROUP_ROWS, HEAD_DIM), _BF),
            pltpu.VMEM((GROUP_ROWS, 4 * HEAD_DIM), _F32),
            pltpu.VMEM((4, N_CHUNK, HEAD_DIM), _F32),
            pltpu.VMEM((2, N_CHUNK, HEAD_DIM), _F32),
        ],
        compiler_params=_cparams(("arbitrary",)),
        name=f"inmix{layer}",
    )(hn, w_in_heads, cw, cb, w_a, w_i, b_a, b_i, lam, scw, h0)


OUT_TM = 512


def _outproj_kernel(lru_ref, sc_ref, x_ref, w_ref, mod_ref, g_ref, x1_ref, hn2_ref, wb_scr):
    @pl.when(pl.program_id(0) == 0)
    def _():
        wb_scr[...] = w_ref[...].astype(_BF)

    grp = _row_group(pl.program_id(0), OUT_TM)
    m = jnp.dot(lru_ref[...], wb_scr[0:LRU_WIDTH, :], preferred_element_type=_F32)
    m = m + jnp.dot(sc_ref[...], wb_scr[LRU_WIDTH:, :], preferred_element_type=_F32)
    x1 = x_ref[...] + _mod_row(mod_ref, 2, grp) * m
    x1_ref[...] = x1
    y = _rms(x1) * g_ref[1:2, :]
    hn2_ref[...] = (y * (1.0 + _mod_row(mod_ref, 4, grp)) + _mod_row(mod_ref, 3, grp)).astype(_BF)


def _outproj(layer, lru, sc, x, w_out, mod, norm_g):
    row = lambda i: (i, 0)
    return pl.pallas_call(
        _outproj_kernel,
        grid=(ROWS // OUT_TM,),
        in_specs=[
            pl.BlockSpec((OUT_TM, LRU_WIDTH), row),
            pl.BlockSpec((OUT_TM, CONV_WIDTH), row),
            pl.BlockSpec((OUT_TM, D_MODEL), row),
            pl.BlockSpec((None, D_MODEL, D_MODEL), lambda i: (layer, 0, 0), pipeline_mode=pl.Buffered(1)),
            pl.BlockSpec((None, N_MOD, 8, D_MODEL), lambda i: (layer, 0, 0, 0)),
            pl.BlockSpec((None, 2, D_MODEL), lambda i: (layer, 0, 0)),
        ],
        out_specs=[pl.BlockSpec((OUT_TM, D_MODEL), row), pl.BlockSpec((OUT_TM, D_MODEL), row)],
        out_shape=[
            jax.ShapeDtypeStruct((ROWS, D_MODEL), _F32),
            jax.ShapeDtypeStruct((ROWS, D_MODEL), _BF),
        ],
        scratch_shapes=[pltpu.VMEM((D_MODEL, D_MODEL), _BF)],
        compiler_params=_cparams(("arbitrary",)),
        name=f"outproj{layer}",
    )(lru, sc, x, w_out, mod, norm_g)


FFN_TM = 512
FFN_TF = 512
FFN_STEPS = D_FF // FFN_TF


def _ffn_accumulate(hn_ref, wg_ref, wu_ref, wd_ref, acc_ref):
    @pl.when(pl.program_id(1) == 0)
    def _():
        acc_ref[...] = jnp.zeros_like(acc_ref)

    h = hn_ref[...]
    gate = jnp.dot(h, wg_ref[...], preferred_element_type=_F32)
    up = jnp.dot(h, wu_ref[...], preferred_element_type=_F32)
    act = (gate * _sigmoid(gate) * up).astype(_BF)
    acc_ref[...] += jnp.dot(act, wd_ref[...], preferred_element_type=_F32)


def _ffn_mid_kernel(hn_ref, x1_ref, wg_ref, wu_ref, wd_ref, mod_ref, modn_ref, gn_ref, x2_ref, hnn_ref, acc_ref):
    _ffn_accumulate(hn_ref, wg_ref, wu_ref, wd_ref, acc_ref)

    @pl.when(pl.program_id(1) == FFN_STEPS - 1)
    def _():
        grp = _row_group(pl.program_id(0), FFN_TM)
        x2 = x1_ref[...] + _mod_row(mod_ref, 5, grp) * acc_ref[...]
        x2_ref[...] = x2
        y = _rms(x2) * gn_ref[0:1, :]
        hnn_ref[...] = (y * (1.0 + _mod_row(modn_ref, 1, grp)) + _mod_row(modn_ref, 0, grp)).astype(_BF)


def _ffn_last_kernel(hn_ref, x1_ref, wg_ref, wu_ref, wd_ref, mod_ref, fg_ref, yp_ref, ys_ref, acc_ref):
    _ffn_accumulate(hn_ref, wg_ref, wu_ref, wd_ref, acc_ref)
    i = pl.program_id(0)
    last = pl.program_id(1) == FFN_STEPS - 1
    is_lat = i >= CTX_ROWS // FFN_TM

    def result():
        grp = _row_group(i, FFN_TM)
        x2 = x1_ref[...] + _mod_row(mod_ref, 5, grp) * acc_ref[...]
        return _rms(x2) * fg_ref[...]

    @pl.when(last & jnp.logical_not(is_lat))
    def _():
        yp_ref[...] = result()

    @pl.when(last & is_lat)
    def _():
        ys_ref[...] = result()


def _ffn_specs(layer):
    return [
        pl.BlockSpec((FFN_TM, D_MODEL), lambda i, f: (i, 0)),
        pl.BlockSpec((FFN_TM, D_MODEL), lambda i, f: (i, 0)),
        pl.BlockSpec((None, D_MODEL, FFN_TF), lambda i, f: (layer, 0, f)),
        pl.BlockSpec((None, D_MODEL, FFN_TF), lambda i, f: (layer, 0, f)),
        pl.BlockSpec((None, FFN_TF, D_MODEL), lambda i, f: (layer, f, 0)),
        pl.BlockSpec((None, N_MOD, 8, D_MODEL), lambda i, f: (layer, 0, 0, 0)),
    ]


def _ffn_mid(layer, hn2, x1, wg_b, wu_b, wd_b, mod, norm_g):
    row = lambda i, f: (i, 0)
    return pl.pallas_call(
        _ffn_mid_kernel,
        grid=(ROWS // FFN_TM, FFN_STEPS),
        in_specs=_ffn_specs(layer) + [
            pl.BlockSpec((None, N_MOD, 8, D_MODEL), lambda i, f: (layer + 1, 0, 0, 0)),
            pl.BlockSpec((None, 2, D_MODEL), lambda i, f: (layer + 1, 0, 0)),
        ],
        out_specs=[pl.BlockSpec((FFN_TM, D_MODEL), row), pl.BlockSpec((FFN_TM, D_MODEL), row)],
        out_shape=[
            jax.ShapeDtypeStruct((ROWS, D_MODEL), _F32),
            jax.ShapeDtypeStruct((ROWS, D_MODEL), _BF),
        ],
        scratch_shapes=[pltpu.VMEM((FFN_TM, D_MODEL), _F32)],
        compiler_params=_cparams(("arbitrary", "arbitrary")),
        name=f"ffn{layer}",
    )(hn2, x1, wg_b, wu_b, wd_b, mod, mod, norm_g)


def _ffn_last(layer, hn2, x1, wg_b, wu_b, wd_b, mod, final_g):
    nct = CTX_ROWS // FFN_TM
    return pl.pallas_call(
        _ffn_last_kernel,
        grid=(ROWS // FFN_TM, FFN_STEPS),
        in_specs=_ffn_specs(layer) + [pl.BlockSpec((1, D_MODEL), lambda i, f: (0, 0))],
        out_specs=[
            pl.BlockSpec((FFN_TM, D_MODEL), lambda i, f: (jnp.minimum(i, nct - 1), 0)),
            pl.BlockSpec((FFN_TM, D_MODEL), lambda i, f: (jnp.maximum(i - nct, 0), 0)),
        ],
        out_shape=[
            jax.ShapeDtypeStruct((CTX_ROWS, D_MODEL), _F32),
            jax.ShapeDtypeStruct((LAT_ROWS, D_MODEL), _F32),
        ],
        scratch_shapes=[pltpu.VMEM((FFN_TM, D_MODEL), _F32)],
        compiler_params=_cparams(("arbitrary", "arbitrary")),
        name=f"ffn{layer}",
    )(hn2, x1, wg_b, wu_b, wd_b, mod, final_g.reshape(1, D_MODEL))


def _pos_table():
    rows = DEC_SEQ // GRID_W
    r = np.repeat(np.arange(rows, dtype=np.float32), GRID_W)
    col = np.tile(np.arange(GRID_W, dtype=np.float32), rows)
    n_freq = D_MODEL // 4
    freqs = (1.0 / (np.float32(POS_BASE) ** (np.arange(n_freq, dtype=np.float32) / np.float32(n_freq)))).astype(np.float32)
    er = r[:, None] * freqs
    ec = col[:, None] * freqs
    return np.concatenate([np.sin(er), np.cos(er), np.sin(ec), np.cos(ec)], axis=-1).astype(np.float32)


def kernel(x_prompt, x_sample, state_rglru, c, c_ctx, w_mod, b_mod, norm_g, w_in, lru_conv_w, lru_conv_b,
           lru_w_a, lru_b_a, lru_w_i, lru_b_i, lru_lambda, sc_conv_w, w_out, w_gate, w_up, w_down, final_g):
    w_in_heads = jnp.transpose(w_in.reshape(DEPTH, D_MODEL, Z_SLABS, LRU_HEADS, HEAD_DIM), (0, 3, 1, 2, 4))
    w_in_heads = w_in_heads.reshape(DEPTH, LRU_HEADS, D_MODEL, Z_SLABS * HEAD_DIM).astype(_BF)
    wg_b = w_gate.astype(_BF)
    wu_b = w_up.astype(_BF)
    wd_b = w_down.astype(_BF)

    c8 = jnp.concatenate([c_ctx[None, :], c, jnp.zeros((8 - 1 - DEC_BATCH, D_MODEL), _F32)], axis=0)
    mod = _modulation(c8, w_mod, b_mod)

    pos = jnp.asarray(_pos_table())
    hn, x = _prenorm(x_prompt.reshape(CTX_ROWS, D_MODEL), x_sample.reshape(LAT_ROWS, D_MODEL), pos, mod, norm_g)

    cb = lru_conv_b.reshape(DEPTH, 1, LRU_WIDTH)
    b_a = lru_b_a.reshape(DEPTH, 2, LRU_WIDTH)
    b_i = lru_b_i.reshape(DEPTH, 2, LRU_WIDTH)
    h0 = jnp.pad(jnp.transpose(state_rglru, (1, 2, 0, 3)), ((0, 0), (0, 0), (0, 8 - DEC_BATCH), (0, 0)))

    states = []
    for layer in range(DEPTH):
        lru, sc, st = _inmix(layer, hn, w_in_heads, lru_conv_w, cb, lru_w_a, lru_w_i, b_a, b_i, lru_lambda,
                             sc_conv_w, h0[layer])
        states.append(st[:, :BATCH, :])
        x1, hn2 = _outproj(layer, lru, sc, x, w_out, mod, norm_g)
        if layer + 1 < DEPTH:
            x, hn = _ffn_mid(layer, hn2, x1, wg_b, wu_b, wd_b, mod, norm_g)
        else:
            y_p, y_s = _ffn_last(layer, hn2, x1, wg_b, wu_b, wd_b, mod, final_g)

    new_state = jnp.transpose(jnp.stack(states, axis=0), (2, 0, 1, 3))
    return (y_p.reshape(BATCH, SEQ, D_MODEL), y_s.reshape(DEC_BATCH, DEC_SEQ, D_MODEL), new_state)
```

```python
import math

import numpy as np
import jax
import jax.numpy as jnp
from jax import lax
from jax.experimental import pallas as pl
from jax.experimental.pallas import tpu as pltpu

D_MODEL = 2048
BATCH = 32
SEQ = 256
DEPTH = 2
DEC_BATCH = 2
DEC_SEQ = 2048
GRID_W = 64
LRU_WIDTH = 1024
LRU_HEADS = 8
HEAD_DIM = 128
LRU_CONV = 4
LRU_C = 8.0
CONV_WIDTH = 1024
SHORT_CONV = 3
D_IN = 2 * LRU_WIDTH + 3 * CONV_WIDTH
D_FF = 5632
N_MOD = 6
EPS = 1e-6
POS_BASE = 10000.0

CTX_ROWS = BATCH * SEQ
LAT_ROWS = DEC_BATCH * DEC_SEQ
ROWS = CTX_ROWS + LAT_ROWS
N_SLAB = D_IN // HEAD_DIM

GROUP_ROWS = 2048
CHUNK = 256
N_CHUNK = GROUP_ROWS // CHUNK
N_GROUPS = ROWS // GROUP_ROWS
N_CTX_GROUPS = CTX_ROWS // GROUP_ROWS
SUBLANES = 8
PITCH = CHUNK + SUBLANES
XPITCH = CHUNK + 2 * SUBLANES

V7X_VMEM_LIMIT = 56 * 1024 * 1024
F32_TINY = float(np.finfo(np.float32).tiny)
GELU_C = math.sqrt(2.0 / math.pi)
GELU_C3 = 0.044715 * GELU_C

_BF = jnp.bfloat16
_F32 = jnp.float32


def _cparams(sem):
    return pltpu.CompilerParams(dimension_semantics=sem, vmem_limit_bytes=V7X_VMEM_LIMIT)


def _sigmoid(x):
    return 0.5 * jnp.tanh(0.5 * x) + 0.5


def _row_group(tile, rows_per_tile):
    return jnp.maximum((tile * rows_per_tile) // DEC_SEQ - CTX_ROWS // DEC_SEQ + 1, 0)


def _mod_row(mod_ref, j, grp):
    return mod_ref[j, pl.ds(grp, 1), :]


def _rms(x):
    return x * lax.rsqrt(jnp.mean(x * x, axis=-1, keepdims=True) + EPS)


MOD_TN = 1024


def _mod_kernel(c_ref, w_ref, b_ref, o_ref):
    c = c_ref[...]
    s = (c * _sigmoid(c)).astype(_BF)
    o_ref[...] = jnp.dot(s, w_ref[...].astype(_BF), preferred_element_type=_F32) + b_ref[...]


def _modulation(c8, w_mod, b_mod):
    per = D_MODEL // MOD_TN
    b4 = b_mod.reshape(DEPTH, N_MOD, 1, D_MODEL)
    return pl.pallas_call(
        _mod_kernel,
        grid=(DEPTH, N_MOD * per),
        in_specs=[
            pl.BlockSpec((8, D_MODEL), lambda l, n: (0, 0)),
            pl.BlockSpec((None, D_MODEL, MOD_TN), lambda l, n: (l, 0, n)),
            pl.BlockSpec((None, None, 1, MOD_TN), lambda l, n: (l, n // per, 0, n % per)),
        ],
        out_specs=pl.BlockSpec((None, None, 8, MOD_TN), lambda l, n: (l, n // per, 0, n % per)),
        out_shape=jax.ShapeDtypeStruct((DEPTH, N_MOD, 8, D_MODEL), _F32),
        compiler_params=_cparams(("arbitrary", "arbitrary")),
        name="modulation",
    )(c8, w_mod, b4)


PRE_TM = 512


def _prenorm_kernel(xp_ref, xs_ref, pos_ref, mod_ref, g_ref, hn_ref, x0_ref):
    i = pl.program_id(0)
    is_lat = i >= CTX_ROWS // PRE_TM
    grp = _row_group(i, PRE_TM)
    x = jnp.where(is_lat, xs_ref[...] + pos_ref[...], xp_ref[...])
    x0_ref[...] = x
    y = _rms(x) * g_ref[0:1, :]
    hn = y * (1.0 + _mod_row(mod_ref, 1, grp)) + _mod_row(mod_ref, 0, grp)
    hn_ref[...] = hn.astype(_BF)


def _prenorm(xp, xs, pos, mod, norm_g):
    nct = CTX_ROWS // PRE_TM
    npos = DEC_SEQ // PRE_TM
    return pl.pallas_call(
        _prenorm_kernel,
        grid=(ROWS // PRE_TM,),
        in_specs=[
            pl.BlockSpec((PRE_TM, D_MODEL), lambda i: (jnp.minimum(i, nct - 1), 0)),
            pl.BlockSpec((PRE_TM, D_MODEL), lambda i: (jnp.maximum(i - nct, 0), 0)),
            pl.BlockSpec((PRE_TM, D_MODEL), lambda i: (jnp.maximum(i - nct, 0) % npos, 0)),
            pl.BlockSpec((None, N_MOD, 8, D_MODEL), lambda i: (0, 0, 0, 0)),
            pl.BlockSpec((None, 2, D_MODEL), lambda i: (0, 0, 0)),
        ],
        out_specs=[
            pl.BlockSpec((PRE_TM, D_MODEL), lambda i: (i, 0)),
            pl.BlockSpec((PRE_TM, D_MODEL), lambda i: (i, 0)),
        ],
        out_shape=[
            jax.ShapeDtypeStruct((ROWS, D_MODEL), _BF),
            jax.ShapeDtypeStruct((ROWS, D_MODEL), _F32),
        ],
        compiler_params=_cparams(("arbitrary",)),
        name="prenorm",
    )(xp, xs, pos, mod, norm_g)


IN_TM = 1024
IN_TN = 1024
IN_STEPS = (D_IN // IN_TN) * (ROWS // IN_TM)
CAST_STEPS = 32
CAST_ROWS_IN = D_MODEL // CAST_STEPS
CAST_ROWS_FF = D_FF // CAST_STEPS


def _inproj_kernel(hn_ref, w_ref, wg_ref, wu_ref, wd_ref, z_ref, wgb_ref, wub_ref, wdb_ref, wb_scr):
    @pl.when(pl.program_id(1) == 0)
    def _():
        wb_scr[...] = w_ref[...].astype(_BF)

    @pl.when(pl.program_id(0) * (ROWS // IN_TM) + pl.program_id(1) < CAST_STEPS)
    def _():
        wgb_ref[...] = wg_ref[...].astype(_BF)
        wub_ref[...] = wu_ref[...].astype(_BF)
        wdb_ref[...] = wd_ref[...].astype(_BF)

    acc = jnp.dot(hn_ref[...], wb_scr[...], preferred_element_type=_F32)
    for j in range(IN_TN // HEAD_DIM):
        z_ref[j] = acc[:, j * HEAD_DIM:(j + 1) * HEAD_DIM]


def _inproj(layer, hn, w_in, w_gate, w_up, w_down):
    assert CAST_STEPS <= IN_STEPS
    cast_block = lambda n, i: (jnp.minimum(n * (ROWS // IN_TM) + i, CAST_STEPS - 1), 0)
    in_w = pl.BlockSpec((None, CAST_ROWS_IN, D_FF), lambda n, i: (layer,) + cast_block(n, i))
    ff_w = pl.BlockSpec((None, CAST_ROWS_FF, D_MODEL), lambda n, i: (layer,) + cast_block(n, i))
    return pl.pallas_call(
        _inproj_kernel,
        grid=(D_IN // IN_TN, ROWS // IN_TM),
        in_specs=[
            pl.BlockSpec((IN_TM, D_MODEL), lambda n, i: (i, 0)),
            pl.BlockSpec((None, D_MODEL, IN_TN), lambda n, i: (layer, 0, n)),
            in_w, in_w, ff_w,
        ],
        out_specs=[
            pl.BlockSpec((IN_TN // HEAD_DIM, IN_TM, HEAD_DIM), lambda n, i: (n, i, 0)),
            pl.BlockSpec((CAST_ROWS_IN, D_FF), cast_block),
            pl.BlockSpec((CAST_ROWS_IN, D_FF), cast_block),
            pl.BlockSpec((CAST_ROWS_FF, D_MODEL), cast_block),
        ],
        out_shape=[
            jax.ShapeDtypeStruct((N_SLAB, ROWS, HEAD_DIM), _F32),
            jax.ShapeDtypeStruct((D_MODEL, D_FF), _BF),
            jax.ShapeDtypeStruct((D_MODEL, D_FF), _BF),
            jax.ShapeDtypeStruct((D_FF, D_MODEL), _BF),
        ],
        scratch_shapes=[pltpu.VMEM((D_MODEL, IN_TN), _BF)],
        compiler_params=_cparams(("arbitrary", "arbitrary")),
        name=f"inproj{layer}",
    )(hn, w_in, w_gate, w_up, w_down)


def _mixer_kernel(xl_ref, yg_ref, bg_ref, cg_ref, v_ref, cw_ref, cb_ref, wa_ref, wi_ref, ba_ref, bi_ref,
                  lam_ref, scw_ref, h0_ref, lru_ref, sc_ref, st_ref,
                  a_scr, u_scr, h_scr, pad_scr, pe_scr, hc_scr):
    g = pl.program_id(0)
    is_lat = g >= N_CTX_GROUPS

    def fill_padded(rows):
        zeros = jnp.zeros((SUBLANES, HEAD_DIM), _F32)
        for s in range(N_CHUNK):
            base = s * XPITCH
            lo = s * CHUNK
            front = jnp.where(is_lat, rows(lo - SUBLANES, SUBLANES), 0.0) if s > 0 else zeros
            back = jnp.where(is_lat, rows(lo + CHUNK, SUBLANES), 0.0) if s + 1 < N_CHUNK else zeros
            pad_scr[base:base + SUBLANES, :] = front
            pad_scr[base + SUBLANES:base + SUBLANES + CHUNK, :] = rows(lo, CHUNK)
            pad_scr[base + SUBLANES + CHUNK:base + XPITCH, :] = back

    def tap(s, k):
        start = s * XPITCH + SUBLANES + k
        return pad_scr[start:start + CHUNK, :]

    fill_padded(lambda lo, n: xl_ref[0, lo:lo + n, :])
    wcat = (0.5 * jnp.concatenate([wa_ref[0, 0], wi_ref[0, 0], wa_ref[1, 0], wi_ref[1, 0]], axis=1)).astype(_BF)
    half_ba = 0.5 * ba_ref[...]
    half_bi = 0.5 * bi_ref[...]
    nlam = -lam_ref[...]
    softplus = jnp.maximum(nlam, 0.0) + jnp.log1p(jnp.exp(-jnp.abs(nlam)))
    half_rate = (-0.5 * LRU_C) * softplus

    for s in range(N_CHUNK):
        xc = (cw_ref[2:3, :] * xl_ref[0, s * CHUNK:(s + 1) * CHUNK, :] + cb_ref[...]
              + cw_ref[0:1, :] * tap(s, -2) + cw_ref[1:2, :] * tap(s, -1) + cw_ref[3:4, :] * tap(s, 1))
        gates = jnp.dot(xc.astype(_BF), wcat, preferred_element_type=_F32)
        for d in range(2):
            half_ra = gates[:, d * 256:d * 256 + HEAD_DIM] + half_ba[d:d + 1, :]
            half_ia = gates[:, d * 256 + HEAD_DIM:(d + 1) * 256] + half_bi[d:d + 1, :]
            log_a = half_rate[d:d + 1, :] * (1.0 + jnp.tanh(half_ra))
            a = jnp.exp(log_a)
            th = jnp.tanh(log_a)
            q = (-0.5 * th) / (1.0 - th)
            root = jnp.maximum(q, 0.0) * lax.rsqrt(jnp.maximum(q, F32_TINY))
            a_scr[d, s * PITCH:s * PITCH + CHUNK, :] = a
            u_scr[d, s * PITCH:s * PITCH + CHUNK, :] = root * ((1.0 + jnp.tanh(half_ia)) * xc)

    def strided(t):
        return pl.ds(t, N_CHUNK, stride=PITCH)

    @pl.when(jnp.logical_not(is_lat))
    def _():
        hc_scr[...] = jnp.zeros_like(hc_scr)

    @pl.when(is_lat)
    def _():
        def body(t, c):
            pf, ef, pb, eb = c
            af = a_scr[0, strided(t), :]
            ef = af * ef + u_scr[0, strided(t), :]
            pf = pf * af
            tb = CHUNK - 1 - t
            ab = a_scr[1, strided(tb), :]
            eb = ab * eb + u_scr[1, strided(tb), :]
            pb = pb * ab
            return pf, ef, pb, eb

        one = jnp.ones((N_CHUNK, HEAD_DIM), _F32)
        zero = jnp.zeros((N_CHUNK, HEAD_DIM), _F32)
        pf, ef, pb, eb = lax.fori_loop(0, CHUNK, body, (one, zero, one, zero), unroll=8)
        pe_scr[0] = pf
        pe_scr[1] = ef
        pe_scr[2] = pb
        pe_scr[3] = eb
        seq = g - N_CTX_GROUPS
        h = h0_ref[0, pl.ds(seq, 1), :]
        for c in range(N_CHUNK):
            hc_scr[0, c:c + 1, :] = h
            h = pe_scr[0, c:c + 1, :] * h + pe_scr[1, c:c + 1, :]
        h = h0_ref[1, pl.ds(seq, 1), :]
        for c in reversed(range(N_CHUNK)):
            hc_scr[1, c:c + 1, :] = h
            h = pe_scr[2, c:c + 1, :] * h + pe_scr[3, c:c + 1, :]

    def scan_body(t, c):
        hf, hb = c
        hf = a_scr[0, strided(t), :] * hf + u_scr[0, strided(t), :]
        h_scr[0, strided(t), :] = hf
        tb = CHUNK - 1 - t
        hb = a_scr[1, strided(tb), :] * hb + u_scr[1, strided(tb), :]
        h_scr[1, strided(tb), :] = hb
        return hf, hb

    hf, hb = lax.fori_loop(0, CHUNK, scan_body, (hc_scr[0], hc_scr[1]), unroll=8)
    st_ref[0] = hf
    st_ref[1] = hb

    for s in range(N_CHUNK):
        y = yg_ref[0, s * CHUNK:(s + 1) * CHUNK, :]
        gelu = (0.5 * y) * (1.0 + jnp.tanh(y * (GELU_C + GELU_C3 * (y * y))))
        o = h_scr[0, s * PITCH:s * PITCH + CHUNK, :] + h_scr[1, s * PITCH:s * PITCH + CHUNK, :]
        lru_ref[s * CHUNK:(s + 1) * CHUNK, :] = (o * gelu).astype(_BF)

    fill_padded(lambda lo, n: cg_ref[0, lo:lo + n, :] * v_ref[0, lo:lo + n, :])
    for s in range(N_CHUNK):
        conv = scw_ref[1:2, :] * tap(s, 0) + scw_ref[0:1, :] * tap(s, -1) + scw_ref[2:3, :] * tap(s, 1)
        sc_ref[s * CHUNK:(s + 1) * CHUNK, :] = (bg_ref[0, s * CHUNK:(s + 1) * CHUNK, :] * conv).astype(_BF)


def _mixer(layer, z, cw, cb, w_a, w_i, b_a, b_i, lam, scw, h0):
    def zspec(off):
        return pl.BlockSpec((1, GROUP_ROWS, HEAD_DIM), lambda g, h: (off + h, g, 0))

    def vec(rows):
        return pl.BlockSpec((None, rows, HEAD_DIM), lambda g, h: (layer, 0, h))

    wspec = pl.BlockSpec((None, 2, 1, HEAD_DIM, HEAD_DIM), lambda g, h: (layer, 0, h, 0, 0))
    scan_buf = pltpu.VMEM((2, N_CHUNK * PITCH, HEAD_DIM), _F32)
    return pl.pallas_call(
        _mixer_kernel,
        grid=(N_GROUPS, LRU_HEADS),
        in_specs=[
            zspec(0), zspec(8), zspec(16), zspec(24), zspec(32),
            vec(LRU_CONV), vec(1), wspec, wspec, vec(2), vec(2), vec(2), vec(SHORT_CONV),
            pl.BlockSpec((2, 8, HEAD_DIM), lambda g, h: (0, 0, h)),
        ],
        out_specs=[
            pl.BlockSpec((GROUP_ROWS, HEAD_DIM), lambda g, h: (g, h)),
            pl.BlockSpec((GROUP_ROWS, HEAD_DIM), lambda g, h: (g, h)),
            pl.BlockSpec((2, N_CHUNK, HEAD_DIM), lambda g, h: (0, g, h)),
        ],
        out_shape=[
            jax.ShapeDtypeStruct((ROWS, LRU_WIDTH), _BF),
            jax.ShapeDtypeStruct((ROWS, CONV_WIDTH), _BF),
            jax.ShapeDtypeStruct((2, N_GROUPS * N_CHUNK, LRU_WIDTH), _F32),
        ],
        scratch_shapes=[
            scan_buf, scan_buf, scan_buf,
            pltpu.VMEM((N_CHUNK * XPITCH, HEAD_DIM), _F32),
            pltpu.VMEM((4, N_CHUNK, HEAD_DIM), _F32),
            pltpu.VMEM((2, N_CHUNK, HEAD_DIM), _F32),
        ],
        compiler_params=_cparams(("arbitrary", "arbitrary")),
        name=f"mixer{layer}",
    )(z, z, z, z, z, cw, cb, w_a, w_i, b_a, b_i, lam, scw, h0)


OUT_TM = 512


def _outproj_kernel(lru_ref, sc_ref, x_ref, w_ref, mod_ref, g_ref, x1_ref, hn2_ref, wb_scr):
    @pl.when(pl.program_id(0) == 0)
    def _():
        wb_scr[...] = w_ref[...].astype(_BF)

    grp = _row_group(pl.program_id(0), OUT_TM)
    m = jnp.dot(lru_ref[...], wb_scr[0:LRU_WIDTH, :], preferred_element_type=_F32)
    m = m + jnp.dot(sc_ref[...], wb_scr[LRU_WIDTH:, :], preferred_element_type=_F32)
    x1 = x_ref[...] + _mod_row(mod_ref, 2, grp) * m
    x1_ref[...] = x1
    y = _rms(x1) * g_ref[1:2, :]
    hn2_ref[...] = (y * (1.0 + _mod_row(mod_ref, 4, grp)) + _mod_row(mod_ref, 3, grp)).astype(_BF)


def _outproj(layer, lru, sc, x, w_out, mod, norm_g):
    row = lambda i: (i, 0)
    return pl.pallas_call(
        _outproj_kernel,
        grid=(ROWS // OUT_TM,),
        in_specs=[
            pl.BlockSpec((OUT_TM, LRU_WIDTH), row),
            pl.BlockSpec((OUT_TM, CONV_WIDTH), row),
            pl.BlockSpec((OUT_TM, D_MODEL), row),
            pl.BlockSpec((None, D_MODEL, D_MODEL), lambda i: (layer, 0, 0), pipeline_mode=pl.Buffered(1)),
            pl.BlockSpec((None, N_MOD, 8, D_MODEL), lambda i: (layer, 0, 0, 0)),
            pl.BlockSpec((None, 2, D_MODEL), lambda i: (layer, 0, 0)),
        ],
        out_specs=[pl.BlockSpec((OUT_TM, D_MODEL), row), pl.BlockSpec((OUT_TM, D_MODEL), row)],
        out_shape=[
            jax.ShapeDtypeStruct((ROWS, D_MODEL), _F32),
            jax.ShapeDtypeStruct((ROWS, D_MODEL), _BF),
        ],
        scratch_shapes=[pltpu.VMEM((D_MODEL, D_MODEL), _BF)],
        compiler_params=_cparams(("arbitrary",)),
        name=f"outproj{layer}",
    )(lru, sc, x, w_out, mod, norm_g)


FFN_TM = 512
FFN_TF = 768
FFN_STEPS = pl.cdiv(D_FF, FFN_TF)
FFN_LAST_TF = D_FF - (FFN_STEPS - 1) * FFN_TF


def _ffn_accumulate(hn_ref, wg_ref, wu_ref, wd_ref, acc_ref):
    f = pl.program_id(1)

    @pl.when(f == 0)
    def _():
        acc_ref[...] = jnp.zeros_like(acc_ref)

    def chunk(cols):
        h = hn_ref[...]
        gate = jnp.dot(h, wg_ref[:, 0:cols], preferred_element_type=_F32)
        up = jnp.dot(h, wu_ref[:, 0:cols], preferred_element_type=_F32)
        act = (gate * _sigmoid(gate) * up).astype(_BF)
        acc_ref[...] += jnp.dot(act, wd_ref[0:cols, :], preferred_element_type=_F32)

    @pl.when(f < FFN_STEPS - 1)
    def _():
        chunk(FFN_TF)

    @pl.when(f == FFN_STEPS - 1)
    def _():
        chunk(FFN_LAST_TF)


def _ffn_mid_kernel(hn_ref, x1_ref, wg_ref, wu_ref, wd_ref, mod_ref, modn_ref, gn_ref, x2_ref, hnn_ref, acc_ref):
    _ffn_accumulate(hn_ref, wg_ref, wu_ref, wd_ref, acc_ref)

    @pl.when(pl.program_id(1) == FFN_STEPS - 1)
    def _():
        grp = _row_group(pl.program_id(0), FFN_TM)
        x2 = x1_ref[...] + _mod_row(mod_ref, 5, grp) * acc_ref[...]
        x2_ref[...] = x2
        y = _rms(x2) * gn_ref[0:1, :]
        hnn_ref[...] = (y * (1.0 + _mod_row(modn_ref, 1, grp)) + _mod_row(modn_ref, 0, grp)).astype(_BF)


def _ffn_last_kernel(hn_ref, x1_ref, wg_ref, wu_ref, wd_ref, mod_ref, fg_ref, yp_ref, ys_ref, acc_ref):
    _ffn_accumulate(hn_ref, wg_ref, wu_ref, wd_ref, acc_ref)
    i = pl.program_id(0)
    last = pl.program_id(1) == FFN_STEPS - 1
    is_lat = i >= CTX_ROWS // FFN_TM

    def result():
        grp = _row_group(i, FFN_TM)
        x2 = x1_ref[...] + _mod_row(mod_ref, 5, grp) * acc_ref[...]
        return _rms(x2) * fg_ref[...]

    @pl.when(last & jnp.logical_not(is_lat))
    def _():
        yp_ref[...] = result()

    @pl.when(last & is_lat)
    def _():
        ys_ref[...] = result()


def _ffn_specs(layer):
    return [
        pl.BlockSpec((FFN_TM, D_MODEL), lambda i, f: (i, 0)),
        pl.BlockSpec((FFN_TM, D_MODEL), lambda i, f: (i, 0)),
        pl.BlockSpec((D_MODEL, FFN_TF), lambda i, f: (0, f)),
        pl.BlockSpec((D_MODEL, FFN_TF), lambda i, f: (0, f)),
        pl.BlockSpec((FFN_TF, D_MODEL), lambda i, f: (f, 0)),
        pl.BlockSpec((None, N_MOD, 8, D_MODEL), lambda i, f: (layer, 0, 0, 0)),
    ]


def _ffn_mid(layer, hn2, x1, wg_b, wu_b, wd_b, mod, norm_g):
    row = lambda i, f: (i, 0)
    return pl.pallas_call(
        _ffn_mid_kernel,
        grid=(ROWS // FFN_TM, FFN_STEPS),
        in_specs=_ffn_specs(layer) + [
            pl.BlockSpec((None, N_MOD, 8, D_MODEL), lambda i, f: (layer + 1, 0, 0, 0)),
            pl.BlockSpec((None, 2, D_MODEL), lambda i, f: (layer + 1, 0, 0)),
        ],
        out_specs=[pl.BlockSpec((FFN_TM, D_MODEL), row), pl.BlockSpec((FFN_TM, D_MODEL), row)],
        out_shape=[
            jax.ShapeDtypeStruct((ROWS, D_MODEL), _F32),
            jax.ShapeDtypeStruct((ROWS, D_MODEL), _BF),
        ],
        scratch_shapes=[pltpu.VMEM((FFN_TM, D_MODEL), _F32)],
        compiler_params=_cparams(("arbitrary", "arbitrary")),
        name=f"ffn{layer}",
    )(hn2, x1, wg_b, wu_b, wd_b, mod, mod, norm_g)


def _ffn_last(layer, hn2, x1, wg_b, wu_b, wd_b, mod, final_g):
    nct = CTX_ROWS // FFN_TM
    return pl.pallas_call(
        _ffn_last_kernel,
        grid=(ROWS // FFN_TM, FFN_STEPS),
        in_specs=_ffn_specs(layer) + [pl.BlockSpec((1, D_MODEL), lambda i, f: (0, 0))],
        out_specs=[
            pl.BlockSpec((FFN_TM, D_MODEL), lambda i, f: (jnp.minimum(i, nct - 1), 0)),
            pl.BlockSpec((FFN_TM, D_MODEL), lambda i, f: (jnp.maximum(i - nct, 0), 0)),
        ],
        out_shape=[
            jax.ShapeDtypeStruct((CTX_ROWS, D_MODEL), _F32),
            jax.ShapeDtypeStruct((LAT_ROWS, D_MODEL), _F32),
        ],
        scratch_shapes=[pltpu.VMEM((FFN_TM, D_MODEL), _F32)],
        compiler_params=_cparams(("arbitrary", "arbitrary")),
        name=f"ffn{layer}",
    )(hn2, x1, wg_b, wu_b, wd_b, mod, final_g.reshape(1, D_MODEL))


def _pos_table():
    rows = DEC_SEQ // GRID_W
    r = np.repeat(np.arange(rows, dtype=np.float32), GRID_W)
    col = np.tile(np.arange(GRID_W, dtype=np.float32), rows)
    n_freq = D_MODEL // 4
    freqs = (1.0 / (np.float32(POS_BASE) ** (np.arange(n_freq, dtype=np.float32) / np.float32(n_freq)))).astype(np.float32)
    er = r[:, None] * freqs
    ec = col[:, None] * freqs
    return np.concatenate([np.sin(er), np.cos(er), np.sin(ec), np.cos(ec)], axis=-1).astype(np.float32)


def kernel(x_prompt, x_sample, state_rglru, c, c_ctx, w_mod, b_mod, norm_g, w_in, lru_conv_w, lru_conv_b,
           lru_w_a, lru_b_a, lru_w_i, lru_b_i, lru_lambda, sc_conv_w, w_out, w_gate, w_up, w_down, final_g):
    c8 = jnp.concatenate([c_ctx[None, :], c, jnp.zeros((8 - 1 - DEC_BATCH, D_MODEL), _F32)], axis=0)
    mod = _modulation(c8, w_mod, b_mod)

    pos = jnp.asarray(_pos_table())
    hn, x = _prenorm(x_prompt.reshape(CTX_ROWS, D_MODEL), x_sample.reshape(LAT_ROWS, D_MODEL), pos, mod, norm_g)

    cb = lru_conv_b.reshape(DEPTH, 1, LRU_WIDTH)
    b_a = lru_b_a.reshape(DEPTH, 2, LRU_WIDTH)
    b_i = lru_b_i.reshape(DEPTH, 2, LRU_WIDTH)
    h0 = jnp.pad(jnp.transpose(state_rglru, (1, 2, 0, 3)), ((0, 0), (0, 0), (0, 8 - DEC_BATCH), (0, 0)))

    states = []
    for layer in range(DEPTH):
        z, wg_b, wu_b, wd_b = _inproj(layer, hn, w_in, w_gate, w_up, w_down)
        lru, sc, st = _mixer(layer, z, lru_conv_w, cb, lru_w_a, lru_w_i, b_a, b_i, lru_lambda, sc_conv_w, h0[layer])
        states.append(st[:, :BATCH, :])
        x1, hn2 = _outproj(layer, lru, sc, x, w_out, mod, norm_g)
        if layer + 1 < DEPTH:
            x, hn = _ffn_mid(layer, hn2, x1, wg_b, wu_b, wd_b, mod, norm_g)
        else:
            y_p, y_s = _ffn_last(layer, hn2, x1, wg_b, wu_b, wd_b, mod, final_g)

    new_state = jnp.transpose(jnp.stack(states, axis=0), (2, 0, 1, 3))
    return (y_p.reshape(BATCH, SEQ, D_MODEL), y_s.reshape(DEC_BATCH, DEC_SEQ, D_MODEL), new_state)
```

```python
import math

import numpy as np
import jax
import jax.numpy as jnp
from jax import lax
from jax.experimental import pallas as pl
from jax.experimental.pallas import tpu as pltpu

D_MODEL = 2048
BATCH = 32
SEQ = 256
DEPTH = 2
DEC_BATCH = 2
DEC_SEQ = 2048
GRID_W = 64
LRU_WIDTH = 1024
LRU_HEADS = 8
HEAD_DIM = 128
LRU_CONV = 4
LRU_C = 8.0
CONV_WIDTH = 1024
SHORT_CONV = 3
D_IN = 2 * LRU_WIDTH + 3 * CONV_WIDTH
D_FF = 5632
N_MOD = 6
EPS = 1e-6
POS_BASE = 10000.0

CTX_ROWS = BATCH * SEQ
LAT_ROWS = DEC_BATCH * DEC_SEQ
ROWS = CTX_ROWS + LAT_ROWS
N_SLAB = D_IN // HEAD_DIM

GROUP_ROWS = 2048
CHUNK = 256
N_CHUNK = GROUP_ROWS // CHUNK
N_GROUPS = ROWS // GROUP_ROWS
N_CTX_GROUPS = CTX_ROWS // GROUP_ROWS
SUBLANES = 8
PITCH = CHUNK + SUBLANES
XPITCH = CHUNK + 2 * SUBLANES

V7X_VMEM_LIMIT = 56 * 1024 * 1024
F32_TINY = float(np.finfo(np.float32).tiny)
GELU_C = math.sqrt(2.0 / math.pi)
GELU_C3 = 0.044715 * GELU_C

_BF = jnp.bfloat16
_F32 = jnp.float32


def _cparams(sem):
    return pltpu.CompilerParams(dimension_semantics=sem, vmem_limit_bytes=V7X_VMEM_LIMIT)


def _sigmoid(x):
    return 0.5 * jnp.tanh(0.5 * x) + 0.5


def _row_group(tile, rows_per_tile):
    return jnp.maximum((tile * rows_per_tile) // DEC_SEQ - CTX_ROWS // DEC_SEQ + 1, 0)


def _mod_row(mod_ref, j, grp):
    return mod_ref[j, pl.ds(grp, 1), :]


def _rms(x):
    return x * lax.rsqrt(jnp.mean(x * x, axis=-1, keepdims=True) + EPS)


MOD_TN = 1024


def _mod_kernel(c_ref, w_ref, b_ref, o_ref):
    c = c_ref[...]
    s = (c * _sigmoid(c)).astype(_BF)
    o_ref[...] = jnp.dot(s, w_ref[...].astype(_BF), preferred_element_type=_F32) + b_ref[...]


def _modulation(c8, w_mod, b_mod):
    per = D_MODEL // MOD_TN
    b4 = b_mod.reshape(DEPTH, N_MOD, 1, D_MODEL)
    return pl.pallas_call(
        _mod_kernel,
        grid=(DEPTH, N_MOD * per),
        in_specs=[
            pl.BlockSpec((8, D_MODEL), lambda l, n: (0, 0)),
            pl.BlockSpec((None, D_MODEL, MOD_TN), lambda l, n: (l, 0, n)),
            pl.BlockSpec((None, None, 1, MOD_TN), lambda l, n: (l, n // per, 0, n % per)),
        ],
        out_specs=pl.BlockSpec((None, None, 8, MOD_TN), lambda l, n: (l, n // per, 0, n % per)),
        out_shape=jax.ShapeDtypeStruct((DEPTH, N_MOD, 8, D_MODEL), _F32),
        compiler_params=_cparams(("arbitrary", "arbitrary")),
        name="modulation",
    )(c8, w_mod, b4)


PRE_TM = 512


def _prenorm_kernel(xp_ref, xs_ref, pos_ref, mod_ref, g_ref, hn_ref, x0_ref):
    i = pl.program_id(0)
    is_lat = i >= CTX_ROWS // PRE_TM
    grp = _row_group(i, PRE_TM)
    x = jnp.where(is_lat, xs_ref[...] + pos_ref[...], xp_ref[...])
    x0_ref[...] = x
    y = _rms(x) * g_ref[0:1, :]
    hn = y * (1.0 + _mod_row(mod_ref, 1, grp)) + _mod_row(mod_ref, 0, grp)
    hn_ref[...] = hn.astype(_BF)


def _prenorm(xp, xs, pos, mod, norm_g):
    nct = CTX_ROWS // PRE_TM
    npos = DEC_SEQ // PRE_TM
    return pl.pallas_call(
        _prenorm_kernel,
        grid=(ROWS // PRE_TM,),
        in_specs=[
            pl.BlockSpec((PRE_TM, D_MODEL), lambda i: (jnp.minimum(i, nct - 1), 0)),
            pl.BlockSpec((PRE_TM, D_MODEL), lambda i: (jnp.maximum(i - nct, 0), 0)),
            pl.BlockSpec((PRE_TM, D_MODEL), lambda i: (jnp.maximum(i - nct, 0) % npos, 0)),
            pl.BlockSpec((None, N_MOD, 8, D_MODEL), lambda i: (0, 0, 0, 0)),
            pl.BlockSpec((None, 2, D_MODEL), lambda i: (0, 0, 0)),
        ],
        out_specs=[
            pl.BlockSpec((PRE_TM, D_MODEL), lambda i: (i, 0)),
            pl.BlockSpec((PRE_TM, D_MODEL), lambda i: (i, 0)),
        ],
        out_shape=[
            jax.ShapeDtypeStruct((ROWS, D_MODEL), _BF),
            jax.ShapeDtypeStruct((ROWS, D_MODEL), _F32),
        ],
        compiler_params=_cparams(("arbitrary",)),
        name="prenorm",
    )(xp, xs, pos, mod, norm_g)


IN_TM = 1024
IN_TN = 1024
IN_STEPS = (D_IN // IN_TN) * (ROWS // IN_TM)
CAST_STEPS = 32
CAST_ROWS_IN = D_MODEL // CAST_STEPS
CAST_ROWS_FF = D_FF // CAST_STEPS


def _inproj_kernel(hn_ref, w_ref, wg_ref, wu_ref, wd_ref, z_ref, wgb_ref, wub_ref, wdb_ref, wb_scr):
    @pl.when(pl.program_id(1) == 0)
    def _():
        wb_scr[...] = w_ref[...].astype(_BF)

    step = pl.program_id(0) * (ROWS // IN_TM) + pl.program_id(1)

    @pl.when(step < CAST_STEPS)
    def _():
        wgb_ref[...] = wg_ref[...].astype(_BF)
        wub_ref[...] = wu_ref[...].astype(_BF)

    @pl.when(step >= IN_STEPS - CAST_STEPS)
    def _():
        wdb_ref[...] = wd_ref[...].astype(_BF)

    acc = jnp.dot(hn_ref[...], wb_scr[...], preferred_element_type=_F32)
    for j in range(IN_TN // HEAD_DIM):
        z_ref[j] = acc[:, j * HEAD_DIM:(j + 1) * HEAD_DIM]


def _inproj(layer, hn, w_in, w_gate, w_up, w_down):
    assert CAST_STEPS <= IN_STEPS
    step = lambda n, i: n * (ROWS // IN_TM) + i
    cast_block = lambda n, i: (jnp.minimum(step(n, i), CAST_STEPS - 1), 0)
    late_block = lambda n, i: (jnp.maximum(step(n, i) - (IN_STEPS - CAST_STEPS), 0), 0)
    in_w = pl.BlockSpec((None, CAST_ROWS_IN, D_FF), lambda n, i: (layer,) + cast_block(n, i))
    ff_w = pl.BlockSpec((None, CAST_ROWS_FF, D_MODEL), lambda n, i: (layer,) + late_block(n, i))
    return pl.pallas_call(
        _inproj_kernel,
        grid=(D_IN // IN_TN, ROWS // IN_TM),
        in_specs=[
            pl.BlockSpec((IN_TM, D_MODEL), lambda n, i: (i, 0)),
            pl.BlockSpec((None, D_MODEL, IN_TN), lambda n, i: (layer, 0, n)),
            in_w, in_w, ff_w,
        ],
        out_specs=[
            pl.BlockSpec((IN_TN // HEAD_DIM, IN_TM, HEAD_DIM), lambda n, i: (n, i, 0)),
            pl.BlockSpec((CAST_ROWS_IN, D_FF), cast_block),
            pl.BlockSpec((CAST_ROWS_IN, D_FF), cast_block),
            pl.BlockSpec((CAST_ROWS_FF, D_MODEL), late_block),
        ],
        out_shape=[
            jax.ShapeDtypeStruct((N_SLAB, ROWS, HEAD_DIM), _F32),
            jax.ShapeDtypeStruct((D_MODEL, D_FF), _BF),
            jax.ShapeDtypeStruct((D_MODEL, D_FF), _BF),
            jax.ShapeDtypeStruct((D_FF, D_MODEL), _BF),
        ],
        scratch_shapes=[pltpu.VMEM((D_MODEL, IN_TN), _BF)],
        compiler_params=_cparams(("arbitrary", "arbitrary")),
        name=f"inproj{layer}",
    )(hn, w_in, w_gate, w_up, w_down)


def _mixer_kernel(xl_ref, yg_ref, bg_ref, cg_ref, v_ref, cw_ref, cb_ref, wa_ref, wi_ref, ba_ref, bi_ref,
                  lam_ref, scw_ref, h0_ref, lru_ref, sc_ref, st_ref,
                  a_scr, u_scr, h_scr, pad_scr, pe_scr, hc_scr):
    g = pl.program_id(0)
    is_lat = g >= N_CTX_GROUPS

    def fill_padded(rows):
        zeros = jnp.zeros((SUBLANES, HEAD_DIM), _F32)
        for s in range(N_CHUNK):
            base = s * XPITCH
            lo = s * CHUNK
            front = jnp.where(is_lat, rows(lo - SUBLANES, SUBLANES), 0.0) if s > 0 else zeros
            back = jnp.where(is_lat, rows(lo + CHUNK, SUBLANES), 0.0) if s + 1 < N_CHUNK else zeros
            pad_scr[base:base + SUBLANES, :] = front
            pad_scr[base + SUBLANES:base + SUBLANES + CHUNK, :] = rows(lo, CHUNK)
            pad_scr[base + SUBLANES + CHUNK:base + XPITCH, :] = back

    def tap(s, k):
        start = s * XPITCH + SUBLANES + k
        return pad_scr[start:start + CHUNK, :]

    fill_padded(lambda lo, n: xl_ref[0, lo:lo + n, :])
    wcat = (0.5 * jnp.concatenate([wa_ref[0, 0], wi_ref[0, 0], wa_ref[1, 0], wi_ref[1, 0]], axis=1)).astype(_BF)
    half_ba = 0.5 * ba_ref[...]
    half_bi = 0.5 * bi_ref[...]
    nlam = -lam_ref[...]
    softplus = jnp.maximum(nlam, 0.0) + jnp.log1p(jnp.exp(-jnp.abs(nlam)))
    half_rate = (-0.5 * LRU_C) * softplus

    for s in range(N_CHUNK):
        xc = (cw_ref[2:3, :] * xl_ref[0, s * CHUNK:(s + 1) * CHUNK, :] + cb_ref[...]
              + cw_ref[0:1, :] * tap(s, -2) + cw_ref[1:2, :] * tap(s, -1) + cw_ref[3:4, :] * tap(s, 1))
        gates = jnp.dot(xc.astype(_BF), wcat, preferred_element_type=_F32)
        for d in range(2):
            half_ra = gates[:, d * 256:d * 256 + HEAD_DIM] + half_ba[d:d + 1, :]
            half_ia = gates[:, d * 256 + HEAD_DIM:(d + 1) * 256] + half_bi[d:d + 1, :]
            log_a = half_rate[d:d + 1, :] * (1.0 + jnp.tanh(half_ra))
            a = jnp.exp(log_a)
            th = jnp.tanh(log_a)
            q = (-0.5 * th) / (1.0 - th)
            root = jnp.maximum(q, 0.0) * lax.rsqrt(jnp.maximum(q, F32_TINY))
            a_scr[d, s * PITCH:s * PITCH + CHUNK, :] = a
            u_scr[d, s * PITCH:s * PITCH + CHUNK, :] = root * ((1.0 + jnp.tanh(half_ia)) * xc)

    def strided(t):
        return pl.ds(t, N_CHUNK, stride=PITCH)

    @pl.when(jnp.logical_not(is_lat))
    def _():
        hc_scr[...] = jnp.zeros_like(hc_scr)

    @pl.when(is_lat)
    def _():
        def body(t, c):
            pf, ef, pb, eb = c
            af = a_scr[0, strided(t), :]
            ef = af * ef + u_scr[0, strided(t), :]
            pf = pf * af
            tb = CHUNK - 1 - t
            ab = a_scr[1, strided(tb), :]
            eb = ab * eb + u_scr[1, strided(tb), :]
            pb = pb * ab
            return pf, ef, pb, eb

        one = jnp.ones((N_CHUNK, HEAD_DIM), _F32)
        zero = jnp.zeros((N_CHUNK, HEAD_DIM), _F32)
        pf, ef, pb, eb = lax.fori_loop(0, CHUNK, body, (one, zero, one, zero), unroll=8)
        pe_scr[0] = pf
        pe_scr[1] = ef
        pe_scr[2] = pb
        pe_scr[3] = eb
        seq = g - N_CTX_GROUPS
        h = h0_ref[0, pl.ds(seq, 1), :]
        for c in range(N_CHUNK):
            hc_scr[0, c:c + 1, :] = h
            h = pe_scr[0, c:c + 1, :] * h + pe_scr[1, c:c + 1, :]
        h = h0_ref[1, pl.ds(seq, 1), :]
        for c in reversed(range(N_CHUNK)):
            hc_scr[1, c:c + 1, :] = h
            h = pe_scr[2, c:c + 1, :] * h + pe_scr[3, c:c + 1, :]

    def scan_body(t, c):
        hf, hb = c
        hf = a_scr[0, strided(t), :] * hf + u_scr[0, strided(t), :]
        h_scr[0, strided(t), :] = hf
        tb = CHUNK - 1 - t
        hb = a_scr[1, strided(tb), :] * hb + u_scr[1, strided(tb), :]
        h_scr[1, strided(tb), :] = hb
        return hf, hb

    hf, hb = lax.fori_loop(0, CHUNK, scan_body, (hc_scr[0], hc_scr[1]), unroll=8)
    st_ref[0] = hf
    st_ref[1] = hb

    for s in range(N_CHUNK):
        y = yg_ref[0, s * CHUNK:(s + 1) * CHUNK, :]
        gelu = (0.5 * y) * (1.0 + jnp.tanh(y * (GELU_C + GELU_C3 * (y * y))))
        o = h_scr[0, s * PITCH:s * PITCH + CHUNK, :] + h_scr[1, s * PITCH:s * PITCH + CHUNK, :]
        lru_ref[s * CHUNK:(s + 1) * CHUNK, :] = (o * gelu).astype(_BF)

    fill_padded(lambda lo, n: cg_ref[0, lo:lo + n, :] * v_ref[0, lo:lo + n, :])
    for s in range(N_CHUNK):
        conv = scw_ref[1:2, :] * tap(s, 0) + scw_ref[0:1, :] * tap(s, -1) + scw_ref[2:3, :] * tap(s, 1)
        sc_ref[s * CHUNK:(s + 1) * CHUNK, :] = (bg_ref[0, s * CHUNK:(s + 1) * CHUNK, :] * conv).astype(_BF)


def _mixer(layer, z, cw, cb, w_a, w_i, b_a, b_i, lam, scw, h0):
    def zspec(off):
        return pl.BlockSpec((1, GROUP_ROWS, HEAD_DIM), lambda g, h: (off + h, g, 0))

    def vec(rows):
        return pl.BlockSpec((None, rows, HEAD_DIM), lambda g, h: (layer, 0, h))

    wspec = pl.BlockSpec((None, 2, 1, HEAD_DIM, HEAD_DIM), lambda g, h: (layer, 0, h, 0, 0))
    scan_buf = pltpu.VMEM((2, N_CHUNK * PITCH, HEAD_DIM), _F32)
    return pl.pallas_call(
        _mixer_kernel,
        grid=(N_GROUPS, LRU_HEADS),
        in_specs=[
            zspec(0), zspec(8), zspec(16), zspec(24), zspec(32),
            vec(LRU_CONV), vec(1), wspec, wspec, vec(2), vec(2), vec(2), vec(SHORT_CONV),
            pl.BlockSpec((2, 8, HEAD_DIM), lambda g, h: (0, 0, h)),
        ],
        out_specs=[
            pl.BlockSpec((GROUP_ROWS, HEAD_DIM), lambda g, h: (g, h)),
            pl.BlockSpec((GROUP_ROWS, HEAD_DIM), lambda g, h: (g, h)),
            pl.BlockSpec((2, N_CHUNK, HEAD_DIM), lambda g, h: (0, g, h)),
        ],
        out_shape=[
            jax.ShapeDtypeStruct((ROWS, LRU_WIDTH), _BF),
            jax.ShapeDtypeStruct((ROWS, CONV_WIDTH), _BF),
            jax.ShapeDtypeStruct((2, N_GROUPS * N_CHUNK, LRU_WIDTH), _F32),
        ],
        scratch_shapes=[
            scan_buf, scan_buf, scan_buf,
            pltpu.VMEM((N_CHUNK * XPITCH, HEAD_DIM), _F32),
            pltpu.VMEM((4, N_CHUNK, HEAD_DIM), _F32),
            pltpu.VMEM((2, N_CHUNK, HEAD_DIM), _F32),
        ],
        compiler_params=_cparams(("arbitrary", "arbitrary")),
        name=f"mixer{layer}",
    )(z, z, z, z, z, cw, cb, w_a, w_i, b_a, b_i, lam, scw, h0)


OUT_TM = 512


def _outproj_kernel(lru_ref, sc_ref, x_ref, w_ref, mod_ref, g_ref, x1_ref, hn2_ref, wb_scr):
    @pl.when(pl.program_id(0) == 0)
    def _():
        wb_scr[...] = w_ref[...].astype(_BF)

    grp = _row_group(pl.program_id(0), OUT_TM)
    gate = _mod_row(mod_ref, 2, grp)
    scale = g_ref[1:2, :] * (1.0 + _mod_row(mod_ref, 4, grp))
    shift = _mod_row(mod_ref, 3, grp)
    for r0 in range(0, OUT_TM, OUT_TM // 2):
        rows = slice(r0, r0 + OUT_TM // 2)
        m = jnp.dot(lru_ref[rows, :], wb_scr[0:LRU_WIDTH, :], preferred_element_type=_F32)
        m = m + jnp.dot(sc_ref[rows, :], wb_scr[LRU_WIDTH:, :], preferred_element_type=_F32)
        x1 = x_ref[rows, :] + gate * m
        x1_ref[rows, :] = x1
        hn2_ref[rows, :] = (_rms(x1) * scale + shift).astype(_BF)


def _outproj(layer, lru, sc, x, w_out, mod, norm_g):
    row = lambda i: (i, 0)
    return pl.pallas_call(
        _outproj_kernel,
        grid=(ROWS // OUT_TM,),
        in_specs=[
            pl.BlockSpec((OUT_TM, LRU_WIDTH), row),
            pl.BlockSpec((OUT_TM, CONV_WIDTH), row),
            pl.BlockSpec((OUT_TM, D_MODEL), row),
            pl.BlockSpec((None, D_MODEL, D_MODEL), lambda i: (layer, 0, 0), pipeline_mode=pl.Buffered(1)),
            pl.BlockSpec((None, N_MOD, 8, D_MODEL), lambda i: (layer, 0, 0, 0)),
            pl.BlockSpec((None, 2, D_MODEL), lambda i: (layer, 0, 0)),
        ],
        out_specs=[pl.BlockSpec((OUT_TM, D_MODEL), row), pl.BlockSpec((OUT_TM, D_MODEL), row)],
        out_shape=[
            jax.ShapeDtypeStruct((ROWS, D_MODEL), _F32),
            jax.ShapeDtypeStruct((ROWS, D_MODEL), _BF),
        ],
        scratch_shapes=[pltpu.VMEM((D_MODEL, D_MODEL), _BF)],
        compiler_params=_cparams(("arbitrary",)),
        name=f"outproj{layer}",
    )(lru, sc, x, w_out, mod, norm_g)


FFN_TM = 512
FFN_TF = 512
FFN_STEPS = D_FF // FFN_TF


def _ffn_accumulate(hn_ref, wg_ref, wu_ref, wd_ref, acc_ref):
    @pl.when(pl.program_id(1) == 0)
    def _():
        acc_ref[...] = jnp.zeros_like(acc_ref)

    h = hn_ref[...]
    gate = jnp.dot(h, wg_ref[...], preferred_element_type=_F32)
    up = jnp.dot(h, wu_ref[...], preferred_element_type=_F32)
    act = (gate * _sigmoid(gate) * up).astype(_BF)
    acc_ref[...] += jnp.dot(act, wd_ref[...], preferred_element_type=_F32)


def _ffn_mid_kernel(hn_ref, x1_ref, wg_ref, wu_ref, wd_ref, mod_ref, modn_ref, gn_ref, x2_ref, hnn_ref, acc_ref):
    _ffn_accumulate(hn_ref, wg_ref, wu_ref, wd_ref, acc_ref)

    @pl.when(pl.program_id(1) == FFN_STEPS - 1)
    def _():
        grp = _row_group(pl.program_id(0), FFN_TM)
        x2 = x1_ref[...] + _mod_row(mod_ref, 5, grp) * acc_ref[...]
        x2_ref[...] = x2
        y = _rms(x2) * gn_ref[0:1, :]
        hnn_ref[...] = (y * (1.0 + _mod_row(modn_ref, 1, grp)) + _mod_row(modn_ref, 0, grp)).astype(_BF)


def _ffn_last_kernel(hn_ref, x1_ref, wg_ref, wu_ref, wd_ref, mod_ref, fg_ref, yp_ref, ys_ref, acc_ref):
    _ffn_accumulate(hn_ref, wg_ref, wu_ref, wd_ref, acc_ref)
    i = pl.program_id(0)
    last = pl.program_id(1) == FFN_STEPS - 1
    is_lat = i >= CTX_ROWS // FFN_TM

    def result():
        grp = _row_group(i, FFN_TM)
        x2 = x1_ref[...] + _mod_row(mod_ref, 5, grp) * acc_ref[...]
        return _rms(x2) * fg_ref[...]

    @pl.when(last & jnp.logical_not(is_lat))
    def _():
        yp_ref[...] = result()

    @pl.when(last & is_lat)
    def _():
        ys_ref[...] = result()


def _ffn_specs(layer):
    return [
        pl.BlockSpec((FFN_TM, D_MODEL), lambda i, f: (i, 0)),
        pl.BlockSpec((FFN_TM, D_MODEL), lambda i, f: (i, 0)),
        pl.BlockSpec((D_MODEL, FFN_TF), lambda i, f: (0, f)),
        pl.BlockSpec((D_MODEL, FFN_TF), lambda i, f: (0, f)),
        pl.BlockSpec((FFN_TF, D_MODEL), lambda i, f: (f, 0)),
        pl.BlockSpec((None, N_MOD, 8, D_MODEL), lambda i, f: (layer, 0, 0, 0)),
    ]


def _ffn_mid(layer, hn2, x1, wg_b, wu_b, wd_b, mod, norm_g):
    row = lambda i, f: (i, 0)
    return pl.pallas_call(
        _ffn_mid_kernel,
        grid=(ROWS // FFN_TM, FFN_STEPS),
        in_specs=_ffn_specs(layer) + [
            pl.BlockSpec((None, N_MOD, 8, D_MODEL), lambda i, f: (layer + 1, 0, 0, 0)),
            pl.BlockSpec((None, 2, D_MODEL), lambda i, f: (layer + 1, 0, 0)),
        ],
        out_specs=[pl.BlockSpec((FFN_TM, D_MODEL), row), pl.BlockSpec((FFN_TM, D_MODEL), row)],
        out_shape=[
            jax.ShapeDtypeStruct((ROWS, D_MODEL), _F32),
            jax.ShapeDtypeStruct((ROWS, D_MODEL), _BF),
        ],
        scratch_shapes=[pltpu.VMEM((FFN_TM, D_MODEL), _F32)],
        compiler_params=_cparams(("arbitrary", "arbitrary")),
        name=f"ffn{layer}",
    )(hn2, x1, wg_b, wu_b, wd_b, mod, mod, norm_g)


def _ffn_last(layer, hn2, x1, wg_b, wu_b, wd_b, mod, final_g):
    nct = CTX_ROWS // FFN_TM
    return pl.pallas_call(
        _ffn_last_kernel,
        grid=(ROWS // FFN_TM, FFN_STEPS),
        in_specs=_ffn_specs(layer) + [pl.BlockSpec((1, D_MODEL), lambda i, f: (0, 0))],
        out_specs=[
            pl.BlockSpec((FFN_TM, D_MODEL), lambda i, f: (jnp.minimum(i, nct - 1), 0)),
            pl.BlockSpec((FFN_TM, D_MODEL), lambda i, f: (jnp.maximum(i - nct, 0), 0)),
        ],
        out_shape=[
            jax.ShapeDtypeStruct((CTX_ROWS, D_MODEL), _F32),
            jax.ShapeDtypeStruct((LAT_ROWS, D_MODEL), _F32),
        ],
        scratch_shapes=[pltpu.VMEM((FFN_TM, D_MODEL), _F32)],
        compiler_params=_cparams(("arbitrary", "arbitrary")),
        name=f"ffn{layer}",
    )(hn2, x1, wg_b, wu_b, wd_b, mod, final_g.reshape(1, D_MODEL))


def _pos_table():
    rows = DEC_SEQ // GRID_W
    r = np.repeat(np.arange(rows, dtype=np.float32), GRID_W)
    col = np.tile(np.arange(GRID_W, dtype=np.float32), rows)
    n_freq = D_MODEL // 4
    freqs = (1.0 / (np.float32(POS_BASE) ** (np.arange(n_freq, dtype=np.float32) / np.float32(n_freq)))).astype(np.float32)
    er = r[:, None] * freqs
    ec = col[:, None] * freqs
    return np.concatenate([np.sin(er), np.cos(er), np.sin(ec), np.cos(ec)], axis=-1).astype(np.float32)


def kernel(x_prompt, x_sample, state_rglru, c, c_ctx, w_mod, b_mod, norm_g, w_in, lru_conv_w, lru_conv_b,
           lru_w_a, lru_b_a, lru_w_i, lru_b_i, lru_lambda, sc_conv_w, w_out, w_gate, w_up, w_down, final_g):
    c8 = jnp.concatenate([c_ctx[None, :], c, jnp.zeros((8 - 1 - DEC_BATCH, D_MODEL), _F32)], axis=0)
    mod = _modulation(c8, w_mod, b_mod)

    pos = jnp.asarray(_pos_table())
    hn, x = _prenorm(x_prompt.reshape(CTX_ROWS, D_MODEL), x_sample.reshape(LAT_ROWS, D_MODEL), pos, mod, norm_g)

    cb = lru_conv_b.reshape(DEPTH, 1, LRU_WIDTH)
    b_a = lru_b_a.reshape(DEPTH, 2, LRU_WIDTH)
    b_i = lru_b_i.reshape(DEPTH, 2, LRU_WIDTH)
    h0 = jnp.pad(jnp.transpose(state_rglru, (1, 2, 0, 3)), ((0, 0), (0, 0), (0, 8 - DEC_BATCH), (0, 0)))

    states = []
    for layer in range(DEPTH):
        z, wg_b, wu_b, wd_b = _inproj(layer, hn, w_in, w_gate, w_up, w_down)
        lru, sc, st = _mixer(layer, z, lru_conv_w, cb, lru_w_a, lru_w_i, b_a, b_i, lru_lambda, sc_conv_w, h0[layer])
        states.append(st[:, :BATCH, :])
        x1, hn2 = _outproj(layer, lru, sc, x, w_out, mod, norm_g)
        if layer + 1 < DEPTH:
            x, hn = _ffn_mid(layer, hn2, x1, wg_b, wu_b, wd_b, mod, norm_g)
        else:
            y_p, y_s = _ffn_last(layer, hn2, x1, wg_b, wu_b, wd_b, mod, final_g)

    new_state = jnp.transpose(jnp.stack(states, axis=0), (2, 0, 1, 3))
    return (y_p.reshape(BATCH, SEQ, D_MODEL), y_s.reshape(DEC_BATCH, DEC_SEQ, D_MODEL), new_state)
```

```python
import math

import numpy as np
import jax
import jax.numpy as jnp
from jax import lax
from jax.experimental import pallas as pl
from jax.experimental.pallas import tpu as pltpu

D_MODEL = 2048
BATCH = 32
SEQ = 256
DEPTH = 2
DEC_BATCH = 2
DEC_SEQ = 2048
GRID_W = 64
LRU_WIDTH = 1024
LRU_HEADS = 8
HEAD_DIM = 128
LRU_CONV = 4
LRU_C = 8.0
CONV_WIDTH = 1024
SHORT_CONV = 3
D_IN = 2 * LRU_WIDTH + 3 * CONV_WIDTH
D_FF = 5632
N_MOD = 6
EPS = 1e-6
POS_BASE = 10000.0

CTX_ROWS = BATCH * SEQ
LAT_ROWS = DEC_BATCH * DEC_SEQ
ROWS = CTX_ROWS + LAT_ROWS
N_SLAB = D_IN // HEAD_DIM

GROUP_ROWS = 2048
CHUNK = 256
N_CHUNK = GROUP_ROWS // CHUNK
N_GROUPS = ROWS // GROUP_ROWS
N_CTX_GROUPS = CTX_ROWS // GROUP_ROWS
SUBLANES = 8
PITCH = CHUNK + SUBLANES
XPITCH = CHUNK + 2 * SUBLANES

V7X_VMEM_LIMIT = 56 * 1024 * 1024
F32_TINY = float(np.finfo(np.float32).tiny)
GELU_C = math.sqrt(2.0 / math.pi)
GELU_C3 = 0.044715 * GELU_C

_BF = jnp.bfloat16
_F32 = jnp.float32


def _cparams(sem):
    return pltpu.CompilerParams(dimension_semantics=sem, vmem_limit_bytes=V7X_VMEM_LIMIT)


def _sigmoid(x):
    return 0.5 * jnp.tanh(0.5 * x) + 0.5


def _row_group(tile, rows_per_tile):
    return jnp.maximum((tile * rows_per_tile) // DEC_SEQ - CTX_ROWS // DEC_SEQ + 1, 0)


def _mod_row(mod_ref, j, grp):
    return mod_ref[j, pl.ds(grp, 1), :]


def _rms(x):
    return x * lax.rsqrt(jnp.mean(x * x, axis=-1, keepdims=True) + EPS)


MOD_TN = 1024


def _mod_kernel(c_ref, w_ref, b_ref, o_ref):
    c = c_ref[...]
    s = (c * _sigmoid(c)).astype(_BF)
    o_ref[...] = jnp.dot(s, w_ref[...].astype(_BF), preferred_element_type=_F32) + b_ref[...]


def _modulation(c8, w_mod, b_mod):
    per = D_MODEL // MOD_TN
    b4 = b_mod.reshape(DEPTH, N_MOD, 1, D_MODEL)
    return pl.pallas_call(
        _mod_kernel,
        grid=(DEPTH, N_MOD * per),
        in_specs=[
            pl.BlockSpec((8, D_MODEL), lambda l, n: (0, 0)),
            pl.BlockSpec((None, D_MODEL, MOD_TN), lambda l, n: (l, 0, n)),
            pl.BlockSpec((None, None, 1, MOD_TN), lambda l, n: (l, n // per, 0, n % per)),
        ],
        out_specs=pl.BlockSpec((None, None, 8, MOD_TN), lambda l, n: (l, n // per, 0, n % per)),
        out_shape=jax.ShapeDtypeStruct((DEPTH, N_MOD, 8, D_MODEL), _F32),
        compiler_params=_cparams(("arbitrary", "arbitrary")),
        name="modulation",
    )(c8, w_mod, b4)


PRE_TM = 512


def _prenorm_kernel(xp_ref, xs_ref, pos_ref, mod_ref, g_ref, hn_ref, x0_ref):
    i = pl.program_id(0)
    is_lat = i >= CTX_ROWS // PRE_TM
    grp = _row_group(i, PRE_TM)
    x = jnp.where(is_lat, xs_ref[...] + pos_ref[...], xp_ref[...])
    x0_ref[...] = x
    y = _rms(x) * g_ref[0:1, :]
    hn = y * (1.0 + _mod_row(mod_ref, 1, grp)) + _mod_row(mod_ref, 0, grp)
    hn_ref[...] = hn.astype(_BF)


def _prenorm(xp, xs, pos, mod, norm_g):
    nct = CTX_ROWS // PRE_TM
    npos = DEC_SEQ // PRE_TM
    return pl.pallas_call(
        _prenorm_kernel,
        grid=(ROWS // PRE_TM,),
        in_specs=[
            pl.BlockSpec((PRE_TM, D_MODEL), lambda i: (jnp.minimum(i, nct - 1), 0)),
            pl.BlockSpec((PRE_TM, D_MODEL), lambda i: (jnp.maximum(i - nct, 0), 0)),
            pl.BlockSpec((PRE_TM, D_MODEL), lambda i: (jnp.maximum(i - nct, 0) % npos, 0)),
            pl.BlockSpec((None, N_MOD, 8, D_MODEL), lambda i: (0, 0, 0, 0)),
            pl.BlockSpec((None, 2, D_MODEL), lambda i: (0, 0, 0)),
        ],
        out_specs=[
            pl.BlockSpec((PRE_TM, D_MODEL), lambda i: (i, 0)),
            pl.BlockSpec((PRE_TM, D_MODEL), lambda i: (i, 0)),
        ],
        out_shape=[
            jax.ShapeDtypeStruct((ROWS, D_MODEL), _BF),
            jax.ShapeDtypeStruct((ROWS, D_MODEL), _F32),
        ],
        compiler_params=_cparams(("arbitrary",)),
        name="prenorm",
    )(xp, xs, pos, mod, norm_g)


IN_TM = 1024
IN_TN = 1024
IN_STEPS = (D_IN // IN_TN) * (ROWS // IN_TM)
CAST_STEPS = 32
CAST_ROWS_IN = D_MODEL // CAST_STEPS
CAST_ROWS_FF = D_FF // CAST_STEPS


def _inproj_kernel(hn_ref, w_ref, wg_ref, wu_ref, wd_ref, z_ref, wgb_ref, wub_ref, wdb_ref, wb_scr):
    @pl.when(pl.program_id(1) == 0)
    def _():
        wb_scr[...] = w_ref[...].astype(_BF)

    step = pl.program_id(0) * (ROWS // IN_TM) + pl.program_id(1)

    @pl.when(step < CAST_STEPS)
    def _():
        wgb_ref[...] = wg_ref[...].astype(_BF)
        wub_ref[...] = wu_ref[...].astype(_BF)

    @pl.when(step >= IN_STEPS - CAST_STEPS)
    def _():
        wdb_ref[...] = wd_ref[...].astype(_BF)

    acc = jnp.dot(hn_ref[...], wb_scr[...], preferred_element_type=_F32)
    for j in range(IN_TN // HEAD_DIM):
        z_ref[j] = acc[:, j * HEAD_DIM:(j + 1) * HEAD_DIM]


def _inproj(layer, hn, w_in, w_gate, w_up, w_down):
    assert CAST_STEPS <= IN_STEPS
    step = lambda n, i: n * (ROWS // IN_TM) + i
    cast_block = lambda n, i: (jnp.minimum(step(n, i), CAST_STEPS - 1), 0)
    late_block = lambda n, i: (jnp.maximum(step(n, i) - (IN_STEPS - CAST_STEPS), 0), 0)
    in_w = pl.BlockSpec((None, CAST_ROWS_IN, D_FF), lambda n, i: (layer,) + cast_block(n, i))
    ff_w = pl.BlockSpec((None, CAST_ROWS_FF, D_MODEL), lambda n, i: (layer,) + late_block(n, i))
    return pl.pallas_call(
        _inproj_kernel,
        grid=(D_IN // IN_TN, ROWS // IN_TM),
        in_specs=[
            pl.BlockSpec((IN_TM, D_MODEL), lambda n, i: (i, 0)),
            pl.BlockSpec((None, D_MODEL, IN_TN), lambda n, i: (layer, 0, n)),
            in_w, in_w, ff_w,
        ],
        out_specs=[
            pl.BlockSpec((IN_TN // HEAD_DIM, IN_TM, HEAD_DIM), lambda n, i: (n, i, 0)),
            pl.BlockSpec((CAST_ROWS_IN, D_FF), cast_block),
            pl.BlockSpec((CAST_ROWS_IN, D_FF), cast_block),
            pl.BlockSpec((CAST_ROWS_FF, D_MODEL), late_block),
        ],
        out_shape=[
            jax.ShapeDtypeStruct((N_SLAB, ROWS, HEAD_DIM), _F32),
            jax.ShapeDtypeStruct((D_MODEL, D_FF), _BF),
            jax.ShapeDtypeStruct((D_MODEL, D_FF), _BF),
            jax.ShapeDtypeStruct((D_FF, D_MODEL), _BF),
        ],
        scratch_shapes=[pltpu.VMEM((D_MODEL, IN_TN), _BF)],
        compiler_params=_cparams(("arbitrary", "arbitrary")),
        name=f"inproj{layer}",
    )(hn, w_in, w_gate, w_up, w_down)


def _mixer_kernel(xl_ref, yg_ref, bg_ref, cg_ref, v_ref, cw_ref, cb_ref, wa_ref, wi_ref, ba_ref, bi_ref,
                  lam_ref, scw_ref, h0_ref, lru_ref, sc_ref, st_ref,
                  a_scr, u_scr, h_scr, pad_scr, pe_scr, hc_scr):
    g = pl.program_id(0)
    is_lat = g >= N_CTX_GROUPS

    def fill_padded(rows):
        zeros = jnp.zeros((SUBLANES, HEAD_DIM), _F32)
        for s in range(N_CHUNK):
            base = s * XPITCH
            lo = s * CHUNK
            front = jnp.where(is_lat, rows(lo - SUBLANES, SUBLANES), 0.0) if s > 0 else zeros
            back = jnp.where(is_lat, rows(lo + CHUNK, SUBLANES), 0.0) if s + 1 < N_CHUNK else zeros
            pad_scr[base:base + SUBLANES, :] = front
            pad_scr[base + SUBLANES:base + SUBLANES + CHUNK, :] = rows(lo, CHUNK)
            pad_scr[base + SUBLANES + CHUNK:base + XPITCH, :] = back

    def tap(s, k):
        start = s * XPITCH + SUBLANES + k
        return pad_scr[start:start + CHUNK, :]

    fill_padded(lambda lo, n: xl_ref[0, lo:lo + n, :])
    wcat = (0.5 * jnp.concatenate([wa_ref[0, 0], wi_ref[0, 0], wa_ref[1, 0], wi_ref[1, 0]], axis=1)).astype(_BF)
    half_ba = 0.5 * ba_ref[...]
    half_bi = 0.5 * bi_ref[...]
    nlam = -lam_ref[...]
    softplus = jnp.maximum(nlam, 0.0) + jnp.log1p(jnp.exp(-jnp.abs(nlam)))
    half_rate = (-0.5 * LRU_C) * softplus

    for s in range(N_CHUNK):
        xc = (cw_ref[2:3, :] * xl_ref[0, s * CHUNK:(s + 1) * CHUNK, :] + cb_ref[...]
              + cw_ref[0:1, :] * tap(s, -2) + cw_ref[1:2, :] * tap(s, -1) + cw_ref[3:4, :] * tap(s, 1))
        gates = jnp.dot(xc.astype(_BF), wcat, preferred_element_type=_F32)
        for d in range(2):
            half_ra = gates[:, d * 256:d * 256 + HEAD_DIM] + half_ba[d:d + 1, :]
            half_ia = gates[:, d * 256 + HEAD_DIM:(d + 1) * 256] + half_bi[d:d + 1, :]
            log_a = half_rate[d:d + 1, :] * (1.0 + jnp.tanh(half_ra))
            a = jnp.exp(log_a)
            th = jnp.tanh(log_a)
            q = (-0.5 * th) / (1.0 - th)
            root = jnp.maximum(q, 0.0) * lax.rsqrt(jnp.maximum(q, F32_TINY))
            a_scr[d, s * PITCH:s * PITCH + CHUNK, :] = a
            u_scr[d, s * PITCH:s * PITCH + CHUNK, :] = root * ((1.0 + jnp.tanh(half_ia)) * xc)

    def strided(t):
        return pl.ds(t, N_CHUNK, stride=PITCH)

    @pl.when(jnp.logical_not(is_lat))
    def _():
        hc_scr[...] = jnp.zeros_like(hc_scr)

    @pl.when(is_lat)
    def _():
        def body(t, c):
            pf, ef, pb, eb = c
            af = a_scr[0, strided(t), :]
            ef = af * ef + u_scr[0, strided(t), :]
            pf = pf * af
            tb = CHUNK - 1 - t
            ab = a_scr[1, strided(tb), :]
            eb = ab * eb + u_scr[1, strided(tb), :]
            pb = pb * ab
            return pf, ef, pb, eb

        one = jnp.ones((N_CHUNK, HEAD_DIM), _F32)
        zero = jnp.zeros((N_CHUNK, HEAD_DIM), _F32)
        pf, ef, pb, eb = lax.fori_loop(0, CHUNK, body, (one, zero, one, zero), unroll=8)
        pe_scr[0] = pf
        pe_scr[1] = ef
        pe_scr[2] = pb
        pe_scr[3] = eb
        seq = g - N_CTX_GROUPS
        h = h0_ref[0, pl.ds(seq, 1), :]
        for c in range(N_CHUNK):
            hc_scr[0, c:c + 1, :] = h
            h = pe_scr[0, c:c + 1, :] * h + pe_scr[1, c:c + 1, :]
        h = h0_ref[1, pl.ds(seq, 1), :]
        for c in reversed(range(N_CHUNK)):
            hc_scr[1, c:c + 1, :] = h
            h = pe_scr[2, c:c + 1, :] * h + pe_scr[3, c:c + 1, :]

    def scan_body(t, c):
        hf, hb = c
        hf = a_scr[0, strided(t), :] * hf + u_scr[0, strided(t), :]
        h_scr[0, strided(t), :] = hf
        tb = CHUNK - 1 - t
        hb = a_scr[1, strided(tb), :] * hb + u_scr[1, strided(tb), :]
        h_scr[1, strided(tb), :] = hb
        return hf, hb

    hf, hb = lax.fori_loop(0, CHUNK, scan_body, (hc_scr[0], hc_scr[1]), unroll=8)
    st_ref[0] = hf
    st_ref[1] = hb

    for s in range(N_CHUNK):
        y = yg_ref[0, s * CHUNK:(s + 1) * CHUNK, :]
        gelu = (0.5 * y) * (1.0 + jnp.tanh(y * (GELU_C + GELU_C3 * (y * y))))
        o = h_scr[0, s * PITCH:s * PITCH + CHUNK, :] + h_scr[1, s * PITCH:s * PITCH + CHUNK, :]
        lru_ref[s * CHUNK:(s + 1) * CHUNK, :] = (o * gelu).astype(_BF)

    fill_padded(lambda lo, n: cg_ref[0, lo:lo + n, :] * v_ref[0, lo:lo + n, :])
    for s in range(N_CHUNK):
        conv = scw_ref[1:2, :] * tap(s, 0) + scw_ref[0:1, :] * tap(s, -1) + scw_ref[2:3, :] * tap(s, 1)
        sc_ref[s * CHUNK:(s + 1) * CHUNK, :] = (bg_ref[0, s * CHUNK:(s + 1) * CHUNK, :] * conv).astype(_BF)


def _mixer(layer, z, cw, cb, w_a, w_i, b_a, b_i, lam, scw, h0):
    def zspec(off):
        return pl.BlockSpec((1, GROUP_ROWS, HEAD_DIM), lambda g, h: (off + h, g, 0))

    def vec(rows):
        return pl.BlockSpec((None, rows, HEAD_DIM), lambda g, h: (layer, 0, h))

    wspec = pl.BlockSpec((None, 2, 1, HEAD_DIM, HEAD_DIM), lambda g, h: (layer, 0, h, 0, 0))
    scan_buf = pltpu.VMEM((2, N_CHUNK * PITCH, HEAD_DIM), _F32)
    return pl.pallas_call(
        _mixer_kernel,
        grid=(N_GROUPS, LRU_HEADS),
        in_specs=[
            zspec(0), zspec(8), zspec(16), zspec(24), zspec(32),
            vec(LRU_CONV), vec(1), wspec, wspec, vec(2), vec(2), vec(2), vec(SHORT_CONV),
            pl.BlockSpec((2, 8, HEAD_DIM), lambda g, h: (0, 0, h)),
        ],
        out_specs=[
            pl.BlockSpec((GROUP_ROWS, HEAD_DIM), lambda g, h: (g, h)),
            pl.BlockSpec((GROUP_ROWS, HEAD_DIM), lambda g, h: (g, h)),
            pl.BlockSpec((2, N_CHUNK, HEAD_DIM), lambda g, h: (0, g, h)),
        ],
        out_shape=[
            jax.ShapeDtypeStruct((ROWS, LRU_WIDTH), _BF),
            jax.ShapeDtypeStruct((ROWS, CONV_WIDTH), _BF),
            jax.ShapeDtypeStruct((2, N_GROUPS * N_CHUNK, LRU_WIDTH), _F32),
        ],
        scratch_shapes=[
            scan_buf, scan_buf, scan_buf,
            pltpu.VMEM((N_CHUNK * XPITCH, HEAD_DIM), _F32),
            pltpu.VMEM((4, N_CHUNK, HEAD_DIM), _F32),
            pltpu.VMEM((2, N_CHUNK, HEAD_DIM), _F32),
        ],
        compiler_params=_cparams(("arbitrary", "arbitrary")),
        name=f"mixer{layer}",
    )(z, z, z, z, z, cw, cb, w_a, w_i, b_a, b_i, lam, scw, h0)


OUT_TM = 512


def _outproj_kernel(lru_ref, sc_ref, x_ref, w_ref, mod_ref, g_ref, x1_ref, hn2_ref, wb_scr):
    @pl.when(pl.program_id(0) == 0)
    def _():
        wb_scr[...] = w_ref[...].astype(_BF)

    grp = _row_group(pl.program_id(0), OUT_TM)
    gate = _mod_row(mod_ref, 2, grp)
    scale = g_ref[1:2, :] * (1.0 + _mod_row(mod_ref, 4, grp))
    shift = _mod_row(mod_ref, 3, grp)
    for r0 in range(0, OUT_TM, OUT_TM // 2):
        rows = slice(r0, r0 + OUT_TM // 2)
        m = jnp.dot(lru_ref[rows, :], wb_scr[0:LRU_WIDTH, :], preferred_element_type=_F32)
        m = m + jnp.dot(sc_ref[rows, :], wb_scr[LRU_WIDTH:, :], preferred_element_type=_F32)
        x1 = x_ref[rows, :] + gate * m
        x1_ref[rows, :] = x1
        hn2_ref[rows, :] = (_rms(x1) * scale + shift).astype(_BF)


def _outproj(layer, lru, sc, x, w_out, mod, norm_g):
    row = lambda i: (i, 0)
    return pl.pallas_call(
        _outproj_kernel,
        grid=(ROWS // OUT_TM,),
        in_specs=[
            pl.BlockSpec((OUT_TM, LRU_WIDTH), row),
            pl.BlockSpec((OUT_TM, CONV_WIDTH), row),
            pl.BlockSpec((OUT_TM, D_MODEL), row),
            pl.BlockSpec((None, D_MODEL, D_MODEL), lambda i: (layer, 0, 0), pipeline_mode=pl.Buffered(1)),
            pl.BlockSpec((None, N_MOD, 8, D_MODEL), lambda i: (layer, 0, 0, 0)),
            pl.BlockSpec((None, 2, D_MODEL), lambda i: (layer, 0, 0)),
        ],
        out_specs=[pl.BlockSpec((OUT_TM, D_MODEL), row), pl.BlockSpec((OUT_TM, D_MODEL), row)],
        out_shape=[
            jax.ShapeDtypeStruct((ROWS, D_MODEL), _F32),
            jax.ShapeDtypeStruct((ROWS, D_MODEL), _BF),
        ],
        scratch_shapes=[pltpu.VMEM((D_MODEL, D_MODEL), _BF)],
        compiler_params=_cparams(("arbitrary",)),
        name=f"outproj{layer}",
    )(lru, sc, x, w_out, mod, norm_g)


FFN_TM = 1024
FFN_TF = 512
FFN_STEPS = D_FF // FFN_TF
EPI_TM = 256
EPI_STEPS = FFN_TM // EPI_TM


def _ffn_accumulate(hn_ref, wg_ref, wu_ref, wd_ref, acc_ref):
    f = pl.program_id(1)

    @pl.when(f == 0)
    def _():
        acc_ref[...] = jnp.zeros_like(acc_ref)

    @pl.when(f < FFN_STEPS)
    def _():
        h = hn_ref[...]
        gate = jnp.dot(h, wg_ref[...], preferred_element_type=_F32)
        up = jnp.dot(h, wu_ref[...], preferred_element_type=_F32)
        act = (gate * _sigmoid(gate) * up).astype(_BF)
        acc_ref[...] += jnp.dot(act, wd_ref[...], preferred_element_type=_F32)


def _ffn_residual(x1_ref, mod_ref, acc_ref):
    e = pl.program_id(1) - FFN_STEPS
    grp = _row_group(pl.program_id(0), FFN_TM)
    rows = pl.ds(pl.multiple_of(e * EPI_TM, EPI_TM), EPI_TM)
    return x1_ref[...] + _mod_row(mod_ref, 5, grp) * acc_ref[rows, :], grp


def _ffn_mid_kernel(hn_ref, x1_ref, wg_ref, wu_ref, wd_ref, mod_ref, modn_ref, gn_ref, x2_ref, hnn_ref, acc_ref):
    _ffn_accumulate(hn_ref, wg_ref, wu_ref, wd_ref, acc_ref)

    @pl.when(pl.program_id(1) >= FFN_STEPS)
    def _():
        x2, grp = _ffn_residual(x1_ref, mod_ref, acc_ref)
        x2_ref[...] = x2
        y = _rms(x2) * gn_ref[0:1, :]
        hnn_ref[...] = (y * (1.0 + _mod_row(modn_ref, 1, grp)) + _mod_row(modn_ref, 0, grp)).astype(_BF)


def _ffn_last_kernel(hn_ref, x1_ref, wg_ref, wu_ref, wd_ref, mod_ref, fg_ref, yp_ref, ys_ref, acc_ref):
    _ffn_accumulate(hn_ref, wg_ref, wu_ref, wd_ref, acc_ref)
    epilogue = pl.program_id(1) >= FFN_STEPS
    is_lat = pl.program_id(0) >= CTX_ROWS // FFN_TM

    def result():
        x2, _ = _ffn_residual(x1_ref, mod_ref, acc_ref)
        return _rms(x2) * fg_ref[...]

    @pl.when(epilogue & jnp.logical_not(is_lat))
    def _():
        yp_ref[...] = result()

    @pl.when(epilogue & is_lat)
    def _():
        ys_ref[...] = result()


def _epi_tile(i, f):
    return i * EPI_STEPS + jnp.maximum(f - FFN_STEPS, 0)


def _ffn_specs(layer):
    chunk = lambda f: jnp.minimum(f, FFN_STEPS - 1)
    return [
        pl.BlockSpec((FFN_TM, D_MODEL), lambda i, f: (i, 0)),
        pl.BlockSpec((EPI_TM, D_MODEL), lambda i, f: (_epi_tile(i, f), 0)),
        pl.BlockSpec((D_MODEL, FFN_TF), lambda i, f: (0, chunk(f))),
        pl.BlockSpec((D_MODEL, FFN_TF), lambda i, f: (0, chunk(f))),
        pl.BlockSpec((FFN_TF, D_MODEL), lambda i, f: (chunk(f), 0)),
        pl.BlockSpec((None, N_MOD, 8, D_MODEL), lambda i, f: (layer, 0, 0, 0)),
    ]


def _ffn_mid(layer, hn2, x1, wg_b, wu_b, wd_b, mod, norm_g):
    row = lambda i, f: (_epi_tile(i, f), 0)
    return pl.pallas_call(
        _ffn_mid_kernel,
        grid=(ROWS // FFN_TM, FFN_STEPS + EPI_STEPS),
        in_specs=_ffn_specs(layer) + [
            pl.BlockSpec((None, N_MOD, 8, D_MODEL), lambda i, f: (layer + 1, 0, 0, 0)),
            pl.BlockSpec((None, 2, D_MODEL), lambda i, f: (layer + 1, 0, 0)),
        ],
        out_specs=[pl.BlockSpec((EPI_TM, D_MODEL), row), pl.BlockSpec((EPI_TM, D_MODEL), row)],
        out_shape=[
            jax.ShapeDtypeStruct((ROWS, D_MODEL), _F32),
            jax.ShapeDtypeStruct((ROWS, D_MODEL), _BF),
        ],
        scratch_shapes=[pltpu.VMEM((FFN_TM, D_MODEL), _F32)],
        compiler_params=_cparams(("arbitrary", "arbitrary")),
        name=f"ffn{layer}",
    )(hn2, x1, wg_b, wu_b, wd_b, mod, mod, norm_g)


def _ffn_last(layer, hn2, x1, wg_b, wu_b, wd_b, mod, final_g):
    nct = CTX_ROWS // EPI_TM
    return pl.pallas_call(
        _ffn_last_kernel,
        grid=(ROWS // FFN_TM, FFN_STEPS + EPI_STEPS),
        in_specs=_ffn_specs(layer) + [pl.BlockSpec((1, D_MODEL), lambda i, f: (0, 0))],
        out_specs=[
            pl.BlockSpec((EPI_TM, D_MODEL), lambda i, f: (jnp.minimum(_epi_tile(i, f), nct - 1), 0)),
            pl.BlockSpec((EPI_TM, D_MODEL), lambda i, f: (jnp.maximum(_epi_tile(i, f) - nct, 0), 0)),
        ],
        out_shape=[
            jax.ShapeDtypeStruct((CTX_ROWS, D_MODEL), _F32),
            jax.ShapeDtypeStruct((LAT_ROWS, D_MODEL), _F32),
        ],
        scratch_shapes=[pltpu.VMEM((FFN_TM, D_MODEL), _F32)],
        compiler_params=_cparams(("arbitrary", "arbitrary")),
        name=f"ffn{layer}",
    )(hn2, x1, wg_b, wu_b, wd_b, mod, final_g.reshape(1, D_MODEL))


def _pos_table():
    rows = DEC_SEQ // GRID_W
    r = np.repeat(np.arange(rows, dtype=np.float32), GRID_W)
    col = np.tile(np.arange(GRID_W, dtype=np.float32), rows)
    n_freq = D_MODEL // 4
    freqs = (1.0 / (np.float32(POS_BASE) ** (np.arange(n_freq, dtype=np.float32) / np.float32(n_freq)))).astype(np.float32)
    er = r[:, None] * freqs
    ec = col[:, None] * freqs
    return np.concatenate([np.sin(er), np.cos(er), np.sin(ec), np.cos(ec)], axis=-1).astype(np.float32)


def kernel(x_prompt, x_sample, state_rglru, c, c_ctx, w_mod, b_mod, norm_g, w_in, lru_conv_w, lru_conv_b,
           lru_w_a, lru_b_a, lru_w_i, lru_b_i, lru_lambda, sc_conv_w, w_out, w_gate, w_up, w_down, final_g):
    c8 = jnp.concatenate([c_ctx[None, :], c, jnp.zeros((8 - 1 - DEC_BATCH, D_MODEL), _F32)], axis=0)
    mod = _modulation(c8, w_mod, b_mod)

    pos = jnp.asarray(_pos_table())
    hn, x = _prenorm(x_prompt.reshape(CTX_ROWS, D_MODEL), x_sample.reshape(LAT_ROWS, D_MODEL), pos, mod, norm_g)

    cb = lru_conv_b.reshape(DEPTH, 1, LRU_WIDTH)
    b_a = lru_b_a.reshape(DEPTH, 2, LRU_WIDTH)
    b_i = lru_b_i.reshape(DEPTH, 2, LRU_WIDTH)
    h0 = jnp.pad(jnp.transpose(state_rglru, (1, 2, 0, 3)), ((0, 0), (0, 0), (0, 8 - DEC_BATCH), (0, 0)))

    states = []
    for layer in range(DEPTH):
        z, wg_b, wu_b, wd_b = _inproj(layer, hn, w_in, w_gate, w_up, w_down)
        lru, sc, st = _mixer(layer, z, lru_conv_w, cb, lru_w_a, lru_w_i, b_a, b_i, lru_lambda, sc_conv_w, h0[layer])
        states.append(st[:, :BATCH, :])
        x1, hn2 = _outproj(layer, lru, sc, x, w_out, mod, norm_g)
        if layer + 1 < DEPTH:
            x, hn = _ffn_mid(layer, hn2, x1, wg_b, wu_b, wd_b, mod, norm_g)
        else:
            y_p, y_s = _ffn_last(layer, hn2, x1, wg_b, wu_b, wd_b, mod, final_g)

    new_state = jnp.transpose(jnp.stack(states, axis=0), (2, 0, 1, 3))
    return (y_p.reshape(BATCH, SEQ, D_MODEL), y_s.reshape(DEC_BATCH, DEC_SEQ, D_MODEL), new_state)
```

```python
import math

import numpy as np
import jax
import jax.numpy as jnp
from jax import lax
from jax.experimental import pallas as pl
from jax.experimental.pallas import tpu as pltpu

D_MODEL = 2048
BATCH = 32
SEQ = 256
DEPTH = 2
DEC_BATCH = 2
DEC_SEQ = 2048
GRID_W = 64
LRU_WIDTH = 1024
LRU_HEADS = 8
HEAD_DIM = 128
LRU_CONV = 4
LRU_C = 8.0
CONV_WIDTH = 1024
SHORT_CONV = 3
D_IN = 2 * LRU_WIDTH + 3 * CONV_WIDTH
D_FF = 5632
N_MOD = 6
EPS = 1e-6
POS_BASE = 10000.0

CTX_ROWS = BATCH * SEQ
LAT_ROWS = DEC_BATCH * DEC_SEQ
ROWS = CTX_ROWS + LAT_ROWS
N_SLAB = D_IN // HEAD_DIM

GROUP_ROWS = 2048
CHUNK = 256
N_CHUNK = GROUP_ROWS // CHUNK
N_GROUPS = ROWS // GROUP_ROWS
N_CTX_GROUPS = CTX_ROWS // GROUP_ROWS
SUBLANES = 8
PITCH = CHUNK + SUBLANES
XPITCH = CHUNK + 2 * SUBLANES

V7X_VMEM_LIMIT = 56 * 1024 * 1024
F32_TINY = float(np.finfo(np.float32).tiny)
GELU_C = math.sqrt(2.0 / math.pi)
GELU_C3 = 0.044715 * GELU_C

_BF = jnp.bfloat16
_F32 = jnp.float32


def _cparams(sem):
    return pltpu.CompilerParams(dimension_semantics=sem, vmem_limit_bytes=V7X_VMEM_LIMIT)


def _sigmoid(x):
    return 0.5 * jnp.tanh(0.5 * x) + 0.5


def _row_group(tile, rows_per_tile):
    return jnp.maximum((tile * rows_per_tile) // DEC_SEQ - CTX_ROWS // DEC_SEQ + 1, 0)


def _mod_row(mod_ref, j, grp):
    return mod_ref[j, pl.ds(grp, 1), :]


def _rms(x):
    return x * lax.rsqrt(jnp.mean(x * x, axis=-1, keepdims=True) + EPS)


MOD_TN = 1024


def _mod_kernel(c_ref, w_ref, b_ref, o_ref):
    c = c_ref[...]
    s = (c * _sigmoid(c)).astype(_BF)
    o_ref[...] = jnp.dot(s, w_ref[...].astype(_BF), preferred_element_type=_F32) + b_ref[...]


def _modulation(c8, w_mod, b_mod):
    per = D_MODEL // MOD_TN
    b4 = b_mod.reshape(DEPTH, N_MOD, 1, D_MODEL)
    return pl.pallas_call(
        _mod_kernel,
        grid=(DEPTH, N_MOD * per),
        in_specs=[
            pl.BlockSpec((8, D_MODEL), lambda l, n: (0, 0)),
            pl.BlockSpec((None, D_MODEL, MOD_TN), lambda l, n: (l, 0, n)),
            pl.BlockSpec((None, None, 1, MOD_TN), lambda l, n: (l, n // per, 0, n % per)),
        ],
        out_specs=pl.BlockSpec((None, None, 8, MOD_TN), lambda l, n: (l, n // per, 0, n % per)),
        out_shape=jax.ShapeDtypeStruct((DEPTH, N_MOD, 8, D_MODEL), _F32),
        compiler_params=_cparams(("arbitrary", "arbitrary")),
        name="modulation",
    )(c8, w_mod, b4)


PRE_TM = 512


def _prenorm_kernel(xp_ref, xs_ref, pos_ref, mod_ref, g_ref, hn_ref, x0_ref):
    i = pl.program_id(0)
    is_lat = i >= CTX_ROWS // PRE_TM
    grp = _row_group(i, PRE_TM)
    x = jnp.where(is_lat, xs_ref[...] + pos_ref[...], xp_ref[...])
    x0_ref[...] = x
    y = _rms(x) * g_ref[0:1, :]
    hn = y * (1.0 + _mod_row(mod_ref, 1, grp)) + _mod_row(mod_ref, 0, grp)
    hn_ref[...] = hn.astype(_BF)


def _prenorm(xp, xs, pos, mod, norm_g):
    nct = CTX_ROWS // PRE_TM
    npos = DEC_SEQ // PRE_TM
    return pl.pallas_call(
        _prenorm_kernel,
        grid=(ROWS // PRE_TM,),
        in_specs=[
            pl.BlockSpec((PRE_TM, D_MODEL), lambda i: (jnp.minimum(i, nct - 1), 0)),
            pl.BlockSpec((PRE_TM, D_MODEL), lambda i: (jnp.maximum(i - nct, 0), 0)),
            pl.BlockSpec((PRE_TM, D_MODEL), lambda i: (jnp.maximum(i - nct, 0) % npos, 0)),
            pl.BlockSpec((None, N_MOD, 8, D_MODEL), lambda i: (0, 0, 0, 0)),
            pl.BlockSpec((None, 2, D_MODEL), lambda i: (0, 0, 0)),
        ],
        out_specs=[
            pl.BlockSpec((PRE_TM, D_MODEL), lambda i: (i, 0)),
            pl.BlockSpec((PRE_TM, D_MODEL), lambda i: (i, 0)),
        ],
        out_shape=[
            jax.ShapeDtypeStruct((ROWS, D_MODEL), _BF),
            jax.ShapeDtypeStruct((ROWS, D_MODEL), _F32),
        ],
        compiler_params=_cparams(("arbitrary",)),
        name="prenorm",
    )(xp, xs, pos, mod, norm_g)


IN_TM = 1024
IN_TN = 1024
IN_STEPS = (D_IN // IN_TN) * (ROWS // IN_TM)
CAST_STEPS = 32
CAST_ROWS_IN = D_MODEL // CAST_STEPS
CAST_ROWS_FF = D_FF // CAST_STEPS


def _inproj_kernel(hn_ref, w_ref, wg_ref, wu_ref, wd_ref, z_ref, wgb_ref, wub_ref, wdb_ref, wb_scr):
    @pl.when(pl.program_id(1) == 0)
    def _():
        wb_scr[...] = w_ref[...].astype(_BF)

    step = pl.program_id(0) * (ROWS // IN_TM) + pl.program_id(1)

    @pl.when(step < CAST_STEPS)
    def _():
        wgb_ref[...] = wg_ref[...].astype(_BF)
        wub_ref[...] = wu_ref[...].astype(_BF)

    @pl.when(step >= IN_STEPS - CAST_STEPS)
    def _():
        wdb_ref[...] = wd_ref[...].astype(_BF)

    acc = jnp.dot(hn_ref[...], wb_scr[...], preferred_element_type=_F32)
    for j in range(IN_TN // HEAD_DIM):
        z_ref[j] = acc[:, j * HEAD_DIM:(j + 1) * HEAD_DIM]


def _inproj(layer, hn, w_in, w_gate, w_up, w_down):
    assert CAST_STEPS <= IN_STEPS
    step = lambda n, i: n * (ROWS // IN_TM) + i
    cast_block = lambda n, i: (jnp.minimum(step(n, i), CAST_STEPS - 1), 0)
    late_block = lambda n, i: (jnp.maximum(step(n, i) - (IN_STEPS - CAST_STEPS), 0), 0)
    in_w = pl.BlockSpec((None, CAST_ROWS_IN, D_FF), lambda n, i: (layer,) + cast_block(n, i))
    ff_w = pl.BlockSpec((None, CAST_ROWS_FF, D_MODEL), lambda n, i: (layer,) + late_block(n, i))
    return pl.pallas_call(
        _inproj_kernel,
        grid=(D_IN // IN_TN, ROWS // IN_TM),
        in_specs=[
            pl.BlockSpec((IN_TM, D_MODEL), lambda n, i: (i, 0)),
            pl.BlockSpec((None, D_MODEL, IN_TN), lambda n, i: (layer, 0, n)),
            in_w, in_w, ff_w,
        ],
        out_specs=[
            pl.BlockSpec((IN_TN // HEAD_DIM, IN_TM, HEAD_DIM), lambda n, i: (n, i, 0)),
            pl.BlockSpec((CAST_ROWS_IN, D_FF), cast_block),
            pl.BlockSpec((CAST_ROWS_IN, D_FF), cast_block),
            pl.BlockSpec((CAST_ROWS_FF, D_MODEL), late_block),
        ],
        out_shape=[
            jax.ShapeDtypeStruct((N_SLAB, ROWS, HEAD_DIM), _F32),
            jax.ShapeDtypeStruct((D_MODEL, D_FF), _BF),
            jax.ShapeDtypeStruct((D_MODEL, D_FF), _BF),
            jax.ShapeDtypeStruct((D_FF, D_MODEL), _BF),
        ],
        scratch_shapes=[pltpu.VMEM((D_MODEL, IN_TN), _BF)],
        compiler_params=_cparams(("arbitrary", "arbitrary")),
        name=f"inproj{layer}",
    )(hn, w_in, w_gate, w_up, w_down)


def _mixer_kernel(xl_ref, yg_ref, bg_ref, cg_ref, v_ref, cw_ref, cb_ref, wa_ref, wi_ref, ba_ref, bi_ref,
                  lam_ref, scw_ref, h0_ref, lru_ref, sc_ref, st_ref,
                  a_scr, u_scr, h_scr, pad_scr, pe_scr, hc_scr):
    g = pl.program_id(0)
    is_lat = g >= N_CTX_GROUPS

    def fill_padded(rows):
        zeros = jnp.zeros((SUBLANES, HEAD_DIM), _F32)
        for s in range(N_CHUNK):
            base = s * XPITCH
            lo = s * CHUNK
            front = jnp.where(is_lat, rows(lo - SUBLANES, SUBLANES), 0.0) if s > 0 else zeros
            back = jnp.where(is_lat, rows(lo + CHUNK, SUBLANES), 0.0) if s + 1 < N_CHUNK else zeros
            pad_scr[base:base + SUBLANES, :] = front
            pad_scr[base + SUBLANES:base + SUBLANES + CHUNK, :] = rows(lo, CHUNK)
            pad_scr[base + SUBLANES + CHUNK:base + XPITCH, :] = back

    def tap(s, k):
        start = s * XPITCH + SUBLANES + k
        return pad_scr[start:start + CHUNK, :]

    fill_padded(lambda lo, n: xl_ref[0, lo:lo + n, :])
    wcat = (0.5 * jnp.concatenate([wa_ref[0, 0], wi_ref[0, 0], wa_ref[1, 0], wi_ref[1, 0]], axis=1)).astype(_BF)
    half_ba = 0.5 * ba_ref[...]
    half_bi = 0.5 * bi_ref[...]
    nlam = -lam_ref[...]
    softplus = jnp.maximum(nlam, 0.0) + jnp.log1p(jnp.exp(-jnp.abs(nlam)))
    half_rate = (-0.5 * LRU_C) * softplus

    for s in range(N_CHUNK):
        xc = (cw_ref[2:3, :] * xl_ref[0, s * CHUNK:(s + 1) * CHUNK, :] + cb_ref[...]
              + cw_ref[0:1, :] * tap(s, -2) + cw_ref[1:2, :] * tap(s, -1) + cw_ref[3:4, :] * tap(s, 1))
        gates = jnp.dot(xc.astype(_BF), wcat, preferred_element_type=_F32)
        for d in range(2):
            half_ra = gates[:, d * 256:d * 256 + HEAD_DIM] + half_ba[d:d + 1, :]
            half_ia = gates[:, d * 256 + HEAD_DIM:(d + 1) * 256] + half_bi[d:d + 1, :]
            log_a = half_rate[d:d + 1, :] * (1.0 + jnp.tanh(half_ra))
            a = jnp.exp(log_a)
            th = jnp.tanh(log_a)
            q = (-0.5 * th) / (1.0 - th)
            root = jnp.maximum(q, 0.0) * lax.rsqrt(jnp.maximum(q, F32_TINY))
            a_scr[d, s * PITCH:s * PITCH + CHUNK, :] = a
            u_scr[d, s * PITCH:s * PITCH + CHUNK, :] = root * ((1.0 + jnp.tanh(half_ia)) * xc)

    def strided(t):
        return pl.ds(t, N_CHUNK, stride=PITCH)

    @pl.when(jnp.logical_not(is_lat))
    def _():
        hc_scr[...] = jnp.zeros_like(hc_scr)

    @pl.when(is_lat)
    def _():
        def body(t, c):
            pf, ef, pb, eb = c
            af = a_scr[0, strided(t), :]
            ef = af * ef + u_scr[0, strided(t), :]
            pf = pf * af
            tb = CHUNK - 1 - t
            ab = a_scr[1, strided(tb), :]
            eb = ab * eb + u_scr[1, strided(tb), :]
            pb = pb * ab
            return pf, ef, pb, eb

        one = jnp.ones((N_CHUNK, HEAD_DIM), _F32)
        zero = jnp.zeros((N_CHUNK, HEAD_DIM), _F32)
        pf, ef, pb, eb = lax.fori_loop(0, CHUNK, body, (one, zero, one, zero), unroll=8)
        pe_scr[0] = pf
        pe_scr[1] = ef
        pe_scr[2] = pb
        pe_scr[3] = eb
        seq = g - N_CTX_GROUPS
        h = h0_ref[0, pl.ds(seq, 1), :]
        for c in range(N_CHUNK):
            hc_scr[0, c:c + 1, :] = h
            h = pe_scr[0, c:c + 1, :] * h + pe_scr[1, c:c + 1, :]
        h = h0_ref[1, pl.ds(seq, 1), :]
        for c in reversed(range(N_CHUNK)):
            hc_scr[1, c:c + 1, :] = h
            h = pe_scr[2, c:c + 1, :] * h + pe_scr[3, c:c + 1, :]

    def scan_body(t, c):
        hf, hb = c
        hf = a_scr[0, strided(t), :] * hf + u_scr[0, strided(t), :]
        h_scr[0, strided(t), :] = hf
        tb = CHUNK - 1 - t
        hb = a_scr[1, strided(tb), :] * hb + u_scr[1, strided(tb), :]
        h_scr[1, strided(tb), :] = hb
        return hf, hb

    hf, hb = lax.fori_loop(0, CHUNK, scan_body, (hc_scr[0], hc_scr[1]), unroll=8)
    st_ref[0] = hf
    st_ref[1] = hb

    for s in range(N_CHUNK):
        y = yg_ref[0, s * CHUNK:(s + 1) * CHUNK, :]
        gelu = (0.5 * y) * (1.0 + jnp.tanh(y * (GELU_C + GELU_C3 * (y * y))))
        o = h_scr[0, s * PITCH:s * PITCH + CHUNK, :] + h_scr[1, s * PITCH:s * PITCH + CHUNK, :]
        lru_ref[s * CHUNK:(s + 1) * CHUNK, :] = (o * gelu).astype(_BF)

    fill_padded(lambda lo, n: cg_ref[0, lo:lo + n, :] * v_ref[0, lo:lo + n, :])
    for s in range(N_CHUNK):
        conv = scw_ref[1:2, :] * tap(s, 0) + scw_ref[0:1, :] * tap(s, -1) + scw_ref[2:3, :] * tap(s, 1)
        sc_ref[s * CHUNK:(s + 1) * CHUNK, :] = (bg_ref[0, s * CHUNK:(s + 1) * CHUNK, :] * conv).astype(_BF)


def _mixer(layer, z, cw, cb, w_a, w_i, b_a, b_i, lam, scw, h0):
    def zspec(off):
        return pl.BlockSpec((1, GROUP_ROWS, HEAD_DIM), lambda g, h: (off + h, g, 0))

    def vec(rows):
        return pl.BlockSpec((None, rows, HEAD_DIM), lambda g, h: (layer, 0, h))

    wspec = pl.BlockSpec((None, 2, 1, HEAD_DIM, HEAD_DIM), lambda g, h: (layer, 0, h, 0, 0))
    scan_buf = pltpu.VMEM((2, N_CHUNK * PITCH, HEAD_DIM), _F32)
    return pl.pallas_call(
        _mixer_kernel,
        grid=(N_GROUPS, LRU_HEADS),
        in_specs=[
            zspec(0), zspec(8), zspec(16), zspec(24), zspec(32),
            vec(LRU_CONV), vec(1), wspec, wspec, vec(2), vec(2), vec(2), vec(SHORT_CONV),
            pl.BlockSpec((2, 8, HEAD_DIM), lambda g, h: (0, 0, h)),
        ],
        out_specs=[
            pl.BlockSpec((GROUP_ROWS, HEAD_DIM), lambda g, h: (g, h)),
            pl.BlockSpec((GROUP_ROWS, HEAD_DIM), lambda g, h: (g, h)),
            pl.BlockSpec((2, N_CHUNK, HEAD_DIM), lambda g, h: (0, g, h)),
        ],
        out_shape=[
            jax.ShapeDtypeStruct((ROWS, LRU_WIDTH), _BF),
            jax.ShapeDtypeStruct((ROWS, CONV_WIDTH), _BF),
            jax.ShapeDtypeStruct((2, N_GROUPS * N_CHUNK, LRU_WIDTH), _F32),
        ],
        scratch_shapes=[
            scan_buf, scan_buf, scan_buf,
            pltpu.VMEM((N_CHUNK * XPITCH, HEAD_DIM), _F32),
            pltpu.VMEM((4, N_CHUNK, HEAD_DIM), _F32),
            pltpu.VMEM((2, N_CHUNK, HEAD_DIM), _F32),
        ],
        compiler_params=_cparams(("arbitrary", "arbitrary")),
        name=f"mixer{layer}",
    )(z, z, z, z, z, cw, cb, w_a, w_i, b_a, b_i, lam, scw, h0)


OUT_TM = 512


def _outproj_kernel(lru_ref, sc_ref, x_ref, w_ref, mod_ref, g_ref, x1_ref, hn2_ref, wb_scr):
    @pl.when(pl.program_id(0) == 0)
    def _():
        wb_scr[...] = w_ref[...].astype(_BF)

    grp = _row_group(pl.program_id(0), OUT_TM)
    gate = _mod_row(mod_ref, 2, grp)
    scale = g_ref[1:2, :] * (1.0 + _mod_row(mod_ref, 4, grp))
    shift = _mod_row(mod_ref, 3, grp)
    for r0 in range(0, OUT_TM, OUT_TM // 2):
        rows = slice(r0, r0 + OUT_TM // 2)
        m = jnp.dot(lru_ref[rows, :], wb_scr[0:LRU_WIDTH, :], preferred_element_type=_F32)
        m = m + jnp.dot(sc_ref[rows, :], wb_scr[LRU_WIDTH:, :], preferred_element_type=_F32)
        x1 = x_ref[rows, :] + gate * m
        x1_ref[rows, :] = x1
        hn2_ref[rows, :] = (_rms(x1) * scale + shift).astype(_BF)


def _outproj(layer, lru, sc, x, w_out, mod, norm_g):
    row = lambda i: (i, 0)
    return pl.pallas_call(
        _outproj_kernel,
        grid=(ROWS // OUT_TM,),
        in_specs=[
            pl.BlockSpec((OUT_TM, LRU_WIDTH), row),
            pl.BlockSpec((OUT_TM, CONV_WIDTH), row),
            pl.BlockSpec((OUT_TM, D_MODEL), row),
            pl.BlockSpec((None, D_MODEL, D_MODEL), lambda i: (layer, 0, 0), pipeline_mode=pl.Buffered(1)),
            pl.BlockSpec((None, N_MOD, 8, D_MODEL), lambda i: (layer, 0, 0, 0)),
            pl.BlockSpec((None, 2, D_MODEL), lambda i: (layer, 0, 0)),
        ],
        out_specs=[pl.BlockSpec((OUT_TM, D_MODEL), row), pl.BlockSpec((OUT_TM, D_MODEL), row)],
        out_shape=[
            jax.ShapeDtypeStruct((ROWS, D_MODEL), _F32),
            jax.ShapeDtypeStruct((ROWS, D_MODEL), _BF),
        ],
        scratch_shapes=[pltpu.VMEM((D_MODEL, D_MODEL), _BF)],
        compiler_params=_cparams(("arbitrary",)),
        name=f"outproj{layer}",
    )(lru, sc, x, w_out, mod, norm_g)


FFN_TM = 512
FFN_TF = 512
FFN_STEPS = D_FF // FFN_TF
FFN_TILES = ROWS // FFN_TM
FFN_CTX_TILES = CTX_ROWS // FFN_TM


def _finished_tile(i, f):
    final = (i == FFN_TILES - 1) & (f == FFN_STEPS - 1)
    return jnp.where(final, FFN_TILES - 1, jnp.maximum(i - 1, 0))


def _ffn_pipeline(hn_ref, wg_ref, wu_ref, wd_ref, acc_a, acc_b, finish, store):
    i = pl.program_id(0)
    f = pl.program_id(1)

    @pl.when((i == 0) & (f == 0))
    def _():
        acc_b[...] = jnp.zeros_like(acc_b)

    def chunk():
        h = hn_ref[...]
        gate = jnp.dot(h, wg_ref[...], preferred_element_type=_F32)
        up = jnp.dot(h, wu_ref[...], preferred_element_type=_F32)
        act = (gate * _sigmoid(gate) * up).astype(_BF)
        return jnp.dot(act, wd_ref[...], preferred_element_type=_F32)

    def variant(cur, prev):
        @pl.when(f == 0)
        def _():
            tile = jnp.maximum(i - 1, 0)
            values = finish(prev, tile)
            cur[...] = chunk()
            store(values, tile)

        @pl.when(f > 0)
        def _():
            cur[...] += chunk()

        @pl.when((f == FFN_STEPS - 1) & (i == FFN_TILES - 1))
        def _():
            tile = jnp.int32(FFN_TILES - 1)
            store(finish(cur, tile), tile)

    @pl.when(i % 2 == 0)
    def _():
        variant(acc_a, acc_b)

    @pl.when(i % 2 == 1)
    def _():
        variant(acc_b, acc_a)


def _ffn_mid_kernel(hn_ref, x1_ref, wg_ref, wu_ref, wd_ref, mod_ref, modn_ref, gn_ref, x2_ref, hnn_ref,
                    acc_a, acc_b):
    def finish(acc_ref, tile):
        grp = _row_group(tile, FFN_TM)
        x2 = x1_ref[...] + _mod_row(mod_ref, 5, grp) * acc_ref[...]
        y = _rms(x2) * gn_ref[0:1, :]
        return x2, (y * (1.0 + _mod_row(modn_ref, 1, grp)) + _mod_row(modn_ref, 0, grp)).astype(_BF)

    def store(values, tile):
        x2_ref[...], hnn_ref[...] = values

    _ffn_pipeline(hn_ref, wg_ref, wu_ref, wd_ref, acc_a, acc_b, finish, store)


def _ffn_last_kernel(hn_ref, x1_ref, wg_ref, wu_ref, wd_ref, mod_ref, fg_ref, yp_ref, ys_ref, acc_a, acc_b):
    def finish(acc_ref, tile):
        grp = _row_group(tile, FFN_TM)
        x2 = x1_ref[...] + _mod_row(mod_ref, 5, grp) * acc_ref[...]
        return _rms(x2) * fg_ref[...]

    def store(y, tile):
        @pl.when(tile < FFN_CTX_TILES)
        def _():
            yp_ref[...] = y

        @pl.when(tile >= FFN_CTX_TILES)
        def _():
            ys_ref[...] = y

    _ffn_pipeline(hn_ref, wg_ref, wu_ref, wd_ref, acc_a, acc_b, finish, store)


def _ffn_specs(layer):
    residual = lambda i, f: (jnp.where(f == FFN_STEPS - 1, i, jnp.maximum(i - 1, 0)), 0)
    return [
        pl.BlockSpec((FFN_TM, D_MODEL), lambda i, f: (i, 0)),
        pl.BlockSpec((FFN_TM, D_MODEL), residual),
        pl.BlockSpec((D_MODEL, FFN_TF), lambda i, f: (0, f)),
        pl.BlockSpec((D_MODEL, FFN_TF), lambda i, f: (0, f)),
        pl.BlockSpec((FFN_TF, D_MODEL), lambda i, f: (f, 0)),
        pl.BlockSpec((None, N_MOD, 8, D_MODEL), lambda i, f: (layer, 0, 0, 0)),
    ]


def _ffn_mid(layer, hn2, x1, wg_b, wu_b, wd_b, mod, norm_g):
    row = lambda i, f: (_finished_tile(i, f), 0)
    return pl.pallas_call(
        _ffn_mid_kernel,
        grid=(FFN_TILES, FFN_STEPS),
        in_specs=_ffn_specs(layer) + [
            pl.BlockSpec((None, N_MOD, 8, D_MODEL), lambda i, f: (layer + 1, 0, 0, 0)),
            pl.BlockSpec((None, 2, D_MODEL), lambda i, f: (layer + 1, 0, 0)),
        ],
        out_specs=[pl.BlockSpec((FFN_TM, D_MODEL), row), pl.BlockSpec((FFN_TM, D_MODEL), row)],
        out_shape=[
            jax.ShapeDtypeStruct((ROWS, D_MODEL), _F32),
            jax.ShapeDtypeStruct((ROWS, D_MODEL), _BF),
        ],
        scratch_shapes=[pltpu.VMEM((FFN_TM, D_MODEL), _F32)] * 2,
        compiler_params=_cparams(("arbitrary", "arbitrary")),
        name=f"ffn{layer}",
    )(hn2, x1, wg_b, wu_b, wd_b, mod, mod, norm_g)


def _ffn_last(layer, hn2, x1, wg_b, wu_b, wd_b, mod, final_g):
    return pl.pallas_call(
        _ffn_last_kernel,
        grid=(FFN_TILES, FFN_STEPS),
        in_specs=_ffn_specs(layer) + [pl.BlockSpec((1, D_MODEL), lambda i, f: (0, 0))],
        out_specs=[
            pl.BlockSpec((FFN_TM, D_MODEL), lambda i, f: (jnp.minimum(_finished_tile(i, f), FFN_CTX_TILES - 1), 0)),
            pl.BlockSpec((FFN_TM, D_MODEL), lambda i, f: (jnp.maximum(_finished_tile(i, f) - FFN_CTX_TILES, 0), 0)),
        ],
        out_shape=[
            jax.ShapeDtypeStruct((CTX_ROWS, D_MODEL), _F32),
            jax.ShapeDtypeStruct((LAT_ROWS, D_MODEL), _F32),
        ],
        scratch_shapes=[pltpu.VMEM((FFN_TM, D_MODEL), _F32)] * 2,
        compiler_params=_cparams(("arbitrary", "arbitrary")),
        name=f"ffn{layer}",
    )(hn2, x1, wg_b, wu_b, wd_b, mod, final_g.reshape(1, D_MODEL))


def _pos_table():
    rows = DEC_SEQ // GRID_W
    r = np.repeat(np.arange(rows, dtype=np.float32), GRID_W)
    col = np.tile(np.arange(GRID_W, dtype=np.float32), rows)
    n_freq = D_MODEL // 4
    freqs = (1.0 / (np.float32(POS_BASE) ** (np.arange(n_freq, dtype=np.float32) / np.float32(n_freq)))).astype(np.float32)
    er = r[:, None] * freqs
    ec = col[:, None] * freqs
    return np.concatenate([np.sin(er), np.cos(er), np.sin(ec), np.cos(ec)], axis=-1).astype(np.float32)


def kernel(x_prompt, x_sample, state_rglru, c, c_ctx, w_mod, b_mod, norm_g, w_in, lru_conv_w, lru_conv_b,
           lru_w_a, lru_b_a, lru_w_i, lru_b_i, lru_lambda, sc_conv_w, w_out, w_gate, w_up, w_down, final_g):
    c8 = jnp.concatenate([c_ctx[None, :], c, jnp.zeros((8 - 1 - DEC_BATCH, D_MODEL), _F32)], axis=0)
    mod = _modulation(c8, w_mod, b_mod)

    pos = jnp.asarray(_pos_table())
    hn, x = _prenorm(x_prompt.reshape(CTX_ROWS, D_MODEL), x_sample.reshape(LAT_ROWS, D_MODEL), pos, mod, norm_g)

    cb = lru_conv_b.reshape(DEPTH, 1, LRU_WIDTH)
    b_a = lru_b_a.reshape(DEPTH, 2, LRU_WIDTH)
    b_i = lru_b_i.reshape(DEPTH, 2, LRU_WIDTH)
    h0 = jnp.pad(jnp.transpose(state_rglru, (1, 2, 0, 3)), ((0, 0), (0, 0), (0, 8 - DEC_BATCH), (0, 0)))

    states = []
    for layer in range(DEPTH):
        z, wg_b, wu_b, wd_b = _inproj(layer, hn, w_in, w_gate, w_up, w_down)
        lru, sc, st = _mixer(layer, z, lru_conv_w, cb, lru_w_a, lru_w_i, b_a, b_i, lru_lambda, sc_conv_w, h0[layer])
        states.append(st[:, :BATCH, :])
        x1, hn2 = _outproj(layer, lru, sc, x, w_out, mod, norm_g)
        if layer + 1 < DEPTH:
            x, hn = _ffn_mid(layer, hn2, x1, wg_b, wu_b, wd_b, mod, norm_g)
        else:
            y_p, y_s = _ffn_last(layer, hn2, x1, wg_b, wu_b, wd_b, mod, final_g)

    new_state = jnp.transpose(jnp.stack(states, axis=0), (2, 0, 1, 3))
    return (y_p.reshape(BATCH, SEQ, D_MODEL), y_s.reshape(DEC_BATCH, DEC_SEQ, D_MODEL), new_state)
```

```python
import math

import numpy as np
import jax
import jax.numpy as jnp
from jax import lax
from jax.experimental import pallas as pl
from jax.experimental.pallas import tpu as pltpu

D_MODEL = 2048
BATCH = 32
SEQ = 256
DEPTH = 2
DEC_BATCH = 2
DEC_SEQ = 2048
GRID_W = 64
LRU_WIDTH = 1024
LRU_HEADS = 8
HEAD_DIM = 128
LRU_CONV = 4
LRU_C = 8.0
CONV_WIDTH = 1024
SHORT_CONV = 3
D_IN = 2 * LRU_WIDTH + 3 * CONV_WIDTH
D_FF = 5632
N_MOD = 6
EPS = 1e-6
POS_BASE = 10000.0

CTX_ROWS = BATCH * SEQ
LAT_ROWS = DEC_BATCH * DEC_SEQ
ROWS = CTX_ROWS + LAT_ROWS
N_SLAB = D_IN // HEAD_DIM

GROUP_ROWS = 2048
CHUNK = 256
N_CHUNK = GROUP_ROWS // CHUNK
N_GROUPS = ROWS // GROUP_ROWS
N_CTX_GROUPS = CTX_ROWS // GROUP_ROWS
SUBLANES = 8
PITCH = CHUNK + SUBLANES
XPITCH = CHUNK + 2 * SUBLANES

V7X_VMEM_LIMIT = 56 * 1024 * 1024
F32_TINY = float(np.finfo(np.float32).tiny)
GELU_C = math.sqrt(2.0 / math.pi)
GELU_C3 = 0.044715 * GELU_C

_BF = jnp.bfloat16
_F32 = jnp.float32


def _cparams(sem):
    return pltpu.CompilerParams(dimension_semantics=sem, vmem_limit_bytes=V7X_VMEM_LIMIT)


def _sigmoid(x):
    return 0.5 * jnp.tanh(0.5 * x) + 0.5


def _row_group(tile, rows_per_tile):
    return jnp.maximum((tile * rows_per_tile) // DEC_SEQ - CTX_ROWS // DEC_SEQ + 1, 0)


def _mod_row(mod_ref, j, grp):
    return mod_ref[j, pl.ds(grp, 1), :]


def _rms(x):
    return x * lax.rsqrt(jnp.mean(x * x, axis=-1, keepdims=True) + EPS)


MOD_TN = 1024


def _mod_kernel(c_ref, w_ref, b_ref, o_ref):
    c = c_ref[...]
    s = (c * _sigmoid(c)).astype(_BF)
    o_ref[...] = jnp.dot(s, w_ref[...].astype(_BF), preferred_element_type=_F32) + b_ref[...]


def _modulation(c8, w_mod, b_mod):
    per = D_MODEL // MOD_TN
    b4 = b_mod.reshape(DEPTH, N_MOD, 1, D_MODEL)
    return pl.pallas_call(
        _mod_kernel,
        grid=(DEPTH, N_MOD * per),
        in_specs=[
            pl.BlockSpec((8, D_MODEL), lambda l, n: (0, 0)),
            pl.BlockSpec((None, D_MODEL, MOD_TN), lambda l, n: (l, 0, n)),
            pl.BlockSpec((None, None, 1, MOD_TN), lambda l, n: (l, n // per, 0, n % per)),
        ],
        out_specs=pl.BlockSpec((None, None, 8, MOD_TN), lambda l, n: (l, n // per, 0, n % per)),
        out_shape=jax.ShapeDtypeStruct((DEPTH, N_MOD, 8, D_MODEL), _F32),
        compiler_params=_cparams(("arbitrary", "arbitrary")),
        name="modulation",
    )(c8, w_mod, b4)


PRE_TM = 512


def _prenorm_kernel(xp_ref, xs_ref, pos_ref, mod_ref, g_ref, hn_ref, x0_ref):
    i = pl.program_id(0)
    is_lat = i >= CTX_ROWS // PRE_TM
    grp = _row_group(i, PRE_TM)
    x = jnp.where(is_lat, xs_ref[...] + pos_ref[...], xp_ref[...])
    x0_ref[...] = x
    y = _rms(x) * g_ref[0:1, :]
    hn = y * (1.0 + _mod_row(mod_ref, 1, grp)) + _mod_row(mod_ref, 0, grp)
    hn_ref[...] = hn.astype(_BF)


def _prenorm(xp, xs, pos, mod, norm_g):
    nct = CTX_ROWS // PRE_TM
    npos = DEC_SEQ // PRE_TM
    return pl.pallas_call(
        _prenorm_kernel,
        grid=(ROWS // PRE_TM,),
        in_specs=[
            pl.BlockSpec((PRE_TM, D_MODEL), lambda i: (jnp.minimum(i, nct - 1), 0)),
            pl.BlockSpec((PRE_TM, D_MODEL), lambda i: (jnp.maximum(i - nct, 0), 0)),
            pl.BlockSpec((PRE_TM, D_MODEL), lambda i: (jnp.maximum(i - nct, 0) % npos, 0)),
            pl.BlockSpec((None, N_MOD, 8, D_MODEL), lambda i: (0, 0, 0, 0)),
            pl.BlockSpec((None, 2, D_MODEL), lambda i: (0, 0, 0)),
        ],
        out_specs=[
            pl.BlockSpec((PRE_TM, D_MODEL), lambda i: (i, 0)),
            pl.BlockSpec((PRE_TM, D_MODEL), lambda i: (i, 0)),
        ],
        out_shape=[
            jax.ShapeDtypeStruct((ROWS, D_MODEL), _BF),
            jax.ShapeDtypeStruct((ROWS, D_MODEL), _F32),
        ],
        compiler_params=_cparams(("arbitrary",)),
        name="prenorm",
    )(xp, xs, pos, mod, norm_g)


IN_TM = 1024
IN_TN = 1024
IN_STEPS = (D_IN // IN_TN) * (ROWS // IN_TM)
CAST_STEPS = 32
CAST_CHUNK = 512
CAST_ROWS_IN = D_MODEL // CAST_STEPS
CAST_ROWS_FF = D_FF // CAST_STEPS


def _inproj_kernel(hn_ref, w_ref, wg_ref, wu_ref, wd_ref, z_ref, wgb_ref, wub_ref, wdb_ref, wb_scr):
    @pl.when(pl.program_id(1) == 0)
    def _():
        wb_scr[...] = w_ref[...].astype(_BF)

    step = pl.program_id(0) * (ROWS // IN_TM) + pl.program_id(1)

    @pl.when(step < CAST_STEPS)
    def _():
        for c in range(D_FF // CAST_CHUNK):
            cols = slice(c * CAST_CHUNK, (c + 1) * CAST_CHUNK)
            wgb_ref[c] = wg_ref[:, cols].astype(_BF)
            wub_ref[c] = wu_ref[:, cols].astype(_BF)

    @pl.when(step >= IN_STEPS - CAST_STEPS)
    def _():
        wdb_ref[...] = wd_ref[...].astype(_BF)

    acc = jnp.dot(hn_ref[...], wb_scr[...], preferred_element_type=_F32)
    for j in range(IN_TN // HEAD_DIM):
        z_ref[j] = acc[:, j * HEAD_DIM:(j + 1) * HEAD_DIM]


def _inproj(layer, hn, w_in, w_gate, w_up, w_down):
    assert CAST_STEPS <= IN_STEPS
    step = lambda n, i: n * (ROWS // IN_TM) + i
    cast_block = lambda n, i: (jnp.minimum(step(n, i), CAST_STEPS - 1), 0)
    late_block = lambda n, i: (jnp.maximum(step(n, i) - (IN_STEPS - CAST_STEPS), 0), 0)
    in_w = pl.BlockSpec((None, CAST_ROWS_IN, D_FF), lambda n, i: (layer,) + cast_block(n, i))
    ff_w = pl.BlockSpec((None, CAST_ROWS_FF, D_MODEL), lambda n, i: (layer,) + late_block(n, i))
    return pl.pallas_call(
        _inproj_kernel,
        grid=(D_IN // IN_TN, ROWS // IN_TM),
        in_specs=[
            pl.BlockSpec((IN_TM, D_MODEL), lambda n, i: (i, 0)),
            pl.BlockSpec((None, D_MODEL, IN_TN), lambda n, i: (layer, 0, n)),
            in_w, in_w, ff_w,
        ],
        out_specs=[
            pl.BlockSpec((IN_TN // HEAD_DIM, IN_TM, HEAD_DIM), lambda n, i: (n, i, 0)),
            pl.BlockSpec((D_FF // CAST_CHUNK, CAST_ROWS_IN, CAST_CHUNK), lambda n, i: (0,) + cast_block(n, i)),
            pl.BlockSpec((D_FF // CAST_CHUNK, CAST_ROWS_IN, CAST_CHUNK), lambda n, i: (0,) + cast_block(n, i)),
            pl.BlockSpec((CAST_ROWS_FF, D_MODEL), late_block),
        ],
        out_shape=[
            jax.ShapeDtypeStruct((N_SLAB, ROWS, HEAD_DIM), _F32),
            jax.ShapeDtypeStruct((D_FF // CAST_CHUNK, D_MODEL, CAST_CHUNK), _BF),
            jax.ShapeDtypeStruct((D_FF // CAST_CHUNK, D_MODEL, CAST_CHUNK), _BF),
            jax.ShapeDtypeStruct((D_FF, D_MODEL), _BF),
        ],
        scratch_shapes=[pltpu.VMEM((D_MODEL, IN_TN), _BF)],
        compiler_params=_cparams(("arbitrary", "arbitrary")),
        name=f"inproj{layer}",
    )(hn, w_in, w_gate, w_up, w_down)


def _mixer_kernel(xl_ref, yg_ref, bg_ref, cg_ref, v_ref, cw_ref, cb_ref, wa_ref, wi_ref, ba_ref, bi_ref,
                  lam_ref, scw_ref, h0_ref, lru_ref, sc_ref, st_ref,
                  a_scr, u_scr, h_scr, pad_scr, pe_scr, hc_scr):
    g = pl.program_id(0)
    is_lat = g >= N_CTX_GROUPS

    def fill_padded(rows):
        zeros = jnp.zeros((SUBLANES, HEAD_DIM), _F32)
        for s in range(N_CHUNK):
            base = s * XPITCH
            lo = s * CHUNK
            front = jnp.where(is_lat, rows(lo - SUBLANES, SUBLANES), 0.0) if s > 0 else zeros
            back = jnp.where(is_lat, rows(lo + CHUNK, SUBLANES), 0.0) if s + 1 < N_CHUNK else zeros
            pad_scr[base:base + SUBLANES, :] = front
            pad_scr[base + SUBLANES:base + SUBLANES + CHUNK, :] = rows(lo, CHUNK)
            pad_scr[base + SUBLANES + CHUNK:base + XPITCH, :] = back

    def tap(s, k):
        start = s * XPITCH + SUBLANES + k
        return pad_scr[start:start + CHUNK, :]

    fill_padded(lambda lo, n: xl_ref[0, lo:lo + n, :])
    wcat = (0.5 * jnp.concatenate([wa_ref[0, 0], wi_ref[0, 0], wa_ref[1, 0], wi_ref[1, 0]], axis=1)).astype(_BF)
    half_ba = 0.5 * ba_ref[...]
    half_bi = 0.5 * bi_ref[...]
    nlam = -lam_ref[...]
    softplus = jnp.maximum(nlam, 0.0) + jnp.log1p(jnp.exp(-jnp.abs(nlam)))
    half_rate = (-0.5 * LRU_C) * softplus

    for s in range(N_CHUNK):
        xc = (cw_ref[2:3, :] * xl_ref[0, s * CHUNK:(s + 1) * CHUNK, :] + cb_ref[...]
              + cw_ref[0:1, :] * tap(s, -2) + cw_ref[1:2, :] * tap(s, -1) + cw_ref[3:4, :] * tap(s, 1))
        gates = jnp.dot(xc.astype(_BF), wcat, preferred_element_type=_F32)
        for d in range(2):
            half_ra = gates[:, d * 256:d * 256 + HEAD_DIM] + half_ba[d:d + 1, :]
            half_ia = gates[:, d * 256 + HEAD_DIM:(d + 1) * 256] + half_bi[d:d + 1, :]
            log_a = half_rate[d:d + 1, :] * (1.0 + jnp.tanh(half_ra))
            a = jnp.exp(log_a)
            th = jnp.tanh(log_a)
            q = (-0.5 * th) / (1.0 - th)
            root = jnp.maximum(q, 0.0) * lax.rsqrt(jnp.maximum(q, F32_TINY))
            a_scr[d, s * PITCH:s * PITCH + CHUNK, :] = a
            u_scr[d, s * PITCH:s * PITCH + CHUNK, :] = root * ((1.0 + jnp.tanh(half_ia)) * xc)

    def strided(t):
        return pl.ds(t, N_CHUNK, stride=PITCH)

    @pl.when(jnp.logical_not(is_lat))
    def _():
        hc_scr[...] = jnp.zeros_like(hc_scr)

    @pl.when(is_lat)
    def _():
        def body(t, c):
            pf, ef, pb, eb = c
            af = a_scr[0, strided(t), :]
            ef = af * ef + u_scr[0, strided(t), :]
            pf = pf * af
            tb = CHUNK - 1 - t
            ab = a_scr[1, strided(tb), :]
            eb = ab * eb + u_scr[1, strided(tb), :]
            pb = pb * ab
            return pf, ef, pb, eb

        one = jnp.ones((N_CHUNK, HEAD_DIM), _F32)
        zero = jnp.zeros((N_CHUNK, HEAD_DIM), _F32)
        pf, ef, pb, eb = lax.fori_loop(0, CHUNK, body, (one, zero, one, zero), unroll=8)
        pe_scr[0] = pf
        pe_scr[1] = ef
        pe_scr[2] = pb
        pe_scr[3] = eb
        seq = g - N_CTX_GROUPS
        h = h0_ref[0, pl.ds(seq, 1), :]
        for c in range(N_CHUNK):
            hc_scr[0, c:c + 1, :] = h
            h = pe_scr[0, c:c + 1, :] * h + pe_scr[1, c:c + 1, :]
        h = h0_ref[1, pl.ds(seq, 1), :]
        for c in reversed(range(N_CHUNK)):
            hc_scr[1, c:c + 1, :] = h
            h = pe_scr[2, c:c + 1, :] * h + pe_scr[3, c:c + 1, :]

    def scan_body(t, c):
        hf, hb = c
        hf = a_scr[0, strided(t), :] * hf + u_scr[0, strided(t), :]
        h_scr[0, strided(t), :] = hf
        tb = CHUNK - 1 - t
        hb = a_scr[1, strided(tb), :] * hb + u_scr[1, strided(tb), :]
        h_scr[1, strided(tb), :] = hb
        return hf, hb

    hf, hb = lax.fori_loop(0, CHUNK, scan_body, (hc_scr[0], hc_scr[1]), unroll=8)
    st_ref[0] = hf
    st_ref[1] = hb

    for s in range(N_CHUNK):
        y = yg_ref[0, s * CHUNK:(s + 1) * CHUNK, :]
        gelu = (0.5 * y) * (1.0 + jnp.tanh(y * (GELU_C + GELU_C3 * (y * y))))
        o = h_scr[0, s * PITCH:s * PITCH + CHUNK, :] + h_scr[1, s * PITCH:s * PITCH + CHUNK, :]
        lru_ref[s * CHUNK:(s + 1) * CHUNK, :] = (o * gelu).astype(_BF)

    fill_padded(lambda lo, n: cg_ref[0, lo:lo + n, :] * v_ref[0, lo:lo + n, :])
    for s in range(N_CHUNK):
        conv = scw_ref[1:2, :] * tap(s, 0) + scw_ref[0:1, :] * tap(s, -1) + scw_ref[2:3, :] * tap(s, 1)
        sc_ref[s * CHUNK:(s + 1) * CHUNK, :] = (bg_ref[0, s * CHUNK:(s + 1) * CHUNK, :] * conv).astype(_BF)


def _mixer(layer, z, cw, cb, w_a, w_i, b_a, b_i, lam, scw, h0):
    def zspec(off):
        return pl.BlockSpec((1, GROUP_ROWS, HEAD_DIM), lambda g, h: (off + h, g, 0))

    def vec(rows):
        return pl.BlockSpec((None, rows, HEAD_DIM), lambda g, h: (layer, 0, h))

    wspec = pl.BlockSpec((None, 2, 1, HEAD_DIM, HEAD_DIM), lambda g, h: (layer, 0, h, 0, 0))
    scan_buf = pltpu.VMEM((2, N_CHUNK * PITCH, HEAD_DIM), _F32)
    return pl.pallas_call(
        _mixer_kernel,
        grid=(N_GROUPS, LRU_HEADS),
        in_specs=[
            zspec(0), zspec(8), zspec(16), zspec(24), zspec(32),
            vec(LRU_CONV), vec(1), wspec, wspec, vec(2), vec(2), vec(2), vec(SHORT_CONV),
            pl.BlockSpec((2, 8, HEAD_DIM), lambda g, h: (0, 0, h)),
        ],
        out_specs=[
            pl.BlockSpec((GROUP_ROWS, HEAD_DIM), lambda g, h: (g, h)),
            pl.BlockSpec((GROUP_ROWS, HEAD_DIM), lambda g, h: (g, h)),
            pl.BlockSpec((2, N_CHUNK, HEAD_DIM), lambda g, h: (0, g, h)),
        ],
        out_shape=[
            jax.ShapeDtypeStruct((ROWS, LRU_WIDTH), _BF),
            jax.ShapeDtypeStruct((ROWS, CONV_WIDTH), _BF),
            jax.ShapeDtypeStruct((2, N_GROUPS * N_CHUNK, LRU_WIDTH), _F32),
        ],
        scratch_shapes=[
            scan_buf, scan_buf, scan_buf,
            pltpu.VMEM((N_CHUNK * XPITCH, HEAD_DIM), _F32),
            pltpu.VMEM((4, N_CHUNK, HEAD_DIM), _F32),
            pltpu.VMEM((2, N_CHUNK, HEAD_DIM), _F32),
        ],
        compiler_params=_cparams(("arbitrary", "arbitrary")),
        name=f"mixer{layer}",
    )(z, z, z, z, z, cw, cb, w_a, w_i, b_a, b_i, lam, scw, h0)


OUT_TM = 512


def _outproj_kernel(lru_ref, sc_ref, x_ref, w_ref, mod_ref, g_ref, x1_ref, hn2_ref, wb_scr):
    @pl.when(pl.program_id(0) == 0)
    def _():
        wb_scr[...] = w_ref[...].astype(_BF)

    grp = _row_group(pl.program_id(0), OUT_TM)
    gate = _mod_row(mod_ref, 2, grp)
    scale = g_ref[1:2, :] * (1.0 + _mod_row(mod_ref, 4, grp))
    shift = _mod_row(mod_ref, 3, grp)
    for r0 in range(0, OUT_TM, OUT_TM // 2):
        rows = slice(r0, r0 + OUT_TM // 2)
        m = jnp.dot(lru_ref[rows, :], wb_scr[0:LRU_WIDTH, :], preferred_element_type=_F32)
        m = m + jnp.dot(sc_ref[rows, :], wb_scr[LRU_WIDTH:, :], preferred_element_type=_F32)
        x1 = x_ref[rows, :] + gate * m
        x1_ref[rows, :] = x1
        hn2_ref[rows, :] = (_rms(x1) * scale + shift).astype(_BF)


def _outproj(layer, lru, sc, x, w_out, mod, norm_g):
    row = lambda i: (i, 0)
    return pl.pallas_call(
        _outproj_kernel,
        grid=(ROWS // OUT_TM,),
        in_specs=[
            pl.BlockSpec((OUT_TM, LRU_WIDTH), row),
            pl.BlockSpec((OUT_TM, CONV_WIDTH), row),
            pl.BlockSpec((OUT_TM, D_MODEL), row),
            pl.BlockSpec((None, D_MODEL, D_MODEL), lambda i: (layer, 0, 0), pipeline_mode=pl.Buffered(1)),
            pl.BlockSpec((None, N_MOD, 8, D_MODEL), lambda i: (layer, 0, 0, 0)),
            pl.BlockSpec((None, 2, D_MODEL), lambda i: (layer, 0, 0)),
        ],
        out_specs=[pl.BlockSpec((OUT_TM, D_MODEL), row), pl.BlockSpec((OUT_TM, D_MODEL), row)],
        out_shape=[
            jax.ShapeDtypeStruct((ROWS, D_MODEL), _F32),
            jax.ShapeDtypeStruct((ROWS, D_MODEL), _BF),
        ],
        scratch_shapes=[pltpu.VMEM((D_MODEL, D_MODEL), _BF)],
        compiler_params=_cparams(("arbitrary",)),
        name=f"outproj{layer}",
    )(lru, sc, x, w_out, mod, norm_g)


FFN_TM = 512
FFN_TF = CAST_CHUNK
FFN_STEPS = D_FF // FFN_TF
FFN_TILES = ROWS // FFN_TM
FFN_CTX_TILES = CTX_ROWS // FFN_TM


def _finished_tile(i, f):
    final = (i == FFN_TILES - 1) & (f == FFN_STEPS - 1)
    return jnp.where(final, FFN_TILES - 1, jnp.maximum(i - 1, 0))


def _ffn_pipeline(hn_ref, wg_ref, wu_ref, wd_ref, acc_a, acc_b, finish, store):
    i = pl.program_id(0)
    f = pl.program_id(1)

    @pl.when((i == 0) & (f == 0))
    def _():
        acc_b[...] = jnp.zeros_like(acc_b)

    def chunk():
        h = hn_ref[...]
        gate = jnp.dot(h, wg_ref[...], preferred_element_type=_F32)
        up = jnp.dot(h, wu_ref[...], preferred_element_type=_F32)
        act = (gate * _sigmoid(gate) * up).astype(_BF)
        return jnp.dot(act, wd_ref[...], preferred_element_type=_F32)

    def variant(cur, prev):
        @pl.when(f == 0)
        def _():
            tile = jnp.maximum(i - 1, 0)
            values = finish(prev, tile)
            cur[...] = chunk()
            store(values, tile)

        @pl.when(f > 0)
        def _():
            cur[...] += chunk()

        @pl.when((f == FFN_STEPS - 1) & (i == FFN_TILES - 1))
        def _():
            tile = jnp.int32(FFN_TILES - 1)
            store(finish(cur, tile), tile)

    @pl.when(i % 2 == 0)
    def _():
        variant(acc_a, acc_b)

    @pl.when(i % 2 == 1)
    def _():
        variant(acc_b, acc_a)


def _ffn_mid_kernel(hn_ref, x1_ref, wg_ref, wu_ref, wd_ref, mod_ref, modn_ref, gn_ref, x2_ref, hnn_ref,
                    acc_a, acc_b):
    def finish(acc_ref, tile):
        grp = _row_group(tile, FFN_TM)
        x2 = x1_ref[...] + _mod_row(mod_ref, 5, grp) * acc_ref[...]
        y = _rms(x2) * gn_ref[0:1, :]
        return x2, (y * (1.0 + _mod_row(modn_ref, 1, grp)) + _mod_row(modn_ref, 0, grp)).astype(_BF)

    def store(values, tile):
        x2_ref[...], hnn_ref[...] = values

    _ffn_pipeline(hn_ref, wg_ref, wu_ref, wd_ref, acc_a, acc_b, finish, store)


def _ffn_last_kernel(hn_ref, x1_ref, wg_ref, wu_ref, wd_ref, mod_ref, fg_ref, yp_ref, ys_ref, acc_a, acc_b):
    def finish(acc_ref, tile):
        grp = _row_group(tile, FFN_TM)
        x2 = x1_ref[...] + _mod_row(mod_ref, 5, grp) * acc_ref[...]
        return _rms(x2) * fg_ref[...]

    def store(y, tile):
        @pl.when(tile < FFN_CTX_TILES)
        def _():
            yp_ref[...] = y

        @pl.when(tile >= FFN_CTX_TILES)
        def _():
            ys_ref[...] = y

    _ffn_pipeline(hn_ref, wg_ref, wu_ref, wd_ref, acc_a, acc_b, finish, store)


def _ffn_specs(layer):
    residual = lambda i, f: (jnp.where(f == FFN_STEPS - 1, i, jnp.maximum(i - 1, 0)), 0)
    return [
        pl.BlockSpec((FFN_TM, D_MODEL), lambda i, f: (i, 0)),
        pl.BlockSpec((FFN_TM, D_MODEL), residual),
        pl.BlockSpec((None, D_MODEL, FFN_TF), lambda i, f: (f, 0, 0)),
        pl.BlockSpec((None, D_MODEL, FFN_TF), lambda i, f: (f, 0, 0)),
        pl.BlockSpec((FFN_TF, D_MODEL), lambda i, f: (f, 0)),
        pl.BlockSpec((None, N_MOD, 8, D_MODEL), lambda i, f: (layer, 0, 0, 0)),
    ]


def _ffn_mid(layer, hn2, x1, wg_b, wu_b, wd_b, mod, norm_g):
    row = lambda i, f: (_finished_tile(i, f), 0)
    return pl.pallas_call(
        _ffn_mid_kernel,
        grid=(FFN_TILES, FFN_STEPS),
        in_specs=_ffn_specs(layer) + [
            pl.BlockSpec((None, N_MOD, 8, D_MODEL), lambda i, f: (layer + 1, 0, 0, 0)),
            pl.BlockSpec((None, 2, D_MODEL), lambda i, f: (layer + 1, 0, 0)),
        ],
        out_specs=[pl.BlockSpec((FFN_TM, D_MODEL), row), pl.BlockSpec((FFN_TM, D_MODEL), row)],
        out_shape=[
            jax.ShapeDtypeStruct((ROWS, D_MODEL), _F32),
            jax.ShapeDtypeStruct((ROWS, D_MODEL), _BF),
        ],
        scratch_shapes=[pltpu.VMEM((FFN_TM, D_MODEL), _F32)] * 2,
        compiler_params=_cparams(("arbitrary", "arbitrary")),
        name=f"ffn{layer}",
    )(hn2, x1, wg_b, wu_b, wd_b, mod, mod, norm_g)


def _ffn_last(layer, hn2, x1, wg_b, wu_b, wd_b, mod, final_g):
    return pl.pallas_call(
        _ffn_last_kernel,
        grid=(FFN_TILES, FFN_STEPS),
        in_specs=_ffn_specs(layer) + [pl.BlockSpec((1, D_MODEL), lambda i, f: (0, 0))],
        out_specs=[
            pl.BlockSpec((FFN_TM, D_MODEL), lambda i, f: (jnp.minimum(_finished_tile(i, f), FFN_CTX_TILES - 1), 0)),
            pl.BlockSpec((FFN_TM, D_MODEL), lambda i, f: (jnp.maximum(_finished_tile(i, f) - FFN_CTX_TILES, 0), 0)),
        ],
        out_shape=[
            jax.ShapeDtypeStruct((CTX_ROWS, D_MODEL), _F32),
            jax.ShapeDtypeStruct((LAT_ROWS, D_MODEL), _F32),
        ],
        scratch_shapes=[pltpu.VMEM((FFN_TM, D_MODEL), _F32)] * 2,
        compiler_params=_cparams(("arbitrary", "arbitrary")),
        name=f"ffn{layer}",
    )(hn2, x1, wg_b, wu_b, wd_b, mod, final_g.reshape(1, D_MODEL))


def _pos_table():
    rows = DEC_SEQ // GRID_W
    r = np.repeat(np.arange(rows, dtype=np.float32), GRID_W)
    col = np.tile(np.arange(GRID_W, dtype=np.float32), rows)
    n_freq = D_MODEL // 4
    freqs = (1.0 / (np.float32(POS_BASE) ** (np.arange(n_freq, dtype=np.float32) / np.float32(n_freq)))).astype(np.float32)
    er = r[:, None] * freqs
    ec = col[:, None] * freqs
    return np.concatenate([np.sin(er), np.cos(er), np.sin(ec), np.cos(ec)], axis=-1).astype(np.float32)


def kernel(x_prompt, x_sample, state_rglru, c, c_ctx, w_mod, b_mod, norm_g, w_in, lru_conv_w, lru_conv_b,
           lru_w_a, lru_b_a, lru_w_i, lru_b_i, lru_lambda, sc_conv_w, w_out, w_gate, w_up, w_down, final_g):
    c8 = jnp.concatenate([c_ctx[None, :], c, jnp.zeros((8 - 1 - DEC_BATCH, D_MODEL), _F32)], axis=0)
    mod = _modulation(c8, w_mod, b_mod)

    pos = jnp.asarray(_pos_table())
    hn, x = _prenorm(x_prompt.reshape(CTX_ROWS, D_MODEL), x_sample.reshape(LAT_ROWS, D_MODEL), pos, mod, norm_g)

    cb = lru_conv_b.reshape(DEPTH, 1, LRU_WIDTH)
    b_a = lru_b_a.reshape(DEPTH, 2, LRU_WIDTH)
    b_i = lru_b_i.reshape(DEPTH, 2, LRU_WIDTH)
    h0 = jnp.pad(jnp.transpose(state_rglru, (1, 2, 0, 3)), ((0, 0), (0, 0), (0, 8 - DEC_BATCH), (0, 0)))

    states = []
    for layer in range(DEPTH):
        z, wg_b, wu_b, wd_b = _inproj(layer, hn, w_in, w_gate, w_up, w_down)
        lru, sc, st = _mixer(layer, z, lru_conv_w, cb, lru_w_a, lru_w_i, b_a, b_i, lru_lambda, sc_conv_w, h0[layer])
        states.append(st[:, :BATCH, :])
        x1, hn2 = _outproj(layer, lru, sc, x, w_out, mod, norm_g)
        if layer + 1 < DEPTH:
            x, hn = _ffn_mid(layer, hn2, x1, wg_b, wu_b, wd_b, mod, norm_g)
        else:
            y_p, y_s = _ffn_last(layer, hn2, x1, wg_b, wu_b, wd_b, mod, final_g)

    new_state = jnp.transpose(jnp.stack(states, axis=0), (2, 0, 1, 3))
    return (y_p.reshape(BATCH, SEQ, D_MODEL), y_s.reshape(DEC_BATCH, DEC_SEQ, D_MODEL), new_state)
```

```python
import math

import numpy as np
import jax
import jax.numpy as jnp
from jax import lax
from jax.experimental import pallas as pl
from jax.experimental.pallas import tpu as pltpu

D_MODEL = 2048
BATCH = 32
SEQ = 256
DEPTH = 2
DEC_BATCH = 2
DEC_SEQ = 2048
GRID_W = 64
LRU_WIDTH = 1024
LRU_HEADS = 8
HEAD_DIM = 128
LRU_CONV = 4
LRU_C = 8.0
CONV_WIDTH = 1024
SHORT_CONV = 3
D_IN = 2 * LRU_WIDTH + 3 * CONV_WIDTH
D_FF = 5632
N_MOD = 6
EPS = 1e-6
POS_BASE = 10000.0

CTX_ROWS = BATCH * SEQ
LAT_ROWS = DEC_BATCH * DEC_SEQ
ROWS = CTX_ROWS + LAT_ROWS
N_SLAB = D_IN // HEAD_DIM

GROUP_ROWS = 2048
CHUNK = 256
N_CHUNK = GROUP_ROWS // CHUNK
N_GROUPS = ROWS // GROUP_ROWS
N_CTX_GROUPS = CTX_ROWS // GROUP_ROWS
SUBLANES = 8
PITCH = CHUNK + SUBLANES
XPITCH = CHUNK + 2 * SUBLANES

V7X_VMEM_LIMIT = 56 * 1024 * 1024
F32_TINY = float(np.finfo(np.float32).tiny)
GELU_C = math.sqrt(2.0 / math.pi)
GELU_C3 = 0.044715 * GELU_C

_BF = jnp.bfloat16
_F32 = jnp.float32


def _cparams(sem):
    return pltpu.CompilerParams(dimension_semantics=sem, vmem_limit_bytes=V7X_VMEM_LIMIT)


def _sigmoid(x):
    return 0.5 * jnp.tanh(0.5 * x) + 0.5


def _row_group(tile, rows_per_tile):
    return jnp.maximum((tile * rows_per_tile) // DEC_SEQ - CTX_ROWS // DEC_SEQ + 1, 0)


def _mod_row(mod_ref, j, grp):
    return mod_ref[j, pl.ds(grp, 1), :]


def _rms(x):
    return x * lax.rsqrt(jnp.mean(x * x, axis=-1, keepdims=True) + EPS)


MOD_TN = 1024


def _mod_kernel(c_ref, w_ref, b_ref, o_ref):
    c = c_ref[...]
    s = (c * _sigmoid(c)).astype(_BF)
    o_ref[...] = jnp.dot(s, w_ref[...].astype(_BF), preferred_element_type=_F32) + b_ref[...]


def _modulation(c8, w_mod, b_mod):
    per = D_MODEL // MOD_TN
    b4 = b_mod.reshape(DEPTH, N_MOD, 1, D_MODEL)
    return pl.pallas_call(
        _mod_kernel,
        grid=(DEPTH, N_MOD * per),
        in_specs=[
            pl.BlockSpec((8, D_MODEL), lambda l, n: (0, 0)),
            pl.BlockSpec((None, D_MODEL, MOD_TN), lambda l, n: (l, 0, n)),
            pl.BlockSpec((None, None, 1, MOD_TN), lambda l, n: (l, n // per, 0, n % per)),
        ],
        out_specs=pl.BlockSpec((None, None, 8, MOD_TN), lambda l, n: (l, n // per, 0, n % per)),
        out_shape=jax.ShapeDtypeStruct((DEPTH, N_MOD, 8, D_MODEL), _F32),
        compiler_params=_cparams(("arbitrary", "arbitrary")),
        name="modulation",
    )(c8, w_mod, b4)


PRE_TM = 512


def _prenorm_kernel(xp_ref, xs_ref, pos_ref, mod_ref, g_ref, hn_ref, x0_ref):
    i = pl.program_id(0)
    is_lat = i >= CTX_ROWS // PRE_TM
    grp = _row_group(i, PRE_TM)
    x = jnp.where(is_lat, xs_ref[...] + pos_ref[...], xp_ref[...])
    x0_ref[...] = x
    y = _rms(x) * g_ref[0:1, :]
    hn = y * (1.0 + _mod_row(mod_ref, 1, grp)) + _mod_row(mod_ref, 0, grp)
    hn_ref[...] = hn.astype(_BF)


def _prenorm(xp, xs, pos, mod, norm_g):
    nct = CTX_ROWS // PRE_TM
    npos = DEC_SEQ // PRE_TM
    return pl.pallas_call(
        _prenorm_kernel,
        grid=(ROWS // PRE_TM,),
        in_specs=[
            pl.BlockSpec((PRE_TM, D_MODEL), lambda i: (jnp.minimum(i, nct - 1), 0)),
            pl.BlockSpec((PRE_TM, D_MODEL), lambda i: (jnp.maximum(i - nct, 0), 0)),
            pl.BlockSpec((PRE_TM, D_MODEL), lambda i: (jnp.maximum(i - nct, 0) % npos, 0)),
            pl.BlockSpec((None, N_MOD, 8, D_MODEL), lambda i: (0, 0, 0, 0)),
            pl.BlockSpec((None, 2, D_MODEL), lambda i: (0, 0, 0)),
        ],
        out_specs=[
            pl.BlockSpec((PRE_TM, D_MODEL), lambda i: (i, 0)),
            pl.BlockSpec((PRE_TM, D_MODEL), lambda i: (i, 0)),
        ],
        out_shape=[
            jax.ShapeDtypeStruct((ROWS, D_MODEL), _BF),
            jax.ShapeDtypeStruct((ROWS, D_MODEL), _F32),
        ],
        compiler_params=_cparams(("arbitrary",)),
        name="prenorm",
    )(xp, xs, pos, mod, norm_g)


IN_TM = 1024
IN_TN = 1024
IN_STEPS = (D_IN // IN_TN) * (ROWS // IN_TM)
CAST_STEPS = 32
CAST_CHUNK = 512
CAST_ROWS_IN = D_MODEL // CAST_STEPS
CAST_ROWS_FF = D_FF // CAST_STEPS


def _cast_hidden_in(w_ref, wb_ref):
    for c in range(D_FF // CAST_CHUNK):
        wb_ref[c] = w_ref[:, c * CAST_CHUNK:(c + 1) * CAST_CHUNK].astype(_BF)


def _project(hn_ref, w_ref, z_ref, wb_scr):
    @pl.when(pl.program_id(1) == 0)
    def _():
        wb_scr[...] = w_ref[...].astype(_BF)

    acc = jnp.dot(hn_ref[...], wb_scr[...], preferred_element_type=_F32)
    for j in range(IN_TN // HEAD_DIM):
        z_ref[j] = acc[:, j * HEAD_DIM:(j + 1) * HEAD_DIM]


def _inproj_kernel(hn_ref, w_ref, z_ref, wb_scr):
    _project(hn_ref, w_ref, z_ref, wb_scr)


def _inproj_cast_kernel(hn_ref, w_ref, wg_ref, wu_ref, wd_ref, z_ref, wgb_ref, wub_ref, wdb_ref, wb_scr):
    step = pl.program_id(0) * (ROWS // IN_TM) + pl.program_id(1)

    @pl.when(step < CAST_STEPS)
    def _():
        _cast_hidden_in(wg_ref, wgb_ref)
        _cast_hidden_in(wu_ref, wub_ref)

    @pl.when(step >= IN_STEPS - CAST_STEPS)
    def _():
        wdb_ref[...] = wd_ref[...].astype(_BF)

    _project(hn_ref, w_ref, z_ref, wb_scr)


def _inproj(layer, hn, w_in):
    return pl.pallas_call(
        _inproj_kernel,
        grid=(D_IN // IN_TN, ROWS // IN_TM),
        in_specs=[
            pl.BlockSpec((IN_TM, D_MODEL), lambda n, i: (i, 0)),
            pl.BlockSpec((None, D_MODEL, IN_TN), lambda n, i: (layer, 0, n)),
        ],
        out_specs=pl.BlockSpec((IN_TN // HEAD_DIM, IN_TM, HEAD_DIM), lambda n, i: (n, i, 0)),
        out_shape=jax.ShapeDtypeStruct((N_SLAB, ROWS, HEAD_DIM), _F32),
        scratch_shapes=[pltpu.VMEM((D_MODEL, IN_TN), _BF)],
        compiler_params=_cparams(("arbitrary", "arbitrary")),
        name=f"inproj{layer}",
    )(hn, w_in)


def _inproj_cast(layer, hn, w_in, w_gate, w_up, w_down):
    assert CAST_STEPS <= IN_STEPS
    step = lambda n, i: n * (ROWS // IN_TM) + i
    cast_block = lambda n, i: (jnp.minimum(step(n, i), CAST_STEPS - 1), 0)
    late_block = lambda n, i: (jnp.maximum(step(n, i) - (IN_STEPS - CAST_STEPS), 0), 0)
    in_w = pl.BlockSpec((None, CAST_ROWS_IN, D_FF), lambda n, i: (layer,) + cast_block(n, i))
    ff_w = pl.BlockSpec((None, CAST_ROWS_FF, D_MODEL), lambda n, i: (layer,) + late_block(n, i))
    return pl.pallas_call(
        _inproj_cast_kernel,
        grid=(D_IN // IN_TN, ROWS // IN_TM),
        in_specs=[
            pl.BlockSpec((IN_TM, D_MODEL), lambda n, i: (i, 0)),
            pl.BlockSpec((None, D_MODEL, IN_TN), lambda n, i: (layer, 0, n)),
            in_w, in_w, ff_w,
        ],
        out_specs=[
            pl.BlockSpec((IN_TN // HEAD_DIM, IN_TM, HEAD_DIM), lambda n, i: (n, i, 0)),
            pl.BlockSpec((D_FF // CAST_CHUNK, CAST_ROWS_IN, CAST_CHUNK), lambda n, i: (0,) + cast_block(n, i)),
            pl.BlockSpec((D_FF // CAST_CHUNK, CAST_ROWS_IN, CAST_CHUNK), lambda n, i: (0,) + cast_block(n, i)),
            pl.BlockSpec((CAST_ROWS_FF, D_MODEL), late_block),
        ],
        out_shape=[
            jax.ShapeDtypeStruct((N_SLAB, ROWS, HEAD_DIM), _F32),
            jax.ShapeDtypeStruct((D_FF // CAST_CHUNK, D_MODEL, CAST_CHUNK), _BF),
            jax.ShapeDtypeStruct((D_FF // CAST_CHUNK, D_MODEL, CAST_CHUNK), _BF),
            jax.ShapeDtypeStruct((D_FF, D_MODEL), _BF),
        ],
        scratch_shapes=[pltpu.VMEM((D_MODEL, IN_TN), _BF)],
        compiler_params=_cparams(("arbitrary", "arbitrary")),
        name=f"inproj{layer}",
    )(hn, w_in, w_gate, w_up, w_down)


def _mixer_kernel(xl_ref, yg_ref, bg_ref, cg_ref, v_ref, cw_ref, cb_ref, wa_ref, wi_ref, ba_ref, bi_ref,
                  lam_ref, scw_ref, h0_ref, lru_ref, sc_ref, st_ref,
                  a_scr, u_scr, h_scr, pad_scr, pe_scr, hc_scr):
    g = pl.program_id(0)
    is_lat = g >= N_CTX_GROUPS

    def fill_padded(rows):
        zeros = jnp.zeros((SUBLANES, HEAD_DIM), _F32)
        for s in range(N_CHUNK):
            base = s * XPITCH
            lo = s * CHUNK
            front = jnp.where(is_lat, rows(lo - SUBLANES, SUBLANES), 0.0) if s > 0 else zeros
            back = jnp.where(is_lat, rows(lo + CHUNK, SUBLANES), 0.0) if s + 1 < N_CHUNK else zeros
            pad_scr[base:base + SUBLANES, :] = front
            pad_scr[base + SUBLANES:base + SUBLANES + CHUNK, :] = rows(lo, CHUNK)
            pad_scr[base + SUBLANES + CHUNK:base + XPITCH, :] = back

    def tap(s, k):
        start = s * XPITCH + SUBLANES + k
        return pad_scr[start:start + CHUNK, :]

    fill_padded(lambda lo, n: xl_ref[0, lo:lo + n, :])
    wcat = (0.5 * jnp.concatenate([wa_ref[0, 0], wi_ref[0, 0], wa_ref[1, 0], wi_ref[1, 0]], axis=1)).astype(_BF)
    half_ba = 0.5 * ba_ref[...]
    half_bi = 0.5 * bi_ref[...]
    nlam = -lam_ref[...]
    softplus = jnp.maximum(nlam, 0.0) + jnp.log1p(jnp.exp(-jnp.abs(nlam)))
    half_rate = (-0.5 * LRU_C) * softplus

    for s in range(N_CHUNK):
        xc = (cw_ref[2:3, :] * xl_ref[0, s * CHUNK:(s + 1) * CHUNK, :] + cb_ref[...]
              + cw_ref[0:1, :] * tap(s, -2) + cw_ref[1:2, :] * tap(s, -1) + cw_ref[3:4, :] * tap(s, 1))
        gates = jnp.dot(xc.astype(_BF), wcat, preferred_element_type=_F32)
        for d in range(2):
            half_ra = gates[:, d * 256:d * 256 + HEAD_DIM] + half_ba[d:d + 1, :]
            half_ia = gates[:, d * 256 + HEAD_DIM:(d + 1) * 256] + half_bi[d:d + 1, :]
            log_a = half_rate[d:d + 1, :] * (1.0 + jnp.tanh(half_ra))
            a = jnp.exp(log_a)
            th = jnp.tanh(log_a)
            q = (-0.5 * th) / (1.0 - th)
            root = jnp.maximum(q, 0.0) * lax.rsqrt(jnp.maximum(q, F32_TINY))
            a_scr[d, s * PITCH:s * PITCH + CHUNK, :] = a
            u_scr[d, s * PITCH:s * PITCH + CHUNK, :] = root * ((1.0 + jnp.tanh(half_ia)) * xc)

    def strided(t):
        return pl.ds(t, N_CHUNK, stride=PITCH)

    @pl.when(jnp.logical_not(is_lat))
    def _():
        hc_scr[...] = jnp.zeros_like(hc_scr)

    @pl.when(is_lat)
    def _():
        def body(t, c):
            pf, ef, pb, eb = c
            af = a_scr[0, strided(t), :]
            ef = af * ef + u_scr[0, strided(t), :]
            pf = pf * af
            tb = CHUNK - 1 - t
            ab = a_scr[1, strided(tb), :]
            eb = ab * eb + u_scr[1, strided(tb), :]
            pb = pb * ab
            return pf, ef, pb, eb

        one = jnp.ones((N_CHUNK, HEAD_DIM), _F32)
        zero = jnp.zeros((N_CHUNK, HEAD_DIM), _F32)
        pf, ef, pb, eb = lax.fori_loop(0, CHUNK, body, (one, zero, one, zero), unroll=8)
        pe_scr[0] = pf
        pe_scr[1] = ef
        pe_scr[2] = pb
        pe_scr[3] = eb
        seq = g - N_CTX_GROUPS
        h = h0_ref[0, pl.ds(seq, 1), :]
        for c in range(N_CHUNK):
            hc_scr[0, c:c + 1, :] = h
            h = pe_scr[0, c:c + 1, :] * h + pe_scr[1, c:c + 1, :]
        h = h0_ref[1, pl.ds(seq, 1), :]
        for c in reversed(range(N_CHUNK)):
            hc_scr[1, c:c + 1, :] = h
            h = pe_scr[2, c:c + 1, :] * h + pe_scr[3, c:c + 1, :]

    def advance_two(d, h, first, second):
        a1 = a_scr[d, strided(first), :]
        u1 = u_scr[d, strided(first), :]
        a2 = a_scr[d, strided(second), :]
        u2 = u_scr[d, strided(second), :]
        h_scr[d, strided(first), :] = a1 * h + u1
        h = (a1 * a2) * h + (a2 * u1 + u2)
        h_scr[d, strided(second), :] = h
        return h

    hf = hc_scr[0]
    hb = hc_scr[1]
    for t in range(0, CHUNK, 2):
        hf = advance_two(0, hf, t, t + 1)
        hb = advance_two(1, hb, CHUNK - 1 - t, CHUNK - 2 - t)
    st_ref[0] = hf
    st_ref[1] = hb

    for s in range(N_CHUNK):
        y = yg_ref[0, s * CHUNK:(s + 1) * CHUNK, :]
        gelu = (0.5 * y) * (1.0 + jnp.tanh(y * (GELU_C + GELU_C3 * (y * y))))
        o = h_scr[0, s * PITCH:s * PITCH + CHUNK, :] + h_scr[1, s * PITCH:s * PITCH + CHUNK, :]
        lru_ref[s * CHUNK:(s + 1) * CHUNK, :] = (o * gelu).astype(_BF)

    fill_padded(lambda lo, n: cg_ref[0, lo:lo + n, :] * v_ref[0, lo:lo + n, :])
    for s in range(N_CHUNK):
        conv = scw_ref[1:2, :] * tap(s, 0) + scw_ref[0:1, :] * tap(s, -1) + scw_ref[2:3, :] * tap(s, 1)
        sc_ref[s * CHUNK:(s + 1) * CHUNK, :] = (bg_ref[0, s * CHUNK:(s + 1) * CHUNK, :] * conv).astype(_BF)


def _mixer(layer, z, cw, cb, w_a, w_i, b_a, b_i, lam, scw, h0):
    def zspec(off):
        return pl.BlockSpec((1, GROUP_ROWS, HEAD_DIM), lambda g, h: (off + h, g, 0))

    def vec(rows):
        return pl.BlockSpec((None, rows, HEAD_DIM), lambda g, h: (layer, 0, h))

    wspec = pl.BlockSpec((None, 2, 1, HEAD_DIM, HEAD_DIM), lambda g, h: (layer, 0, h, 0, 0))
    scan_buf = pltpu.VMEM((2, N_CHUNK * PITCH, HEAD_DIM), _F32)
    return pl.pallas_call(
        _mixer_kernel,
        grid=(N_GROUPS, LRU_HEADS),
        in_specs=[
            zspec(0), zspec(8), zspec(16), zspec(24), zspec(32),
            vec(LRU_CONV), vec(1), wspec, wspec, vec(2), vec(2), vec(2), vec(SHORT_CONV),
            pl.BlockSpec((2, 8, HEAD_DIM), lambda g, h: (0, 0, h)),
        ],
        out_specs=[
            pl.BlockSpec((GROUP_ROWS, HEAD_DIM), lambda g, h: (g, h)),
            pl.BlockSpec((GROUP_ROWS, HEAD_DIM), lambda g, h: (g, h)),
            pl.BlockSpec((2, N_CHUNK, HEAD_DIM), lambda g, h: (0, g, h)),
        ],
        out_shape=[
            jax.ShapeDtypeStruct((ROWS, LRU_WIDTH), _BF),
            jax.ShapeDtypeStruct((ROWS, CONV_WIDTH), _BF),
            jax.ShapeDtypeStruct((2, N_GROUPS * N_CHUNK, LRU_WIDTH), _F32),
        ],
        scratch_shapes=[
            scan_buf, scan_buf, scan_buf,
            pltpu.VMEM((N_CHUNK * XPITCH, HEAD_DIM), _F32),
            pltpu.VMEM((4, N_CHUNK, HEAD_DIM), _F32),
            pltpu.VMEM((2, N_CHUNK, HEAD_DIM), _F32),
        ],
        compiler_params=_cparams(("arbitrary", "arbitrary")),
        name=f"mixer{layer}",
    )(z, z, z, z, z, cw, cb, w_a, w_i, b_a, b_i, lam, scw, h0)


OUT_TM = 512


def _outproj_kernel(lru_ref, sc_ref, x_ref, w_ref, mod_ref, g_ref, x1_ref, hn2_ref, wb_scr):
    @pl.when(pl.program_id(0) == 0)
    def _():
        wb_scr[...] = w_ref[...].astype(_BF)

    grp = _row_group(pl.program_id(0), OUT_TM)
    gate = _mod_row(mod_ref, 2, grp)
    scale = g_ref[1:2, :] * (1.0 + _mod_row(mod_ref, 4, grp))
    shift = _mod_row(mod_ref, 3, grp)
    for r0 in range(0, OUT_TM, OUT_TM // 2):
        rows = slice(r0, r0 + OUT_TM // 2)
        m = jnp.dot(lru_ref[rows, :], wb_scr[0:LRU_WIDTH, :], preferred_element_type=_F32)
        m = m + jnp.dot(sc_ref[rows, :], wb_scr[LRU_WIDTH:, :], preferred_element_type=_F32)
        x1 = x_ref[rows, :] + gate * m
        x1_ref[rows, :] = x1
        hn2_ref[rows, :] = (_rms(x1) * scale + shift).astype(_BF)


def _outproj(layer, lru, sc, x, w_out, mod, norm_g):
    row = lambda i: (i, 0)
    return pl.pallas_call(
        _outproj_kernel,
        grid=(ROWS // OUT_TM,),
        in_specs=[
            pl.BlockSpec((OUT_TM, LRU_WIDTH), row),
            pl.BlockSpec((OUT_TM, CONV_WIDTH), row),
            pl.BlockSpec((OUT_TM, D_MODEL), row),
            pl.BlockSpec((None, D_MODEL, D_MODEL), lambda i: (layer, 0, 0), pipeline_mode=pl.Buffered(1)),
            pl.BlockSpec((None, N_MOD, 8, D_MODEL), lambda i: (layer, 0, 0, 0)),
            pl.BlockSpec((None, 2, D_MODEL), lambda i: (layer, 0, 0)),
        ],
        out_specs=[pl.BlockSpec((OUT_TM, D_MODEL), row), pl.BlockSpec((OUT_TM, D_MODEL), row)],
        out_shape=[
            jax.ShapeDtypeStruct((ROWS, D_MODEL), _F32),
            jax.ShapeDtypeStruct((ROWS, D_MODEL), _BF),
        ],
        scratch_shapes=[pltpu.VMEM((D_MODEL, D_MODEL), _BF)],
        compiler_params=_cparams(("arbitrary",)),
        name=f"outproj{layer}",
    )(lru, sc, x, w_out, mod, norm_g)


FFN_TM = 512
FFN_TF = CAST_CHUNK
FFN_STEPS = D_FF // FFN_TF
FFN_TILES = ROWS // FFN_TM
FFN_CTX_TILES = CTX_ROWS // FFN_TM


def _finished_tile(i, f):
    final = (i == FFN_TILES - 1) & (f == FFN_STEPS - 1)
    return jnp.where(final, FFN_TILES - 1, jnp.maximum(i - 1, 0))


def _ffn_pipeline(hn_ref, wg_ref, wu_ref, wd_ref, acc_a, acc_b, finish, store, side_work=None):
    i = pl.program_id(0)
    f = pl.program_id(1)

    @pl.when((i == 0) & (f == 0))
    def _():
        acc_b[...] = jnp.zeros_like(acc_b)

    def chunk():
        if side_work is not None:
            side_work()
        h = hn_ref[...]
        gate = jnp.dot(h, wg_ref[...], preferred_element_type=_F32)
        up = jnp.dot(h, wu_ref[...], preferred_element_type=_F32)
        act = (gate * _sigmoid(gate) * up).astype(_BF)
        return jnp.dot(act, wd_ref[...], preferred_element_type=_F32)

    def variant(cur, prev):
        @pl.when(f == 0)
        def _():
            tile = jnp.maximum(i - 1, 0)
            values = finish(prev, tile)
            cur[...] = chunk()
            store(values, tile)

        @pl.when(f > 0)
        def _():
            cur[...] += chunk()

        @pl.when((f == FFN_STEPS - 1) & (i == FFN_TILES - 1))
        def _():
            tile = jnp.int32(FFN_TILES - 1)
            store(finish(cur, tile), tile)

    @pl.when(i % 2 == 0)
    def _():
        variant(acc_a, acc_b)

    @pl.when(i % 2 == 1)
    def _():
        variant(acc_b, acc_a)


def _ffn_mid_kernel(hn_ref, x1_ref, wg_ref, wu_ref, wd_ref, mod_ref, modn_ref, gn_ref, wgn_ref, wun_ref, wdn_ref,
                    x2_ref, hnn_ref, wgnb_ref, wunb_ref, wdnb_ref, acc_a, acc_b):
    def finish(acc_ref, tile):
        grp = _row_group(tile, FFN_TM)
        x2 = x1_ref[...] + _mod_row(mod_ref, 5, grp) * acc_ref[...]
        y = _rms(x2) * gn_ref[0:1, :]
        return x2, (y * (1.0 + _mod_row(modn_ref, 1, grp)) + _mod_row(modn_ref, 0, grp)).astype(_BF)

    def store(values, tile):
        x2_ref[...], hnn_ref[...] = values

    def cast_next_layer_weights():
        _cast_hidden_in(wgn_ref, wgnb_ref)
        _cast_hidden_in(wun_ref, wunb_ref)
        wdnb_ref[...] = wdn_ref[...].astype(_BF)

    _ffn_pipeline(hn_ref, wg_ref, wu_ref, wd_ref, acc_a, acc_b, finish, store, cast_next_layer_weights)


def _ffn_last_kernel(hn_ref, x1_ref, wg_ref, wu_ref, wd_ref, mod_ref, fg_ref, yp_ref, ys_ref, acc_a, acc_b):
    def finish(acc_ref, tile):
        grp = _row_group(tile, FFN_TM)
        x2 = x1_ref[...] + _mod_row(mod_ref, 5, grp) * acc_ref[...]
        return _rms(x2) * fg_ref[...]

    def store(y, tile):
        @pl.when(tile < FFN_CTX_TILES)
        def _():
            yp_ref[...] = y

        @pl.when(tile >= FFN_CTX_TILES)
        def _():
            ys_ref[...] = y

    _ffn_pipeline(hn_ref, wg_ref, wu_ref, wd_ref, acc_a, acc_b, finish, store)


def _ffn_specs(layer):
    residual = lambda i, f: (jnp.where(f == FFN_STEPS - 1, i, jnp.maximum(i - 1, 0)), 0)
    return [
        pl.BlockSpec((FFN_TM, D_MODEL), lambda i, f: (i, 0)),
        pl.BlockSpec((FFN_TM, D_MODEL), residual),
        pl.BlockSpec((None, D_MODEL, FFN_TF), lambda i, f: (f, 0, 0)),
        pl.BlockSpec((None, D_MODEL, FFN_TF), lambda i, f: (f, 0, 0)),
        pl.BlockSpec((FFN_TF, D_MODEL), lambda i, f: (f, 0)),
        pl.BlockSpec((None, N_MOD, 8, D_MODEL), lambda i, f: (layer, 0, 0, 0)),
    ]


FFN_GRID_STEPS = FFN_TILES * FFN_STEPS
NEXT_IN_ROWS = 16
NEXT_IN_STEPS = D_MODEL // NEXT_IN_ROWS
NEXT_FF_ROWS = 32
NEXT_FF_STEPS = D_FF // NEXT_FF_ROWS


def _ffn_mid(layer, hn2, x1, wg_b, wu_b, wd_b, mod, norm_g, w_gate, w_up, w_down):
    assert max(NEXT_IN_STEPS, NEXT_FF_STEPS) <= FFN_GRID_STEPS
    row = lambda i, f: (_finished_tile(i, f), 0)
    step = lambda i, f: i * FFN_STEPS + f
    in_rows = lambda i, f: (jnp.minimum(step(i, f), NEXT_IN_STEPS - 1), 0)
    ff_rows = lambda i, f: (jnp.maximum(step(i, f) - (FFN_GRID_STEPS - NEXT_FF_STEPS), 0), 0)
    nxt = layer + 1
    next_in = pl.BlockSpec((None, NEXT_IN_ROWS, D_FF), lambda i, f: (nxt,) + in_rows(i, f))
    next_in_b = pl.BlockSpec((D_FF // CAST_CHUNK, NEXT_IN_ROWS, CAST_CHUNK), lambda i, f: (0,) + in_rows(i, f))
    hidden_b = jax.ShapeDtypeStruct((D_FF // CAST_CHUNK, D_MODEL, CAST_CHUNK), _BF)
    return pl.pallas_call(
        _ffn_mid_kernel,
        grid=(FFN_TILES, FFN_STEPS),
        in_specs=_ffn_specs(layer) + [
            pl.BlockSpec((None, N_MOD, 8, D_MODEL), lambda i, f: (nxt, 0, 0, 0)),
            pl.BlockSpec((None, 2, D_MODEL), lambda i, f: (nxt, 0, 0)),
            next_in, next_in,
            pl.BlockSpec((None, NEXT_FF_ROWS, D_MODEL), lambda i, f: (nxt,) + ff_rows(i, f)),
        ],
        out_specs=[
            pl.BlockSpec((FFN_TM, D_MODEL), row), pl.BlockSpec((FFN_TM, D_MODEL), row),
            next_in_b, next_in_b,
            pl.BlockSpec((NEXT_FF_ROWS, D_MODEL), ff_rows),
        ],
        out_shape=[
            jax.ShapeDtypeStruct((ROWS, D_MODEL), _F32),
            jax.ShapeDtypeStruct((ROWS, D_MODEL), _BF),
            hidden_b, hidden_b,
            jax.ShapeDtypeStruct((D_FF, D_MODEL), _BF),
        ],
        scratch_shapes=[pltpu.VMEM((FFN_TM, D_MODEL), _F32)] * 2,
        compiler_params=_cparams(("arbitrary", "arbitrary")),
        name=f"ffn{layer}",
    )(hn2, x1, wg_b, wu_b, wd_b, mod, mod, norm_g, w_gate, w_up, w_down)


def _ffn_last(layer, hn2, x1, wg_b, wu_b, wd_b, mod, final_g):
    return pl.pallas_call(
        _ffn_last_kernel,
        grid=(FFN_TILES, FFN_STEPS),
        in_specs=_ffn_specs(layer) + [pl.BlockSpec((1, D_MODEL), lambda i, f: (0, 0))],
        out_specs=[
            pl.BlockSpec((FFN_TM, D_MODEL), lambda i, f: (jnp.minimum(_finished_tile(i, f), FFN_CTX_TILES - 1), 0)),
            pl.BlockSpec((FFN_TM, D_MODEL), lambda i, f: (jnp.maximum(_finished_tile(i, f) - FFN_CTX_TILES, 0), 0)),
        ],
        out_shape=[
            jax.ShapeDtypeStruct((CTX_ROWS, D_MODEL), _F32),
            jax.ShapeDtypeStruct((LAT_ROWS, D_MODEL), _F32),
        ],
        scratch_shapes=[pltpu.VMEM((FFN_TM, D_MODEL), _F32)] * 2,
        compiler_params=_cparams(("arbitrary", "arbitrary")),
        name=f"ffn{layer}",
    )(hn2, x1, wg_b, wu_b, wd_b, mod, final_g.reshape(1, D_MODEL))


def _pos_table():
    rows = DEC_SEQ // GRID_W
    r = np.repeat(np.arange(rows, dtype=np.float32), GRID_W)
    col = np.tile(np.arange(GRID_W, dtype=np.float32), rows)
    n_freq = D_MODEL // 4
    freqs = (1.0 / (np.float32(POS_BASE) ** (np.arange(n_freq, dtype=np.float32) / np.float32(n_freq)))).astype(np.float32)
    er = r[:, None] * freqs
    ec = col[:, None] * freqs
    return np.concatenate([np.sin(er), np.cos(er), np.sin(ec), np.cos(ec)], axis=-1).astype(np.float32)


def kernel(x_prompt, x_sample, state_rglru, c, c_ctx, w_mod, b_mod, norm_g, w_in, lru_conv_w, lru_conv_b,
           lru_w_a, lru_b_a, lru_w_i, lru_b_i, lru_lambda, sc_conv_w, w_out, w_gate, w_up, w_down, final_g):
    c8 = jnp.concatenate([c_ctx[None, :], c, jnp.zeros((8 - 1 - DEC_BATCH, D_MODEL), _F32)], axis=0)
    mod = _modulation(c8, w_mod, b_mod)

    pos = jnp.asarray(_pos_table())
    hn, x = _prenorm(x_prompt.reshape(CTX_ROWS, D_MODEL), x_sample.reshape(LAT_ROWS, D_MODEL), pos, mod, norm_g)

    cb = lru_conv_b.reshape(DEPTH, 1, LRU_WIDTH)
    b_a = lru_b_a.reshape(DEPTH, 2, LRU_WIDTH)
    b_i = lru_b_i.reshape(DEPTH, 2, LRU_WIDTH)
    h0 = jnp.pad(jnp.transpose(state_rglru, (1, 2, 0, 3)), ((0, 0), (0, 0), (0, 8 - DEC_BATCH), (0, 0)))

    states = []
    for layer in range(DEPTH):
        if layer == 0:
            z, wg_b, wu_b, wd_b = _inproj_cast(layer, hn, w_in, w_gate, w_up, w_down)
        else:
            z = _inproj(layer, hn, w_in)
        lru, sc, st = _mixer(layer, z, lru_conv_w, cb, lru_w_a, lru_w_i, b_a, b_i, lru_lambda, sc_conv_w, h0[layer])
        states.append(st[:, :BATCH, :])
        x1, hn2 = _outproj(layer, lru, sc, x, w_out, mod, norm_g)
        if layer + 1 < DEPTH:
            x, hn, wg_b, wu_b, wd_b = _ffn_mid(layer, hn2, x1, wg_b, wu_b, wd_b, mod, norm_g, w_gate, w_up, w_down)
        else:
            y_p, y_s = _ffn_last(layer, hn2, x1, wg_b, wu_b, wd_b, mod, final_g)

    new_state = jnp.transpose(jnp.stack(states, axis=0), (2, 0, 1, 3))
    return (y_p.reshape(BATCH, SEQ, D_MODEL), y_s.reshape(DEC_BATCH, DEC_SEQ, D_MODEL), new_state)
```

```python
import math

import numpy as np
import jax
import jax.numpy as jnp
from jax import lax
from jax.experimental import pallas as pl
from jax.experimental.pallas import tpu as pltpu

D_MODEL = 2048
BATCH = 32
SEQ = 256
DEPTH = 2
DEC_BATCH = 2
DEC_SEQ = 2048
GRID_W = 64
LRU_WIDTH = 1024
LRU_HEADS = 8
HEAD_DIM = 128
LRU_CONV = 4
LRU_C = 8.0
CONV_WIDTH = 1024
SHORT_CONV = 3
D_IN = 2 * LRU_WIDTH + 3 * CONV_WIDTH
D_FF = 5632
N_MOD = 6
EPS = 1e-6
POS_BASE = 10000.0

CTX_ROWS = BATCH * SEQ
LAT_ROWS = DEC_BATCH * DEC_SEQ
ROWS = CTX_ROWS + LAT_ROWS
N_SLAB = D_IN // HEAD_DIM

GROUP_ROWS = 2048
CHUNK = 256
N_CHUNK = GROUP_ROWS // CHUNK
N_GROUPS = ROWS // GROUP_ROWS
N_CTX_GROUPS = CTX_ROWS // GROUP_ROWS
SUBLANES = 8
PITCH = CHUNK + SUBLANES
XPITCH = CHUNK + 2 * SUBLANES

V7X_VMEM_LIMIT = 56 * 1024 * 1024
F32_TINY = float(np.finfo(np.float32).tiny)
GELU_C = math.sqrt(2.0 / math.pi)
GELU_C3 = 0.044715 * GELU_C

_BF = jnp.bfloat16
_F32 = jnp.float32


def _cparams(sem):
    return pltpu.CompilerParams(dimension_semantics=sem, vmem_limit_bytes=V7X_VMEM_LIMIT)


def _sigmoid(x):
    return 0.5 * jnp.tanh(0.5 * x) + 0.5


def _row_group(tile, rows_per_tile):
    return jnp.maximum((tile * rows_per_tile) // DEC_SEQ - CTX_ROWS // DEC_SEQ + 1, 0)


def _mod_row(mod_ref, j, grp):
    return mod_ref[j, pl.ds(grp, 1), :]


def _rms(x):
    return x * lax.rsqrt(jnp.mean(x * x, axis=-1, keepdims=True) + EPS)


MOD_TN = 1024


def _mod_kernel(c_ref, w_ref, b_ref, o_ref):
    c = c_ref[...]
    s = (c * _sigmoid(c)).astype(_BF)
    o_ref[...] = jnp.dot(s, w_ref[...].astype(_BF), preferred_element_type=_F32) + b_ref[...]


def _modulation(c8, w_mod, b_mod):
    per = D_MODEL // MOD_TN
    b4 = b_mod.reshape(DEPTH, N_MOD, 1, D_MODEL)
    return pl.pallas_call(
        _mod_kernel,
        grid=(DEPTH, N_MOD * per),
        in_specs=[
            pl.BlockSpec((8, D_MODEL), lambda l, n: (0, 0)),
            pl.BlockSpec((None, D_MODEL, MOD_TN), lambda l, n: (l, 0, n)),
            pl.BlockSpec((None, None, 1, MOD_TN), lambda l, n: (l, n // per, 0, n % per)),
        ],
        out_specs=pl.BlockSpec((None, None, 8, MOD_TN), lambda l, n: (l, n // per, 0, n % per)),
        out_shape=jax.ShapeDtypeStruct((DEPTH, N_MOD, 8, D_MODEL), _F32),
        compiler_params=_cparams(("arbitrary", "arbitrary")),
        name="modulation",
    )(c8, w_mod, b4)


PRE_TM = 512


def _prenorm_kernel(xp_ref, xs_ref, pos_ref, mod_ref, g_ref, hn_ref, x0_ref):
    i = pl.program_id(0)
    is_lat = i >= CTX_ROWS // PRE_TM
    grp = _row_group(i, PRE_TM)
    x = jnp.where(is_lat, xs_ref[...] + pos_ref[...], xp_ref[...])
    x0_ref[...] = x
    y = _rms(x) * g_ref[0:1, :]
    hn = y * (1.0 + _mod_row(mod_ref, 1, grp)) + _mod_row(mod_ref, 0, grp)
    hn_ref[...] = hn.astype(_BF)


def _prenorm(xp, xs, pos, mod, norm_g):
    nct = CTX_ROWS // PRE_TM
    npos = DEC_SEQ // PRE_TM
    return pl.pallas_call(
        _prenorm_kernel,
        grid=(ROWS // PRE_TM,),
        in_specs=[
            pl.BlockSpec((PRE_TM, D_MODEL), lambda i: (jnp.minimum(i, nct - 1), 0)),
            pl.BlockSpec((PRE_TM, D_MODEL), lambda i: (jnp.maximum(i - nct, 0), 0)),
            pl.BlockSpec((PRE_TM, D_MODEL), lambda i: (jnp.maximum(i - nct, 0) % npos, 0)),
            pl.BlockSpec((None, N_MOD, 8, D_MODEL), lambda i: (0, 0, 0, 0)),
            pl.BlockSpec((None, 2, D_MODEL), lambda i: (0, 0, 0)),
        ],
        out_specs=[
            pl.BlockSpec((PRE_TM, D_MODEL), lambda i: (i, 0)),
            pl.BlockSpec((PRE_TM, D_MODEL), lambda i: (i, 0)),
        ],
        out_shape=[
            jax.ShapeDtypeStruct((ROWS, D_MODEL), _BF),
            jax.ShapeDtypeStruct((ROWS, D_MODEL), _F32),
        ],
        compiler_params=_cparams(("arbitrary",)),
        name="prenorm",
    )(xp, xs, pos, mod, norm_g)


IN_TM = 1024
IN_TN = 1024


def _inproj_kernel(hn_ref, w_ref, z_ref, wb_scr):
    @pl.when(pl.program_id(1) == 0)
    def _():
        wb_scr[...] = w_ref[...].astype(_BF)

    acc = jnp.dot(hn_ref[...], wb_scr[...], preferred_element_type=_F32)
    for j in range(IN_TN // HEAD_DIM):
        z_ref[j] = acc[:, j * HEAD_DIM:(j + 1) * HEAD_DIM]


def _inproj(layer, hn, w_in):
    return pl.pallas_call(
        _inproj_kernel,
        grid=(D_IN // IN_TN, ROWS // IN_TM),
        in_specs=[
            pl.BlockSpec((IN_TM, D_MODEL), lambda n, i: (i, 0)),
            pl.BlockSpec((None, D_MODEL, IN_TN), lambda n, i: (layer, 0, n)),
        ],
        out_specs=pl.BlockSpec((IN_TN // HEAD_DIM, IN_TM, HEAD_DIM), lambda n, i: (n, i, 0)),
        out_shape=jax.ShapeDtypeStruct((N_SLAB, ROWS, HEAD_DIM), _F32),
        scratch_shapes=[pltpu.VMEM((D_MODEL, IN_TN), _BF)],
        compiler_params=_cparams(("arbitrary", "arbitrary")),
        name=f"inproj{layer}",
    )(hn, w_in)


MIX_STEPS = N_GROUPS * LRU_HEADS
CAST_STEPS = 32
CAST_CHUNK = 512
CAST_ROWS_IN = D_MODEL // CAST_STEPS
CAST_ROWS_FF = D_FF // CAST_STEPS


def _cast_hidden_in(w_ref, wb_ref):
    for c in range(D_FF // CAST_CHUNK):
        wb_ref[c] = w_ref[:, c * CAST_CHUNK:(c + 1) * CAST_CHUNK].astype(_BF)


def _mixer_kernel(xl_ref, yg_ref, bg_ref, cg_ref, v_ref, cw_ref, cb_ref, wa_ref, wi_ref, ba_ref, bi_ref,
                  lam_ref, scw_ref, h0_ref, wg_ref, wu_ref, wd_ref,
                  lru_ref, sc_ref, st_ref, wgb_ref, wub_ref, wdb_ref,
                  a_scr, u_scr, h_scr, pad_scr, pe_scr, hc_scr):
    g = pl.program_id(0)
    is_lat = g >= N_CTX_GROUPS

    step = g * LRU_HEADS + pl.program_id(1)

    @pl.when(step < CAST_STEPS)
    def _():
        _cast_hidden_in(wg_ref, wgb_ref)
        _cast_hidden_in(wu_ref, wub_ref)

    @pl.when(step >= MIX_STEPS - CAST_STEPS)
    def _():
        wdb_ref[...] = wd_ref[...].astype(_BF)

    def fill_padded(rows):
        zeros = jnp.zeros((SUBLANES, HEAD_DIM), _F32)
        for s in range(N_CHUNK):
            base = s * XPITCH
            lo = s * CHUNK
            front = jnp.where(is_lat, rows(lo - SUBLANES, SUBLANES), 0.0) if s > 0 else zeros
            back = jnp.where(is_lat, rows(lo + CHUNK, SUBLANES), 0.0) if s + 1 < N_CHUNK else zeros
            pad_scr[base:base + SUBLANES, :] = front
            pad_scr[base + SUBLANES:base + SUBLANES + CHUNK, :] = rows(lo, CHUNK)
            pad_scr[base + SUBLANES + CHUNK:base + XPITCH, :] = back

    def tap(s, k):
        start = s * XPITCH + SUBLANES + k
        return pad_scr[start:start + CHUNK, :]

    fill_padded(lambda lo, n: xl_ref[0, lo:lo + n, :])
    wcat = (0.5 * jnp.concatenate([wa_ref[0, 0], wi_ref[0, 0], wa_ref[1, 0], wi_ref[1, 0]], axis=1)).astype(_BF)
    half_ba = 0.5 * ba_ref[...]
    half_bi = 0.5 * bi_ref[...]
    nlam = -lam_ref[...]
    softplus = jnp.maximum(nlam, 0.0) + jnp.log1p(jnp.exp(-jnp.abs(nlam)))
    half_rate = (-0.5 * LRU_C) * softplus

    for s in range(N_CHUNK):
        xc = (cw_ref[2:3, :] * xl_ref[0, s * CHUNK:(s + 1) * CHUNK, :] + cb_ref[...]
              + cw_ref[0:1, :] * tap(s, -2) + cw_ref[1:2, :] * tap(s, -1) + cw_ref[3:4, :] * tap(s, 1))
        gates = jnp.dot(xc.astype(_BF), wcat, preferred_element_type=_F32)
        for d in range(2):
            half_ra = gates[:, d * 256:d * 256 + HEAD_DIM] + half_ba[d:d + 1, :]
            half_ia = gates[:, d * 256 + HEAD_DIM:(d + 1) * 256] + half_bi[d:d + 1, :]
            log_a = half_rate[d:d + 1, :] * (1.0 + jnp.tanh(half_ra))
            a = jnp.exp(log_a)
            th = jnp.tanh(log_a)
            q = (-0.5 * th) / (1.0 - th)
            root = jnp.maximum(q, 0.0) * lax.rsqrt(jnp.maximum(q, F32_TINY))
            a_scr[d, s * PITCH:s * PITCH + CHUNK, :] = a
            u_scr[d, s * PITCH:s * PITCH + CHUNK, :] = root * ((1.0 + jnp.tanh(half_ia)) * xc)

    def strided(t):
        return pl.ds(t, N_CHUNK, stride=PITCH)

    @pl.when(jnp.logical_not(is_lat))
    def _():
        hc_scr[...] = jnp.zeros_like(hc_scr)

    @pl.when(is_lat)
    def _():
        def body(t, c):
            pf, ef, pb, eb = c
            af = a_scr[0, strided(t), :]
            ef = af * ef + u_scr[0, strided(t), :]
            pf = pf * af
            tb = CHUNK - 1 - t
            ab = a_scr[1, strided(tb), :]
            eb = ab * eb + u_scr[1, strided(tb), :]
            pb = pb * ab
            return pf, ef, pb, eb

        one = jnp.ones((N_CHUNK, HEAD_DIM), _F32)
        zero = jnp.zeros((N_CHUNK, HEAD_DIM), _F32)
        pf, ef, pb, eb = lax.fori_loop(0, CHUNK, body, (one, zero, one, zero), unroll=8)
        pe_scr[0] = pf
        pe_scr[1] = ef
        pe_scr[2] = pb
        pe_scr[3] = eb
        seq = g - N_CTX_GROUPS
        h = h0_ref[0, pl.ds(seq, 1), :]
        for c in range(N_CHUNK):
            hc_scr[0, c:c + 1, :] = h
            h = pe_scr[0, c:c + 1, :] * h + pe_scr[1, c:c + 1, :]
        h = h0_ref[1, pl.ds(seq, 1), :]
        for c in reversed(range(N_CHUNK)):
            hc_scr[1, c:c + 1, :] = h
            h = pe_scr[2, c:c + 1, :] * h + pe_scr[3, c:c + 1, :]

    def advance_two(d, h, first, second):
        a1 = a_scr[d, strided(first), :]
        u1 = u_scr[d, strided(first), :]
        a2 = a_scr[d, strided(second), :]
        u2 = u_scr[d, strided(second), :]
        h_scr[d, strided(first), :] = a1 * h + u1
        h = (a1 * a2) * h + (a2 * u1 + u2)
        h_scr[d, strided(second), :] = h
        return h

    hf = hc_scr[0]
    hb = hc_scr[1]
    for t in range(0, CHUNK, 2):
        hf = advance_two(0, hf, t, t + 1)
        hb = advance_two(1, hb, CHUNK - 1 - t, CHUNK - 2 - t)
    st_ref[0] = hf
    st_ref[1] = hb

    for s in range(N_CHUNK):
        y = yg_ref[0, s * CHUNK:(s + 1) * CHUNK, :]
        gelu = (0.5 * y) * (1.0 + jnp.tanh(y * (GELU_C + GELU_C3 * (y * y))))
        o = h_scr[0, s * PITCH:s * PITCH + CHUNK, :] + h_scr[1, s * PITCH:s * PITCH + CHUNK, :]
        lru_ref[s * CHUNK:(s + 1) * CHUNK, :] = (o * gelu).astype(_BF)

    fill_padded(lambda lo, n: cg_ref[0, lo:lo + n, :] * v_ref[0, lo:lo + n, :])
    for s in range(N_CHUNK):
        conv = scw_ref[1:2, :] * tap(s, 0) + scw_ref[0:1, :] * tap(s, -1) + scw_ref[2:3, :] * tap(s, 1)
        sc_ref[s * CHUNK:(s + 1) * CHUNK, :] = (bg_ref[0, s * CHUNK:(s + 1) * CHUNK, :] * conv).astype(_BF)


def _mixer(layer, z, cw, cb, w_a, w_i, b_a, b_i, lam, scw, h0, w_gate, w_up, w_down):
    assert CAST_STEPS <= MIX_STEPS

    def zspec(off):
        return pl.BlockSpec((1, GROUP_ROWS, HEAD_DIM), lambda g, h: (off + h, g, 0))

    def vec(rows):
        return pl.BlockSpec((None, rows, HEAD_DIM), lambda g, h: (layer, 0, h))

    step = lambda g, h: g * LRU_HEADS + h
    early = lambda g, h: (jnp.minimum(step(g, h), CAST_STEPS - 1), 0)
    late = lambda g, h: (jnp.maximum(step(g, h) - (MIX_STEPS - CAST_STEPS), 0), 0)
    in_w = pl.BlockSpec((None, CAST_ROWS_IN, D_FF), lambda g, h: (layer,) + early(g, h))
    in_wb = pl.BlockSpec((D_FF // CAST_CHUNK, CAST_ROWS_IN, CAST_CHUNK), lambda g, h: (0,) + early(g, h))
    hidden_b = jax.ShapeDtypeStruct((D_FF // CAST_CHUNK, D_MODEL, CAST_CHUNK), _BF)
    wspec = pl.BlockSpec((None, 2, 1, HEAD_DIM, HEAD_DIM), lambda g, h: (layer, 0, h, 0, 0))
    scan_buf = pltpu.VMEM((2, N_CHUNK * PITCH, HEAD_DIM), _F32)
    return pl.pallas_call(
        _mixer_kernel,
        grid=(N_GROUPS, LRU_HEADS),
        in_specs=[
            zspec(0), zspec(8), zspec(16), zspec(24), zspec(32),
            vec(LRU_CONV), vec(1), wspec, wspec, vec(2), vec(2), vec(2), vec(SHORT_CONV),
            pl.BlockSpec((2, 8, HEAD_DIM), lambda g, h: (0, 0, h)),
            in_w, in_w,
            pl.BlockSpec((None, CAST_ROWS_FF, D_MODEL), lambda g, h: (layer,) + late(g, h)),
        ],
        out_specs=[
            pl.BlockSpec((GROUP_ROWS, HEAD_DIM), lambda g, h: (g, h)),
            pl.BlockSpec((GROUP_ROWS, HEAD_DIM), lambda g, h: (g, h)),
            pl.BlockSpec((2, N_CHUNK, HEAD_DIM), lambda g, h: (0, g, h)),
            in_wb, in_wb,
            pl.BlockSpec((CAST_ROWS_FF, D_MODEL), late),
        ],
        out_shape=[
            jax.ShapeDtypeStruct((ROWS, LRU_WIDTH), _BF),
            jax.ShapeDtypeStruct((ROWS, CONV_WIDTH), _BF),
            jax.ShapeDtypeStruct((2, N_GROUPS * N_CHUNK, LRU_WIDTH), _F32),
            hidden_b, hidden_b,
            jax.ShapeDtypeStruct((D_FF, D_MODEL), _BF),
        ],
        scratch_shapes=[
            scan_buf, scan_buf, scan_buf,
            pltpu.VMEM((N_CHUNK * XPITCH, HEAD_DIM), _F32),
            pltpu.VMEM((4, N_CHUNK, HEAD_DIM), _F32),
            pltpu.VMEM((2, N_CHUNK, HEAD_DIM), _F32),
        ],
        compiler_params=_cparams(("arbitrary", "arbitrary")),
        name=f"mixer{layer}",
    )(z, z, z, z, z, cw, cb, w_a, w_i, b_a, b_i, lam, scw, h0, w_gate, w_up, w_down)


OUT_TM = 512


def _outproj_kernel(lru_ref, sc_ref, x_ref, w_ref, mod_ref, g_ref, x1_ref, hn2_ref, wb_scr):
    @pl.when(pl.program_id(0) == 0)
    def _():
        wb_scr[...] = w_ref[...].astype(_BF)

    grp = _row_group(pl.program_id(0), OUT_TM)
    gate = _mod_row(mod_ref, 2, grp)
    scale = g_ref[1:2, :] * (1.0 + _mod_row(mod_ref, 4, grp))
    shift = _mod_row(mod_ref, 3, grp)
    for r0 in range(0, OUT_TM, OUT_TM // 2):
        rows = slice(r0, r0 + OUT_TM // 2)
        m = jnp.dot(lru_ref[rows, :], wb_scr[0:LRU_WIDTH, :], preferred_element_type=_F32)
        m = m + jnp.dot(sc_ref[rows, :], wb_scr[LRU_WIDTH:, :], preferred_element_type=_F32)
        x1 = x_ref[rows, :] + gate * m
        x1_ref[rows, :] = x1
        hn2_ref[rows, :] = (_rms(x1) * scale + shift).astype(_BF)


def _outproj(layer, lru, sc, x, w_out, mod, norm_g):
    row = lambda i: (i, 0)
    return pl.pallas_call(
        _outproj_kernel,
        grid=(ROWS // OUT_TM,),
        in_specs=[
            pl.BlockSpec((OUT_TM, LRU_WIDTH), row),
            pl.BlockSpec((OUT_TM, CONV_WIDTH), row),
            pl.BlockSpec((OUT_TM, D_MODEL), row),
            pl.BlockSpec((None, D_MODEL, D_MODEL), lambda i: (layer, 0, 0), pipeline_mode=pl.Buffered(1)),
            pl.BlockSpec((None, N_MOD, 8, D_MODEL), lambda i: (layer, 0, 0, 0)),
            pl.BlockSpec((None, 2, D_MODEL), lambda i: (layer, 0, 0)),
        ],
        out_specs=[pl.BlockSpec((OUT_TM, D_MODEL), row), pl.BlockSpec((OUT_TM, D_MODEL), row)],
        out_shape=[
            jax.ShapeDtypeStruct((ROWS, D_MODEL), _F32),
            jax.ShapeDtypeStruct((ROWS, D_MODEL), _BF),
        ],
        scratch_shapes=[pltpu.VMEM((D_MODEL, D_MODEL), _BF)],
        compiler_params=_cparams(("arbitrary",)),
        name=f"outproj{layer}",
    )(lru, sc, x, w_out, mod, norm_g)


FFN_TM = 512
FFN_TF = CAST_CHUNK
FFN_STEPS = D_FF // FFN_TF
FFN_TILES = ROWS // FFN_TM
FFN_CTX_TILES = CTX_ROWS // FFN_TM


def _finished_tile(i, f):
    final = (i == FFN_TILES - 1) & (f == FFN_STEPS - 1)
    return jnp.where(final, FFN_TILES - 1, jnp.maximum(i - 1, 0))


def _ffn_pipeline(hn_ref, wg_ref, wu_ref, wd_ref, acc_a, acc_b, finish, store):
    i = pl.program_id(0)
    f = pl.program_id(1)

    @pl.when((i == 0) & (f == 0))
    def _():
        acc_b[...] = jnp.zeros_like(acc_b)

    def chunk():
        h = hn_ref[...]
        gate = jnp.dot(h, wg_ref[...], preferred_element_type=_F32)
        up = jnp.dot(h, wu_ref[...], preferred_element_type=_F32)
        act = (gate * _sigmoid(gate) * up).astype(_BF)
        return jnp.dot(act, wd_ref[...], preferred_element_type=_F32)

    def variant(cur, prev):
        @pl.when(f == 0)
        def _():
            tile = jnp.maximum(i - 1, 0)
            values = finish(prev, tile)
            cur[...] = chunk()
            store(values, tile)

        @pl.when(f > 0)
        def _():
            cur[...] += chunk()

        @pl.when((f == FFN_STEPS - 1) & (i == FFN_TILES - 1))
        def _():
            tile = jnp.int32(FFN_TILES - 1)
            store(finish(cur, tile), tile)

    @pl.when(i % 2 == 0)
    def _():
        variant(acc_a, acc_b)

    @pl.when(i % 2 == 1)
    def _():
        variant(acc_b, acc_a)


def _ffn_mid_kernel(hn_ref, x1_ref, wg_ref, wu_ref, wd_ref, mod_ref, modn_ref, gn_ref, x2_ref, hnn_ref,
                    acc_a, acc_b):
    def finish(acc_ref, tile):
        grp = _row_group(tile, FFN_TM)
        x2 = x1_ref[...] + _mod_row(mod_ref, 5, grp) * acc_ref[...]
        y = _rms(x2) * gn_ref[0:1, :]
        return x2, (y * (1.0 + _mod_row(modn_ref, 1, grp)) + _mod_row(modn_ref, 0, grp)).astype(_BF)

    def store(values, tile):
        x2_ref[...], hnn_ref[...] = values

    _ffn_pipeline(hn_ref, wg_ref, wu_ref, wd_ref, acc_a, acc_b, finish, store)


def _ffn_last_kernel(hn_ref, x1_ref, wg_ref, wu_ref, wd_ref, mod_ref, fg_ref, yp_ref, ys_ref, acc_a, acc_b):
    def finish(acc_ref, tile):
        grp = _row_group(tile, FFN_TM)
        x2 = x1_ref[...] + _mod_row(mod_ref, 5, grp) * acc_ref[...]
        return _rms(x2) * fg_ref[...]

    def store(y, tile):
        @pl.when(tile < FFN_CTX_TILES)
        def _():
            yp_ref[...] = y

        @pl.when(tile >= FFN_CTX_TILES)
        def _():
            ys_ref[...] = y

    _ffn_pipeline(hn_ref, wg_ref, wu_ref, wd_ref, acc_a, acc_b, finish, store)


def _ffn_specs(layer):
    residual = lambda i, f: (jnp.where(f == FFN_STEPS - 1, i, jnp.maximum(i - 1, 0)), 0)
    return [
        pl.BlockSpec((FFN_TM, D_MODEL), lambda i, f: (i, 0)),
        pl.BlockSpec((FFN_TM, D_MODEL), residual),
        pl.BlockSpec((None, D_MODEL, FFN_TF), lambda i, f: (f, 0, 0)),
        pl.BlockSpec((None, D_MODEL, FFN_TF), lambda i, f: (f, 0, 0)),
        pl.BlockSpec((FFN_TF, D_MODEL), lambda i, f: (f, 0)),
        pl.BlockSpec((None, N_MOD, 8, D_MODEL), lambda i, f: (layer, 0, 0, 0)),
    ]


def _ffn_mid(layer, hn2, x1, wg_b, wu_b, wd_b, mod, norm_g):
    row = lambda i, f: (_finished_tile(i, f), 0)
    return pl.pallas_call(
        _ffn_mid_kernel,
        grid=(FFN_TILES, FFN_STEPS),
        in_specs=_ffn_specs(layer) + [
            pl.BlockSpec((None, N_MOD, 8, D_MODEL), lambda i, f: (layer + 1, 0, 0, 0)),
            pl.BlockSpec((None, 2, D_MODEL), lambda i, f: (layer + 1, 0, 0)),
        ],
        out_specs=[pl.BlockSpec((FFN_TM, D_MODEL), row), pl.BlockSpec((FFN_TM, D_MODEL), row)],
        out_shape=[
            jax.ShapeDtypeStruct((ROWS, D_MODEL), _F32),
            jax.ShapeDtypeStruct((ROWS, D_MODEL), _BF),
        ],
        scratch_shapes=[pltpu.VMEM((FFN_TM, D_MODEL), _F32)] * 2,
        compiler_params=_cparams(("arbitrary", "arbitrary")),
        name=f"ffn{layer}",
    )(hn2, x1, wg_b, wu_b, wd_b, mod, mod, norm_g)


def _ffn_last(layer, hn2, x1, wg_b, wu_b, wd_b, mod, final_g):
    return pl.pallas_call(
        _ffn_last_kernel,
        grid=(FFN_TILES, FFN_STEPS),
        in_specs=_ffn_specs(layer) + [pl.BlockSpec((1, D_MODEL), lambda i, f: (0, 0))],
        out_specs=[
            pl.BlockSpec((FFN_TM, D_MODEL), lambda i, f: (jnp.minimum(_finished_tile(i, f), FFN_CTX_TILES - 1), 0)),
            pl.BlockSpec((FFN_TM, D_MODEL), lambda i, f: (jnp.maximum(_finished_tile(i, f) - FFN_CTX_TILES, 0), 0)),
        ],
        out_shape=[
            jax.ShapeDtypeStruct((CTX_ROWS, D_MODEL), _F32),
            jax.ShapeDtypeStruct((LAT_ROWS, D_MODEL), _F32),
        ],
        scratch_shapes=[pltpu.VMEM((FFN_TM, D_MODEL), _F32)] * 2,
        compiler_params=_cparams(("arbitrary", "arbitrary")),
        name=f"ffn{layer}",
    )(hn2, x1, wg_b, wu_b, wd_b, mod, final_g.reshape(1, D_MODEL))


def _pos_table():
    rows = DEC_SEQ // GRID_W
    r = np.repeat(np.arange(rows, dtype=np.float32), GRID_W)
    col = np.tile(np.arange(GRID_W, dtype=np.float32), rows)
    n_freq = D_MODEL // 4
    freqs = (1.0 / (np.float32(POS_BASE) ** (np.arange(n_freq, dtype=np.float32) / np.float32(n_freq)))).astype(np.float32)
    er = r[:, None] * freqs
    ec = col[:, None] * freqs
    return np.concatenate([np.sin(er), np.cos(er), np.sin(ec), np.cos(ec)], axis=-1).astype(np.float32)


def kernel(x_prompt, x_sample, state_rglru, c, c_ctx, w_mod, b_mod, norm_g, w_in, lru_conv_w, lru_conv_b,
           lru_w_a, lru_b_a, lru_w_i, lru_b_i, lru_lambda, sc_conv_w, w_out, w_gate, w_up, w_down, final_g):
    c8 = jnp.concatenate([c_ctx[None, :], c, jnp.zeros((8 - 1 - DEC_BATCH, D_MODEL), _F32)], axis=0)
    mod = _modulation(c8, w_mod, b_mod)

    pos = jnp.asarray(_pos_table())
    hn, x = _prenorm(x_prompt.reshape(CTX_ROWS, D_MODEL), x_sample.reshape(LAT_ROWS, D_MODEL), pos, mod, norm_g)

    cb = lru_conv_b.reshape(DEPTH, 1, LRU_WIDTH)
    b_a = lru_b_a.reshape(DEPTH, 2, LRU_WIDTH)
    b_i = lru_b_i.reshape(DEPTH, 2, LRU_WIDTH)
    h0 = jnp.pad(jnp.transpose(state_rglru, (1, 2, 0, 3)), ((0, 0), (0, 0), (0, 8 - DEC_BATCH), (0, 0)))

    states = []
    for layer in range(DEPTH):
        z = _inproj(layer, hn, w_in)
        lru, sc, st, wg_b, wu_b, wd_b = _mixer(layer, z, lru_conv_w, cb, lru_w_a, lru_w_i, b_a, b_i, lru_lambda,
                                               sc_conv_w, h0[layer], w_gate, w_up, w_down)
        states.append(st[:, :BATCH, :])
        x1, hn2 = _outproj(layer, lru, sc, x, w_out, mod, norm_g)
        if layer + 1 < DEPTH:
            x, hn = _ffn_mid(layer, hn2, x1, wg_b, wu_b, wd_b, mod, norm_g)
        else:
            y_p, y_s = _ffn_last(layer, hn2, x1, wg_b, wu_b, wd_b, mod, final_g)

    new_state = jnp.transpose(jnp.stack(states, axis=0), (2, 0, 1, 3))
    return (y_p.reshape(BATCH, SEQ, D_MODEL), y_s.reshape(DEC_BATCH, DEC_SEQ, D_MODEL), new_state)
```

```python
import math

import numpy as np
import jax
import jax.numpy as jnp
from jax import lax
from jax.experimental import pallas as pl
from jax.experimental.pallas import tpu as pltpu

D_MODEL = 2048
BATCH = 32
SEQ = 256
DEPTH = 2
DEC_BATCH = 2
DEC_SEQ = 2048
GRID_W = 64
LRU_WIDTH = 1024
LRU_HEADS = 8
HEAD_DIM = 128
LRU_CONV = 4
LRU_C = 8.0
CONV_WIDTH = 1024
SHORT_CONV = 3
D_IN = 2 * LRU_WIDTH + 3 * CONV_WIDTH
D_FF = 5632
N_MOD = 6
EPS = 1e-6
POS_BASE = 10000.0

CTX_ROWS = BATCH * SEQ
LAT_ROWS = DEC_BATCH * DEC_SEQ
ROWS = CTX_ROWS + LAT_ROWS
N_SLAB = D_IN // HEAD_DIM

GROUP_ROWS = 2048
CHUNK = 256
N_CHUNK = GROUP_ROWS // CHUNK
N_GROUPS = ROWS // GROUP_ROWS
N_CTX_GROUPS = CTX_ROWS // GROUP_ROWS
SUBLANES = 8
PITCH = CHUNK + SUBLANES
XPITCH = CHUNK + 2 * SUBLANES

V7X_VMEM_LIMIT = 56 * 1024 * 1024
F32_TINY = float(np.finfo(np.float32).tiny)
GELU_C = math.sqrt(2.0 / math.pi)
GELU_C3 = 0.044715 * GELU_C

_BF = jnp.bfloat16
_F32 = jnp.float32


def _cparams(sem):
    return pltpu.CompilerParams(dimension_semantics=sem, vmem_limit_bytes=V7X_VMEM_LIMIT)


def _sigmoid(x):
    return 0.5 * jnp.tanh(0.5 * x) + 0.5


def _row_group(tile, rows_per_tile):
    return jnp.maximum((tile * rows_per_tile) // DEC_SEQ - CTX_ROWS // DEC_SEQ + 1, 0)


def _mod_row(mod_ref, j, grp):
    return mod_ref[j, pl.ds(grp, 1), :]


def _rms(x):
    return x * lax.rsqrt(jnp.mean(x * x, axis=-1, keepdims=True) + EPS)


MOD_TN = 1024


def _mod_kernel(c_ref, w_ref, b_ref, o_ref):
    c = c_ref[...]
    s = (c * _sigmoid(c)).astype(_BF)
    o_ref[...] = jnp.dot(s, w_ref[...].astype(_BF), preferred_element_type=_F32) + b_ref[...]


def _modulation(c8, w_mod, b_mod):
    per = D_MODEL // MOD_TN
    b4 = b_mod.reshape(DEPTH, N_MOD, 1, D_MODEL)
    return pl.pallas_call(
        _mod_kernel,
        grid=(DEPTH, N_MOD * per),
        in_specs=[
            pl.BlockSpec((8, D_MODEL), lambda l, n: (0, 0)),
            pl.BlockSpec((None, D_MODEL, MOD_TN), lambda l, n: (l, 0, n)),
            pl.BlockSpec((None, None, 1, MOD_TN), lambda l, n: (l, n // per, 0, n % per)),
        ],
        out_specs=pl.BlockSpec((None, None, 8, MOD_TN), lambda l, n: (l, n // per, 0, n % per)),
        out_shape=jax.ShapeDtypeStruct((DEPTH, N_MOD, 8, D_MODEL), _F32),
        compiler_params=_cparams(("arbitrary", "arbitrary")),
        name="modulation",
    )(c8, w_mod, b4)


PRE_TM = 512


def _prenorm_kernel(xp_ref, xs_ref, pos_ref, mod_ref, g_ref, hn_ref, x0_ref):
    i = pl.program_id(0)
    is_lat = i >= CTX_ROWS // PRE_TM
    grp = _row_group(i, PRE_TM)
    x = jnp.where(is_lat, xs_ref[...] + pos_ref[...], xp_ref[...])

    @pl.when(is_lat)
    def _():
        x0_ref[...] = x

    y = _rms(x) * g_ref[0:1, :]
    hn = y * (1.0 + _mod_row(mod_ref, 1, grp)) + _mod_row(mod_ref, 0, grp)
    hn_ref[...] = hn.astype(_BF)


def _prenorm(xp, xs, pos, mod, norm_g):
    nct = CTX_ROWS // PRE_TM
    npos = DEC_SEQ // PRE_TM
    return pl.pallas_call(
        _prenorm_kernel,
        grid=(ROWS // PRE_TM,),
        in_specs=[
            pl.BlockSpec((PRE_TM, D_MODEL), lambda i: (jnp.minimum(i, nct - 1), 0)),
            pl.BlockSpec((PRE_TM, D_MODEL), lambda i: (jnp.maximum(i - nct, 0), 0)),
            pl.BlockSpec((PRE_TM, D_MODEL), lambda i: (jnp.maximum(i - nct, 0) % npos, 0)),
            pl.BlockSpec((None, N_MOD, 8, D_MODEL), lambda i: (0, 0, 0, 0)),
            pl.BlockSpec((None, 2, D_MODEL), lambda i: (0, 0, 0)),
        ],
        out_specs=[
            pl.BlockSpec((PRE_TM, D_MODEL), lambda i: (i, 0)),
            pl.BlockSpec((PRE_TM, D_MODEL), lambda i: (jnp.maximum(i - nct, 0), 0)),
        ],
        out_shape=[
            jax.ShapeDtypeStruct((ROWS, D_MODEL), _BF),
            jax.ShapeDtypeStruct((LAT_ROWS, D_MODEL), _F32),
        ],
        compiler_params=_cparams(("arbitrary",)),
        name="prenorm",
    )(xp, xs, pos, mod, norm_g)


IN_TM = 1024
IN_TN = 1024


def _inproj_kernel(hn_ref, w_ref, z_ref, wb_scr):
    @pl.when(pl.program_id(1) == 0)
    def _():
        wb_scr[...] = w_ref[...].astype(_BF)

    acc = jnp.dot(hn_ref[...], wb_scr[...], preferred_element_type=_F32)
    for j in range(IN_TN // HEAD_DIM):
        z_ref[j] = acc[:, j * HEAD_DIM:(j + 1) * HEAD_DIM]


def _inproj(layer, hn, w_in):
    return pl.pallas_call(
        _inproj_kernel,
        grid=(D_IN // IN_TN, ROWS // IN_TM),
        in_specs=[
            pl.BlockSpec((IN_TM, D_MODEL), lambda n, i: (i, 0)),
            pl.BlockSpec((None, D_MODEL, IN_TN), lambda n, i: (layer, 0, n)),
        ],
        out_specs=pl.BlockSpec((IN_TN // HEAD_DIM, IN_TM, HEAD_DIM), lambda n, i: (n, i, 0)),
        out_shape=jax.ShapeDtypeStruct((N_SLAB, ROWS, HEAD_DIM), _F32),
        scratch_shapes=[pltpu.VMEM((D_MODEL, IN_TN), _BF)],
        compiler_params=_cparams(("arbitrary", "arbitrary")),
        name=f"inproj{layer}",
    )(hn, w_in)


MIX_STEPS = N_GROUPS * LRU_HEADS
CAST_STEPS = 32
CAST_CHUNK = 512
CAST_ROWS_IN = D_MODEL // CAST_STEPS
CAST_ROWS_FF = D_FF // CAST_STEPS


def _cast_hidden_in(w_ref, wb_ref):
    for c in range(D_FF // CAST_CHUNK):
        wb_ref[c] = w_ref[:, c * CAST_CHUNK:(c + 1) * CAST_CHUNK].astype(_BF)


def _mixer_kernel(xl_ref, yg_ref, bg_ref, cg_ref, v_ref, cw_ref, cb_ref, wa_ref, wi_ref, ba_ref, bi_ref,
                  lam_ref, scw_ref, h0_ref, wg_ref, wu_ref, wd_ref, wo_ref,
                  lru_ref, sc_ref, st_ref, wgb_ref, wub_ref, wdb_ref, wob_ref,
                  a_scr, u_scr, h_scr, pad_scr, pe_scr, hc_scr):
    g = pl.program_id(0)
    is_lat = g >= N_CTX_GROUPS

    step = g * LRU_HEADS + pl.program_id(1)

    @pl.when(step < CAST_STEPS)
    def _():
        _cast_hidden_in(wg_ref, wgb_ref)
        _cast_hidden_in(wu_ref, wub_ref)
        wob_ref[...] = wo_ref[...].astype(_BF)

    @pl.when(step >= MIX_STEPS - CAST_STEPS)
    def _():
        wdb_ref[...] = wd_ref[...].astype(_BF)

    def fill_padded(rows):
        zeros = jnp.zeros((SUBLANES, HEAD_DIM), _F32)
        for s in range(N_CHUNK):
            base = s * XPITCH
            lo = s * CHUNK
            front = jnp.where(is_lat, rows(lo - SUBLANES, SUBLANES), 0.0) if s > 0 else zeros
            back = jnp.where(is_lat, rows(lo + CHUNK, SUBLANES), 0.0) if s + 1 < N_CHUNK else zeros
            pad_scr[base:base + SUBLANES, :] = front
            pad_scr[base + SUBLANES:base + SUBLANES + CHUNK, :] = rows(lo, CHUNK)
            pad_scr[base + SUBLANES + CHUNK:base + XPITCH, :] = back

    def tap(s, k):
        start = s * XPITCH + SUBLANES + k
        return pad_scr[start:start + CHUNK, :]

    fill_padded(lambda lo, n: xl_ref[0, lo:lo + n, :])
    wcat = (0.5 * jnp.concatenate([wa_ref[0, 0], wi_ref[0, 0], wa_ref[1, 0], wi_ref[1, 0]], axis=1)).astype(_BF)
    half_ba = 0.5 * ba_ref[...]
    half_bi = 0.5 * bi_ref[...]
    nlam = -lam_ref[...]
    softplus = jnp.maximum(nlam, 0.0) + jnp.log1p(jnp.exp(-jnp.abs(nlam)))
    half_rate = (-0.5 * LRU_C) * softplus

    for s in range(N_CHUNK):
        xc = (cw_ref[2:3, :] * xl_ref[0, s * CHUNK:(s + 1) * CHUNK, :] + cb_ref[...]
              + cw_ref[0:1, :] * tap(s, -2) + cw_ref[1:2, :] * tap(s, -1) + cw_ref[3:4, :] * tap(s, 1))
        gates = jnp.dot(xc.astype(_BF), wcat, preferred_element_type=_F32)
        for d in range(2):
            half_ra = gates[:, d * 256:d * 256 + HEAD_DIM] + half_ba[d:d + 1, :]
            half_ia = gates[:, d * 256 + HEAD_DIM:(d + 1) * 256] + half_bi[d:d + 1, :]
            log_a = half_rate[d:d + 1, :] * (1.0 + jnp.tanh(half_ra))
            a = jnp.exp(log_a)
            th = jnp.tanh(log_a)
            q = (-0.5 * th) / (1.0 - th)
            root = jnp.maximum(q, 0.0) * lax.rsqrt(jnp.maximum(q, F32_TINY))
            a_scr[d, s * PITCH:s * PITCH + CHUNK, :] = a
            u_scr[d, s * PITCH:s * PITCH + CHUNK, :] = root * ((1.0 + jnp.tanh(half_ia)) * xc)

    def strided(t):
        return pl.ds(t, N_CHUNK, stride=PITCH)

    @pl.when(jnp.logical_not(is_lat))
    def _():
        hc_scr[...] = jnp.zeros_like(hc_scr)

    @pl.when(is_lat)
    def _():
        def body(t, c):
            pf, ef, pb, eb = c
            af = a_scr[0, strided(t), :]
            ef = af * ef + u_scr[0, strided(t), :]
            pf = pf * af
            tb = CHUNK - 1 - t
            ab = a_scr[1, strided(tb), :]
            eb = ab * eb + u_scr[1, strided(tb), :]
            pb = pb * ab
            return pf, ef, pb, eb

        one = jnp.ones((N_CHUNK, HEAD_DIM), _F32)
        zero = jnp.zeros((N_CHUNK, HEAD_DIM), _F32)
        pf, ef, pb, eb = lax.fori_loop(0, CHUNK, body, (one, zero, one, zero), unroll=8)
        pe_scr[0] = pf
        pe_scr[1] = ef
        pe_scr[2] = pb
        pe_scr[3] = eb
        seq = g - N_CTX_GROUPS
        h = h0_ref[0, pl.ds(seq, 1), :]
        for c in range(N_CHUNK):
            hc_scr[0, c:c + 1, :] = h
            h = pe_scr[0, c:c + 1, :] * h + pe_scr[1, c:c + 1, :]
        h = h0_ref[1, pl.ds(seq, 1), :]
        for c in reversed(range(N_CHUNK)):
            hc_scr[1, c:c + 1, :] = h
            h = pe_scr[2, c:c + 1, :] * h + pe_scr[3, c:c + 1, :]

    def advance_two(d, h, first, second):
        a1 = a_scr[d, strided(first), :]
        u1 = u_scr[d, strided(first), :]
        a2 = a_scr[d, strided(second), :]
        u2 = u_scr[d, strided(second), :]
        h_scr[d, strided(first), :] = a1 * h + u1
        h = (a1 * a2) * h + (a2 * u1 + u2)
        h_scr[d, strided(second), :] = h
        return h

    hf = hc_scr[0]
    hb = hc_scr[1]
    for t in range(0, CHUNK, 2):
        hf = advance_two(0, hf, t, t + 1)
        hb = advance_two(1, hb, CHUNK - 1 - t, CHUNK - 2 - t)
    st_ref[0] = hf
    st_ref[1] = hb

    for s in range(N_CHUNK):
        y = yg_ref[0, s * CHUNK:(s + 1) * CHUNK, :]
        gelu = (0.5 * y) * (1.0 + jnp.tanh(y * (GELU_C + GELU_C3 * (y * y))))
        o = h_scr[0, s * PITCH:s * PITCH + CHUNK, :] + h_scr[1, s * PITCH:s * PITCH + CHUNK, :]
        lru_ref[s * CHUNK:(s + 1) * CHUNK, :] = (o * gelu).astype(_BF)

    fill_padded(lambda lo, n: cg_ref[0, lo:lo + n, :] * v_ref[0, lo:lo + n, :])
    for s in range(N_CHUNK):
        conv = scw_ref[1:2, :] * tap(s, 0) + scw_ref[0:1, :] * tap(s, -1) + scw_ref[2:3, :] * tap(s, 1)
        sc_ref[s * CHUNK:(s + 1) * CHUNK, :] = (bg_ref[0, s * CHUNK:(s + 1) * CHUNK, :] * conv).astype(_BF)


def _mixer(layer, z, cw, cb, w_a, w_i, b_a, b_i, lam, scw, h0, w_gate, w_up, w_down, w_out):
    assert CAST_STEPS <= MIX_STEPS

    def zspec(off):
        return pl.BlockSpec((1, GROUP_ROWS, HEAD_DIM), lambda g, h: (off + h, g, 0))

    def vec(rows):
        return pl.BlockSpec((None, rows, HEAD_DIM), lambda g, h: (layer, 0, h))

    step = lambda g, h: g * LRU_HEADS + h
    early = lambda g, h: (jnp.minimum(step(g, h), CAST_STEPS - 1), 0)
    late = lambda g, h: (jnp.maximum(step(g, h) - (MIX_STEPS - CAST_STEPS), 0), 0)
    in_w = pl.BlockSpec((None, CAST_ROWS_IN, D_FF), lambda g, h: (layer,) + early(g, h))
    in_wb = pl.BlockSpec((D_FF // CAST_CHUNK, CAST_ROWS_IN, CAST_CHUNK), lambda g, h: (0,) + early(g, h))
    hidden_b = jax.ShapeDtypeStruct((D_FF // CAST_CHUNK, D_MODEL, CAST_CHUNK), _BF)
    wspec = pl.BlockSpec((None, 2, 1, HEAD_DIM, HEAD_DIM), lambda g, h: (layer, 0, h, 0, 0))
    scan_buf = pltpu.VMEM((2, N_CHUNK * PITCH, HEAD_DIM), _F32)
    return pl.pallas_call(
        _mixer_kernel,
        grid=(N_GROUPS, LRU_HEADS),
        in_specs=[
            zspec(0), zspec(8), zspec(16), zspec(24), zspec(32),
            vec(LRU_CONV), vec(1), wspec, wspec, vec(2), vec(2), vec(2), vec(SHORT_CONV),
            pl.BlockSpec((2, 8, HEAD_DIM), lambda g, h: (0, 0, h)),
            in_w, in_w,
            pl.BlockSpec((None, CAST_ROWS_FF, D_MODEL), lambda g, h: (layer,) + late(g, h)),
            pl.BlockSpec((None, CAST_ROWS_IN, D_MODEL), lambda g, h: (layer,) + early(g, h)),
        ],
        out_specs=[
            pl.BlockSpec((GROUP_ROWS, HEAD_DIM), lambda g, h: (g, h)),
            pl.BlockSpec((GROUP_ROWS, HEAD_DIM), lambda g, h: (g, h)),
            pl.BlockSpec((2, N_CHUNK, HEAD_DIM), lambda g, h: (0, g, h)),
            in_wb, in_wb,
            pl.BlockSpec((CAST_ROWS_FF, D_MODEL), late),
            pl.BlockSpec((CAST_ROWS_IN, D_MODEL), early),
        ],
        out_shape=[
            jax.ShapeDtypeStruct((ROWS, LRU_WIDTH), _BF),
            jax.ShapeDtypeStruct((ROWS, CONV_WIDTH), _BF),
            jax.ShapeDtypeStruct((2, N_GROUPS * N_CHUNK, LRU_WIDTH), _F32),
            hidden_b, hidden_b,
            jax.ShapeDtypeStruct((D_FF, D_MODEL), _BF),
            jax.ShapeDtypeStruct((D_MODEL, D_MODEL), _BF),
        ],
        scratch_shapes=[
            scan_buf, scan_buf, scan_buf,
            pltpu.VMEM((N_CHUNK * XPITCH, HEAD_DIM), _F32),
            pltpu.VMEM((4, N_CHUNK, HEAD_DIM), _F32),
            pltpu.VMEM((2, N_CHUNK, HEAD_DIM), _F32),
        ],
        compiler_params=_cparams(("arbitrary", "arbitrary")),
        name=f"mixer{layer}",
    )(z, z, z, z, z, cw, cb, w_a, w_i, b_a, b_i, lam, scw, h0, w_gate, w_up, w_down, w_out)


OUT_TM = 512


def _outproj_body(lru_ref, sc_ref, residual, w_ref, mod_ref, g_ref, x1_ref, hn2_ref):
    grp = _row_group(pl.program_id(0), OUT_TM)
    gate = _mod_row(mod_ref, 2, grp)
    scale = g_ref[1:2, :] * (1.0 + _mod_row(mod_ref, 4, grp))
    shift = _mod_row(mod_ref, 3, grp)
    for r0 in range(0, OUT_TM, OUT_TM // 2):
        rows = slice(r0, r0 + OUT_TM // 2)
        m = jnp.dot(lru_ref[rows, :], w_ref[0:LRU_WIDTH, :], preferred_element_type=_F32)
        m = m + jnp.dot(sc_ref[rows, :], w_ref[LRU_WIDTH:, :], preferred_element_type=_F32)
        x1 = residual(rows) + gate * m
        x1_ref[rows, :] = x1
        hn2_ref[rows, :] = (_rms(x1) * scale + shift).astype(_BF)


def _outproj_kernel(lru_ref, sc_ref, x_ref, w_ref, mod_ref, g_ref, x1_ref, hn2_ref):
    _outproj_body(lru_ref, sc_ref, lambda rows: x_ref[rows, :], w_ref, mod_ref, g_ref, x1_ref, hn2_ref)


def _outproj_split_kernel(lru_ref, sc_ref, xc_ref, xl_ref, w_ref, mod_ref, g_ref, x1_ref, hn2_ref):
    is_lat = pl.program_id(0) >= CTX_ROWS // OUT_TM
    residual = lambda rows: jnp.where(is_lat, xl_ref[rows, :], xc_ref[rows, :])
    _outproj_body(lru_ref, sc_ref, residual, w_ref, mod_ref, g_ref, x1_ref, hn2_ref)


def _outproj(layer, lru, sc, x, w_out_b, mod, norm_g):
    row = lambda i: (i, 0)
    nct = CTX_ROWS // OUT_TM
    if isinstance(x, tuple):
        body = _outproj_split_kernel
        x_specs = [pl.BlockSpec((OUT_TM, D_MODEL), lambda i: (jnp.minimum(i, nct - 1), 0)),
                   pl.BlockSpec((OUT_TM, D_MODEL), lambda i: (jnp.maximum(i - nct, 0), 0))]
    else:
        body = _outproj_kernel
        x_specs = [pl.BlockSpec((OUT_TM, D_MODEL), row)]
        x = (x,)
    return pl.pallas_call(
        body,
        grid=(ROWS // OUT_TM,),
        in_specs=[pl.BlockSpec((OUT_TM, LRU_WIDTH), row), pl.BlockSpec((OUT_TM, CONV_WIDTH), row)] + x_specs + [
            pl.BlockSpec((D_MODEL, D_MODEL), lambda i: (0, 0), pipeline_mode=pl.Buffered(1)),
            pl.BlockSpec((None, N_MOD, 8, D_MODEL), lambda i: (layer, 0, 0, 0)),
            pl.BlockSpec((None, 2, D_MODEL), lambda i: (layer, 0, 0)),
        ],
        out_specs=[pl.BlockSpec((OUT_TM, D_MODEL), row), pl.BlockSpec((OUT_TM, D_MODEL), row)],
        out_shape=[
            jax.ShapeDtypeStruct((ROWS, D_MODEL), _F32),
            jax.ShapeDtypeStruct((ROWS, D_MODEL), _BF),
        ],
        compiler_params=_cparams(("arbitrary",)),
        name=f"outproj{layer}",
    )(lru, sc, *x, w_out_b, mod, norm_g)


FFN_TM = 512
FFN_TF = CAST_CHUNK
FFN_STEPS = D_FF // FFN_TF
FFN_TILES = ROWS // FFN_TM
FFN_CTX_TILES = CTX_ROWS // FFN_TM


def _finished_tile(i, f):
    final = (i == FFN_TILES - 1) & (f == FFN_STEPS - 1)
    return jnp.where(final, FFN_TILES - 1, jnp.maximum(i - 1, 0))


def _ffn_pipeline(hn_ref, wg_ref, wu_ref, wd_ref, acc_a, acc_b, finish, store):
    i = pl.program_id(0)
    f = pl.program_id(1)

    @pl.when((i == 0) & (f == 0))
    def _():
        acc_b[...] = jnp.zeros_like(acc_b)

    def chunk():
        h = hn_ref[...]
        gate = jnp.dot(h, wg_ref[...], preferred_element_type=_F32)
        up = jnp.dot(h, wu_ref[...], preferred_element_type=_F32)
        act = (gate * _sigmoid(gate) * up).astype(_BF)
        return jnp.dot(act, wd_ref[...], preferred_element_type=_F32)

    def variant(cur, prev):
        @pl.when(f == 0)
        def _():
            tile = jnp.maximum(i - 1, 0)
            values = finish(prev, tile)
            cur[...] = chunk()
            store(values, tile)

        @pl.when(f > 0)
        def _():
            cur[...] += chunk()

        @pl.when((f == FFN_STEPS - 1) & (i == FFN_TILES - 1))
        def _():
            tile = jnp.int32(FFN_TILES - 1)
            store(finish(cur, tile), tile)

    @pl.when(i % 2 == 0)
    def _():
        variant(acc_a, acc_b)

    @pl.when(i % 2 == 1)
    def _():
        variant(acc_b, acc_a)


def _ffn_mid_kernel(hn_ref, x1_ref, wg_ref, wu_ref, wd_ref, mod_ref, modn_ref, gn_ref, x2_ref, hnn_ref,
                    acc_a, acc_b):
    def finish(acc_ref, tile):
        grp = _row_group(tile, FFN_TM)
        x2 = x1_ref[...] + _mod_row(mod_ref, 5, grp) * acc_ref[...]
        y = _rms(x2) * gn_ref[0:1, :]
        return x2, (y * (1.0 + _mod_row(modn_ref, 1, grp)) + _mod_row(modn_ref, 0, grp)).astype(_BF)

    def store(values, tile):
        x2_ref[...], hnn_ref[...] = values

    _ffn_pipeline(hn_ref, wg_ref, wu_ref, wd_ref, acc_a, acc_b, finish, store)


def _ffn_last_kernel(hn_ref, x1_ref, wg_ref, wu_ref, wd_ref, mod_ref, fg_ref, yp_ref, ys_ref, acc_a, acc_b):
    def finish(acc_ref, tile):
        grp = _row_group(tile, FFN_TM)
        x2 = x1_ref[...] + _mod_row(mod_ref, 5, grp) * acc_ref[...]
        return _rms(x2) * fg_ref[...]

    def store(y, tile):
        @pl.when(tile < FFN_CTX_TILES)
        def _():
            yp_ref[...] = y

        @pl.when(tile >= FFN_CTX_TILES)
        def _():
            ys_ref[...] = y

    _ffn_pipeline(hn_ref, wg_ref, wu_ref, wd_ref, acc_a, acc_b, finish, store)


def _ffn_specs(layer):
    residual = lambda i, f: (jnp.where(f == FFN_STEPS - 1, i, jnp.maximum(i - 1, 0)), 0)
    return [
        pl.BlockSpec((FFN_TM, D_MODEL), lambda i, f: (i, 0)),
        pl.BlockSpec((FFN_TM, D_MODEL), residual),
        pl.BlockSpec((None, D_MODEL, FFN_TF), lambda i, f: (f, 0, 0)),
        pl.BlockSpec((None, D_MODEL, FFN_TF), lambda i, f: (f, 0, 0)),
        pl.BlockSpec((FFN_TF, D_MODEL), lambda i, f: (f, 0)),
        pl.BlockSpec((None, N_MOD, 8, D_MODEL), lambda i, f: (layer, 0, 0, 0)),
    ]


def _ffn_mid(layer, hn2, x1, wg_b, wu_b, wd_b, mod, norm_g):
    row = lambda i, f: (_finished_tile(i, f), 0)
    return pl.pallas_call(
        _ffn_mid_kernel,
        grid=(FFN_TILES, FFN_STEPS),
        in_specs=_ffn_specs(layer) + [
            pl.BlockSpec((None, N_MOD, 8, D_MODEL), lambda i, f: (layer + 1, 0, 0, 0)),
            pl.BlockSpec((None, 2, D_MODEL), lambda i, f: (layer + 1, 0, 0)),
        ],
        out_specs=[pl.BlockSpec((FFN_TM, D_MODEL), row), pl.BlockSpec((FFN_TM, D_MODEL), row)],
        out_shape=[
            jax.ShapeDtypeStruct((ROWS, D_MODEL), _F32),
            jax.ShapeDtypeStruct((ROWS, D_MODEL), _BF),
        ],
        scratch_shapes=[pltpu.VMEM((FFN_TM, D_MODEL), _F32)] * 2,
        compiler_params=_cparams(("arbitrary", "arbitrary")),
        name=f"ffn{layer}",
    )(hn2, x1, wg_b, wu_b, wd_b, mod, mod, norm_g)


def _ffn_last(layer, hn2, x1, wg_b, wu_b, wd_b, mod, final_g):
    return pl.pallas_call(
        _ffn_last_kernel,
        grid=(FFN_TILES, FFN_STEPS),
        in_specs=_ffn_specs(layer) + [pl.BlockSpec((1, D_MODEL), lambda i, f: (0, 0))],
        out_specs=[
            pl.BlockSpec((FFN_TM, D_MODEL), lambda i, f: (jnp.minimum(_finished_tile(i, f), FFN_CTX_TILES - 1), 0)),
            pl.BlockSpec((FFN_TM, D_MODEL), lambda i, f: (jnp.maximum(_finished_tile(i, f) - FFN_CTX_TILES, 0), 0)),
        ],
        out_shape=[
            jax.ShapeDtypeStruct((CTX_ROWS, D_MODEL), _F32),
            jax.ShapeDtypeStruct((LAT_ROWS, D_MODEL), _F32),
        ],
        scratch_shapes=[pltpu.VMEM((FFN_TM, D_MODEL), _F32)] * 2,
        compiler_params=_cparams(("arbitrary", "arbitrary")),
        name=f"ffn{layer}",
    )(hn2, x1, wg_b, wu_b, wd_b, mod, final_g.reshape(1, D_MODEL))


def _pos_table():
    rows = DEC_SEQ // GRID_W
    r = np.repeat(np.arange(rows, dtype=np.float32), GRID_W)
    col = np.tile(np.arange(GRID_W, dtype=np.float32), rows)
    n_freq = D_MODEL // 4
    freqs = (1.0 / (np.float32(POS_BASE) ** (np.arange(n_freq, dtype=np.float32) / np.float32(n_freq)))).astype(np.float32)
    er = r[:, None] * freqs
    ec = col[:, None] * freqs
    return np.concatenate([np.sin(er), np.cos(er), np.sin(ec), np.cos(ec)], axis=-1).astype(np.float32)


def kernel(x_prompt, x_sample, state_rglru, c, c_ctx, w_mod, b_mod, norm_g, w_in, lru_conv_w, lru_conv_b,
           lru_w_a, lru_b_a, lru_w_i, lru_b_i, lru_lambda, sc_conv_w, w_out, w_gate, w_up, w_down, final_g):
    c8 = jnp.concatenate([c_ctx[None, :], c, jnp.zeros((8 - 1 - DEC_BATCH, D_MODEL), _F32)], axis=0)
    mod = _modulation(c8, w_mod, b_mod)

    pos = jnp.asarray(_pos_table())
    xp = x_prompt.reshape(CTX_ROWS, D_MODEL)
    hn, x_lat = _prenorm(xp, x_sample.reshape(LAT_ROWS, D_MODEL), pos, mod, norm_g)
    x = (xp, x_lat)

    cb = lru_conv_b.reshape(DEPTH, 1, LRU_WIDTH)
    b_a = lru_b_a.reshape(DEPTH, 2, LRU_WIDTH)
    b_i = lru_b_i.reshape(DEPTH, 2, LRU_WIDTH)
    h0 = jnp.pad(jnp.transpose(state_rglru, (1, 2, 0, 3)), ((0, 0), (0, 0), (0, 8 - DEC_BATCH), (0, 0)))

    states = []
    for layer in range(DEPTH):
        z = _inproj(layer, hn, w_in)
        lru, sc, st, wg_b, wu_b, wd_b, wo_b = _mixer(layer, z, lru_conv_w, cb, lru_w_a, lru_w_i, b_a, b_i, lru_lambda,
                                                     sc_conv_w, h0[layer], w_gate, w_up, w_down, w_out)
        states.append(st[:, :BATCH, :])
        x1, hn2 = _outproj(layer, lru, sc, x, wo_b, mod, norm_g)
        if layer + 1 < DEPTH:
            x, hn = _ffn_mid(layer, hn2, x1, wg_b, wu_b, wd_b, mod, norm_g)
        else:
            y_p, y_s = _ffn_last(layer, hn2, x1, wg_b, wu_b, wd_b, mod, final_g)

    new_state = jnp.transpose(jnp.stack(states, axis=0), (2, 0, 1, 3))
    return (y_p.reshape(BATCH, SEQ, D_MODEL), y_s.reshape(DEC_BATCH, DEC_SEQ, D_MODEL), new_state)
```

```python
import math

import numpy as np
import jax
import jax.numpy as jnp
from jax import lax
from jax.experimental import pallas as pl
from jax.experimental.pallas import tpu as pltpu

D_MODEL = 2048
BATCH = 32
SEQ = 256
DEPTH = 2
DEC_BATCH = 2
DEC_SEQ = 2048
GRID_W = 64
LRU_WIDTH = 1024
LRU_HEADS = 8
HEAD_DIM = 128
LRU_CONV = 4
LRU_C = 8.0
CONV_WIDTH = 1024
SHORT_CONV = 3
D_IN = 2 * LRU_WIDTH + 3 * CONV_WIDTH
D_FF = 5632
N_MOD = 6
EPS = 1e-6
POS_BASE = 10000.0

CTX_ROWS = BATCH * SEQ
LAT_ROWS = DEC_BATCH * DEC_SEQ
ROWS = CTX_ROWS + LAT_ROWS
N_SLAB = D_IN // HEAD_DIM

GROUP_ROWS = 2048
CHUNK = 256
N_CHUNK = GROUP_ROWS // CHUNK
N_GROUPS = ROWS // GROUP_ROWS
N_CTX_GROUPS = CTX_ROWS // GROUP_ROWS
SUBLANES = 8
PITCH = CHUNK + SUBLANES
XPITCH = CHUNK + 2 * SUBLANES

V7X_VMEM_LIMIT = 56 * 1024 * 1024
F32_TINY = float(np.finfo(np.float32).tiny)
GELU_C = math.sqrt(2.0 / math.pi)
GELU_C3 = 0.044715 * GELU_C

_BF = jnp.bfloat16
_F32 = jnp.float32


def _cparams(sem):
    return pltpu.CompilerParams(dimension_semantics=sem, vmem_limit_bytes=V7X_VMEM_LIMIT)


def _sigmoid(x):
    return 0.5 * jnp.tanh(0.5 * x) + 0.5


def _row_group(tile, rows_per_tile):
    return jnp.maximum((tile * rows_per_tile) // DEC_SEQ - CTX_ROWS // DEC_SEQ + 1, 0)


def _mod_row(mod_ref, j, grp):
    return mod_ref[j, pl.ds(grp, 1), :]


def _rms(x):
    return x * lax.rsqrt(jnp.mean(x * x, axis=-1, keepdims=True) + EPS)


MOD_TN = 1024


def _mod_kernel(c_ref, w_ref, b_ref, o_ref):
    c = c_ref[...]
    s = (c * _sigmoid(c)).astype(_BF)
    o_ref[...] = jnp.dot(s, w_ref[...].astype(_BF), preferred_element_type=_F32) + b_ref[...]


def _modulation(c8, w_mod, b_mod):
    per = D_MODEL // MOD_TN
    b4 = b_mod.reshape(DEPTH, N_MOD, 1, D_MODEL)
    return pl.pallas_call(
        _mod_kernel,
        grid=(DEPTH, N_MOD * per),
        in_specs=[
            pl.BlockSpec((8, D_MODEL), lambda l, n: (0, 0)),
            pl.BlockSpec((None, D_MODEL, MOD_TN), lambda l, n: (l, 0, n)),
            pl.BlockSpec((None, None, 1, MOD_TN), lambda l, n: (l, n // per, 0, n % per)),
        ],
        out_specs=pl.BlockSpec((None, None, 8, MOD_TN), lambda l, n: (l, n // per, 0, n % per)),
        out_shape=jax.ShapeDtypeStruct((DEPTH, N_MOD, 8, D_MODEL), _F32),
        compiler_params=_cparams(("arbitrary", "arbitrary")),
        name="modulation",
    )(c8, w_mod, b4)


PRE_TM = 512


def _prenorm_kernel(xp_ref, xs_ref, pos_ref, mod_ref, g_ref, hn_ref, x0_ref):
    i = pl.program_id(0)
    is_lat = i >= CTX_ROWS // PRE_TM
    grp = _row_group(i, PRE_TM)
    x = jnp.where(is_lat, xs_ref[...] + pos_ref[...], xp_ref[...])

    @pl.when(is_lat)
    def _():
        x0_ref[...] = x

    y = _rms(x) * g_ref[0:1, :]
    hn = y * (1.0 + _mod_row(mod_ref, 1, grp)) + _mod_row(mod_ref, 0, grp)
    hn_ref[...] = hn.astype(_BF)


def _prenorm(xp, xs, pos, mod, norm_g):
    nct = CTX_ROWS // PRE_TM
    npos = DEC_SEQ // PRE_TM
    return pl.pallas_call(
        _prenorm_kernel,
        grid=(ROWS // PRE_TM,),
        in_specs=[
            pl.BlockSpec((PRE_TM, D_MODEL), lambda i: (jnp.minimum(i, nct - 1), 0)),
            pl.BlockSpec((PRE_TM, D_MODEL), lambda i: (jnp.maximum(i - nct, 0), 0)),
            pl.BlockSpec((PRE_TM, D_MODEL), lambda i: (jnp.maximum(i - nct, 0) % npos, 0)),
            pl.BlockSpec((None, N_MOD, 8, D_MODEL), lambda i: (0, 0, 0, 0)),
            pl.BlockSpec((None, 2, D_MODEL), lambda i: (0, 0, 0)),
        ],
        out_specs=[
            pl.BlockSpec((PRE_TM, D_MODEL), lambda i: (i, 0)),
            pl.BlockSpec((PRE_TM, D_MODEL), lambda i: (jnp.maximum(i - nct, 0), 0)),
        ],
        out_shape=[
            jax.ShapeDtypeStruct((ROWS, D_MODEL), _BF),
            jax.ShapeDtypeStruct((LAT_ROWS, D_MODEL), _F32),
        ],
        compiler_params=_cparams(("arbitrary",)),
        name="prenorm",
    )(xp, xs, pos, mod, norm_g)


IN_TM = 1024
IN_TN = 1024


def _inproj_kernel(hn_ref, w_ref, z_ref, wb_scr):
    @pl.when(pl.program_id(1) == 0)
    def _():
        wb_scr[...] = w_ref[...].astype(_BF)

    acc = jnp.dot(hn_ref[...], wb_scr[...], preferred_element_type=_F32)
    for j in range(IN_TN // HEAD_DIM):
        z_ref[j] = acc[:, j * HEAD_DIM:(j + 1) * HEAD_DIM]


def _inproj(layer, hn, w_in):
    return pl.pallas_call(
        _inproj_kernel,
        grid=(D_IN // IN_TN, ROWS // IN_TM),
        in_specs=[
            pl.BlockSpec((IN_TM, D_MODEL), lambda n, i: (i, 0)),
            pl.BlockSpec((None, D_MODEL, IN_TN), lambda n, i: (layer, 0, n)),
        ],
        out_specs=pl.BlockSpec((IN_TN // HEAD_DIM, IN_TM, HEAD_DIM), lambda n, i: (n, i, 0)),
        out_shape=jax.ShapeDtypeStruct((N_SLAB, ROWS, HEAD_DIM), _F32),
        scratch_shapes=[pltpu.VMEM((D_MODEL, IN_TN), _BF)],
        compiler_params=_cparams(("arbitrary", "arbitrary")),
        name=f"inproj{layer}",
    )(hn, w_in)


MIX_STEPS = N_GROUPS * LRU_HEADS
CAST_STEPS = 32
CAST_SPLIT = CAST_STEPS // 2
CAST_CHUNK = 512
CAST_ROWS_IN = D_MODEL // CAST_STEPS
CAST_ROWS_FF = D_FF // CAST_STEPS


def _cast_hidden_in(w_ref, wb_ref):
    for c in range(D_FF // CAST_CHUNK):
        wb_ref[c] = w_ref[:, c * CAST_CHUNK:(c + 1) * CAST_CHUNK].astype(_BF)


def _mixer_kernel(xl_ref, yg_ref, bg_ref, cg_ref, v_ref, cw_ref, cb_ref, wa_ref, wi_ref, ba_ref, bi_ref,
                  lam_ref, scw_ref, h0_ref, wg_ref, wu_ref, wd_ref, wo_ref,
                  lru_ref, sc_ref, st_ref, wgb_ref, wub_ref, wdb_ref, wob_ref,
                  a_scr, u_scr, h_scr, pad_scr, pe_scr, hc_scr):
    g = pl.program_id(0)
    is_lat = g >= N_CTX_GROUPS

    step = g * LRU_HEADS + pl.program_id(1)

    @pl.when(step < CAST_STEPS)
    def _():
        _cast_hidden_in(wg_ref, wgb_ref)

    @pl.when(step >= MIX_STEPS - CAST_STEPS)
    def _():
        _cast_hidden_in(wu_ref, wub_ref)
        wob_ref[...] = wo_ref[...].astype(_BF)

    @pl.when((step < CAST_SPLIT) | (step >= MIX_STEPS - CAST_SPLIT))
    def _():
        wdb_ref[...] = wd_ref[...].astype(_BF)

    def fill_padded(rows):
        zeros = jnp.zeros((SUBLANES, HEAD_DIM), _F32)
        for s in range(N_CHUNK):
            base = s * XPITCH
            lo = s * CHUNK
            front = jnp.where(is_lat, rows(lo - SUBLANES, SUBLANES), 0.0) if s > 0 else zeros
            back = jnp.where(is_lat, rows(lo + CHUNK, SUBLANES), 0.0) if s + 1 < N_CHUNK else zeros
            pad_scr[base:base + SUBLANES, :] = front
            pad_scr[base + SUBLANES:base + SUBLANES + CHUNK, :] = rows(lo, CHUNK)
            pad_scr[base + SUBLANES + CHUNK:base + XPITCH, :] = back

    def tap(s, k):
        start = s * XPITCH + SUBLANES + k
        return pad_scr[start:start + CHUNK, :]

    fill_padded(lambda lo, n: xl_ref[0, lo:lo + n, :])
    wcat = (0.5 * jnp.concatenate([wa_ref[0, 0], wi_ref[0, 0], wa_ref[1, 0], wi_ref[1, 0]], axis=1)).astype(_BF)
    half_ba = 0.5 * ba_ref[...]
    half_bi = 0.5 * bi_ref[...]
    nlam = -lam_ref[...]
    softplus = jnp.maximum(nlam, 0.0) + jnp.log1p(jnp.exp(-jnp.abs(nlam)))
    half_rate = (-0.5 * LRU_C) * softplus

    for s in range(N_CHUNK):
        xc = (cw_ref[2:3, :] * xl_ref[0, s * CHUNK:(s + 1) * CHUNK, :] + cb_ref[...]
              + cw_ref[0:1, :] * tap(s, -2) + cw_ref[1:2, :] * tap(s, -1) + cw_ref[3:4, :] * tap(s, 1))
        gates = jnp.dot(xc.astype(_BF), wcat, preferred_element_type=_F32)
        for d in range(2):
            half_ra = gates[:, d * 256:d * 256 + HEAD_DIM] + half_ba[d:d + 1, :]
            half_ia = gates[:, d * 256 + HEAD_DIM:(d + 1) * 256] + half_bi[d:d + 1, :]
            log_a = half_rate[d:d + 1, :] * (1.0 + jnp.tanh(half_ra))
            a = jnp.exp(log_a)
            th = jnp.tanh(log_a)
            q = (-0.5 * th) / (1.0 - th)
            root = jnp.maximum(q, 0.0) * lax.rsqrt(jnp.maximum(q, F32_TINY))
            a_scr[d, s * PITCH:s * PITCH + CHUNK, :] = a
            u_scr[d, s * PITCH:s * PITCH + CHUNK, :] = root * ((1.0 + jnp.tanh(half_ia)) * xc)

    def strided(t):
        return pl.ds(t, N_CHUNK, stride=PITCH)

    @pl.when(jnp.logical_not(is_lat))
    def _():
        hc_scr[...] = jnp.zeros_like(hc_scr)

    @pl.when(is_lat)
    def _():
        def body(t, c):
            pf, ef, pb, eb = c
            af = a_scr[0, strided(t), :]
            ef = af * ef + u_scr[0, strided(t), :]
            pf = pf * af
            tb = CHUNK - 1 - t
            ab = a_scr[1, strided(tb), :]
            eb = ab * eb + u_scr[1, strided(tb), :]
            pb = pb * ab
            return pf, ef, pb, eb

        one = jnp.ones((N_CHUNK, HEAD_DIM), _F32)
        zero = jnp.zeros((N_CHUNK, HEAD_DIM), _F32)
        pf, ef, pb, eb = lax.fori_loop(0, CHUNK, body, (one, zero, one, zero), unroll=8)
        pe_scr[0] = pf
        pe_scr[1] = ef
        pe_scr[2] = pb
        pe_scr[3] = eb
        seq = g - N_CTX_GROUPS
        h = h0_ref[0, pl.ds(seq, 1), :]
        for c in range(N_CHUNK):
            hc_scr[0, c:c + 1, :] = h
            h = pe_scr[0, c:c + 1, :] * h + pe_scr[1, c:c + 1, :]
        h = h0_ref[1, pl.ds(seq, 1), :]
        for c in reversed(range(N_CHUNK)):
            hc_scr[1, c:c + 1, :] = h
            h = pe_scr[2, c:c + 1, :] * h + pe_scr[3, c:c + 1, :]

    def advance_two(d, h, first, second):
        a1 = a_scr[d, strided(first), :]
        u1 = u_scr[d, strided(first), :]
        a2 = a_scr[d, strided(second), :]
        u2 = u_scr[d, strided(second), :]
        h_scr[d, strided(first), :] = a1 * h + u1
        h = (a1 * a2) * h + (a2 * u1 + u2)
        h_scr[d, strided(second), :] = h
        return h

    hf = hc_scr[0]
    hb = hc_scr[1]
    for t in range(0, CHUNK, 2):
        hf = advance_two(0, hf, t, t + 1)
        hb = advance_two(1, hb, CHUNK - 1 - t, CHUNK - 2 - t)
    st_ref[0] = hf
    st_ref[1] = hb

    for s in range(N_CHUNK):
        y = yg_ref[0, s * CHUNK:(s + 1) * CHUNK, :]
        gelu = (0.5 * y) * (1.0 + jnp.tanh(y * (GELU_C + GELU_C3 * (y * y))))
        o = h_scr[0, s * PITCH:s * PITCH + CHUNK, :] + h_scr[1, s * PITCH:s * PITCH + CHUNK, :]
        lru_ref[s * CHUNK:(s + 1) * CHUNK, :] = (o * gelu).astype(_BF)

    fill_padded(lambda lo, n: cg_ref[0, lo:lo + n, :] * v_ref[0, lo:lo + n, :])
    for s in range(N_CHUNK):
        conv = scw_ref[1:2, :] * tap(s, 0) + scw_ref[0:1, :] * tap(s, -1) + scw_ref[2:3, :] * tap(s, 1)
        sc_ref[s * CHUNK:(s + 1) * CHUNK, :] = (bg_ref[0, s * CHUNK:(s + 1) * CHUNK, :] * conv).astype(_BF)


def _mixer(layer, z, cw, cb, w_a, w_i, b_a, b_i, lam, scw, h0, w_gate, w_up, w_down, w_out):
    assert 2 * CAST_SPLIT == CAST_STEPS and CAST_STEPS + CAST_SPLIT == MIX_STEPS

    def zspec(off):
        return pl.BlockSpec((1, GROUP_ROWS, HEAD_DIM), lambda g, h: (off + h, g, 0))

    def vec(rows):
        return pl.BlockSpec((None, rows, HEAD_DIM), lambda g, h: (layer, 0, h))

    step = lambda g, h: g * LRU_HEADS + h
    early = lambda g, h: (jnp.minimum(step(g, h), CAST_STEPS - 1), 0)
    late = lambda g, h: (jnp.maximum(step(g, h) - (MIX_STEPS - CAST_STEPS), 0), 0)
    ends = lambda g, h: (jnp.where(step(g, h) < CAST_SPLIT, step(g, h),
                                   jnp.maximum(step(g, h) - CAST_SPLIT, CAST_SPLIT - 1)), 0)
    hidden = lambda rows: pl.BlockSpec((None, CAST_ROWS_IN, D_FF), lambda g, h: (layer,) + rows(g, h))
    hidden_bf = lambda rows: pl.BlockSpec((D_FF // CAST_CHUNK, CAST_ROWS_IN, CAST_CHUNK),
                                          lambda g, h: (0,) + rows(g, h))
    hidden_b = jax.ShapeDtypeStruct((D_FF // CAST_CHUNK, D_MODEL, CAST_CHUNK), _BF)
    wspec = pl.BlockSpec((None, 2, 1, HEAD_DIM, HEAD_DIM), lambda g, h: (layer, 0, h, 0, 0))
    scan_buf = pltpu.VMEM((2, N_CHUNK * PITCH, HEAD_DIM), _F32)
    return pl.pallas_call(
        _mixer_kernel,
        grid=(N_GROUPS, LRU_HEADS),
        in_specs=[
            zspec(0), zspec(8), zspec(16), zspec(24), zspec(32),
            vec(LRU_CONV), vec(1), wspec, wspec, vec(2), vec(2), vec(2), vec(SHORT_CONV),
            pl.BlockSpec((2, 8, HEAD_DIM), lambda g, h: (0, 0, h)),
            hidden(early), hidden(late),
            pl.BlockSpec((None, CAST_ROWS_FF, D_MODEL), lambda g, h: (layer,) + ends(g, h)),
            pl.BlockSpec((None, CAST_ROWS_IN, D_MODEL), lambda g, h: (layer,) + late(g, h)),
        ],
        out_specs=[
            pl.BlockSpec((GROUP_ROWS, HEAD_DIM), lambda g, h: (g, h)),
            pl.BlockSpec((GROUP_ROWS, HEAD_DIM), lambda g, h: (g, h)),
            pl.BlockSpec((2, N_CHUNK, HEAD_DIM), lambda g, h: (0, g, h)),
            hidden_bf(early), hidden_bf(late),
            pl.BlockSpec((CAST_ROWS_FF, D_MODEL), ends),
            pl.BlockSpec((CAST_ROWS_IN, D_MODEL), late),
        ],
        out_shape=[
            jax.ShapeDtypeStruct((ROWS, LRU_WIDTH), _BF),
            jax.ShapeDtypeStruct((ROWS, CONV_WIDTH), _BF),
            jax.ShapeDtypeStruct((2, N_GROUPS * N_CHUNK, LRU_WIDTH), _F32),
            hidden_b, hidden_b,
            jax.ShapeDtypeStruct((D_FF, D_MODEL), _BF),
            jax.ShapeDtypeStruct((D_MODEL, D_MODEL), _BF),
        ],
        scratch_shapes=[
            scan_buf, scan_buf, scan_buf,
            pltpu.VMEM((N_CHUNK * XPITCH, HEAD_DIM), _F32),
            pltpu.VMEM((4, N_CHUNK, HEAD_DIM), _F32),
            pltpu.VMEM((2, N_CHUNK, HEAD_DIM), _F32),
        ],
        compiler_params=_cparams(("arbitrary", "arbitrary")),
        name=f"mixer{layer}",
    )(z, z, z, z, z, cw, cb, w_a, w_i, b_a, b_i, lam, scw, h0, w_gate, w_up, w_down, w_out)


OUT_TM = 512


def _outproj_body(lru_ref, sc_ref, residual, w_ref, mod_ref, g_ref, x1_ref, hn2_ref):
    grp = _row_group(pl.program_id(0), OUT_TM)
    gate = _mod_row(mod_ref, 2, grp)
    scale = g_ref[1:2, :] * (1.0 + _mod_row(mod_ref, 4, grp))
    shift = _mod_row(mod_ref, 3, grp)
    for r0 in range(0, OUT_TM, OUT_TM // 2):
        rows = slice(r0, r0 + OUT_TM // 2)
        m = jnp.dot(lru_ref[rows, :], w_ref[0:LRU_WIDTH, :], preferred_element_type=_F32)
        m = m + jnp.dot(sc_ref[rows, :], w_ref[LRU_WIDTH:, :], preferred_element_type=_F32)
        x1 = residual(rows) + gate * m
        x1_ref[rows, :] = x1
        hn2_ref[rows, :] = (_rms(x1) * scale + shift).astype(_BF)


def _outproj_kernel(lru_ref, sc_ref, x_ref, w_ref, mod_ref, g_ref, x1_ref, hn2_ref):
    _outproj_body(lru_ref, sc_ref, lambda rows: x_ref[rows, :], w_ref, mod_ref, g_ref, x1_ref, hn2_ref)


def _outproj_split_kernel(lru_ref, sc_ref, xc_ref, xl_ref, w_ref, mod_ref, g_ref, x1_ref, hn2_ref):
    is_lat = pl.program_id(0) >= CTX_ROWS // OUT_TM
    residual = lambda rows: jnp.where(is_lat, xl_ref[rows, :], xc_ref[rows, :])
    _outproj_body(lru_ref, sc_ref, residual, w_ref, mod_ref, g_ref, x1_ref, hn2_ref)


def _outproj(layer, lru, sc, x, w_out_b, mod, norm_g):
    row = lambda i: (i, 0)
    nct = CTX_ROWS // OUT_TM
    if isinstance(x, tuple):
        body = _outproj_split_kernel
        x_specs = [pl.BlockSpec((OUT_TM, D_MODEL), lambda i: (jnp.minimum(i, nct - 1), 0)),
                   pl.BlockSpec((OUT_TM, D_MODEL), lambda i: (jnp.maximum(i - nct, 0), 0))]
    else:
        body = _outproj_kernel
        x_specs = [pl.BlockSpec((OUT_TM, D_MODEL), row)]
        x = (x,)
    return pl.pallas_call(
        body,
        grid=(ROWS // OUT_TM,),
        in_specs=[pl.BlockSpec((OUT_TM, LRU_WIDTH), row), pl.BlockSpec((OUT_TM, CONV_WIDTH), row)] + x_specs + [
            pl.BlockSpec((D_MODEL, D_MODEL), lambda i: (0, 0), pipeline_mode=pl.Buffered(1)),
            pl.BlockSpec((None, N_MOD, 8, D_MODEL), lambda i: (layer, 0, 0, 0)),
            pl.BlockSpec((None, 2, D_MODEL), lambda i: (layer, 0, 0)),
        ],
        out_specs=[pl.BlockSpec((OUT_TM, D_MODEL), row), pl.BlockSpec((OUT_TM, D_MODEL), row)],
        out_shape=[
            jax.ShapeDtypeStruct((ROWS, D_MODEL), _F32),
            jax.ShapeDtypeStruct((ROWS, D_MODEL), _BF),
        ],
        compiler_params=_cparams(("arbitrary",)),
        name=f"outproj{layer}",
    )(lru, sc, *x, w_out_b, mod, norm_g)


FFN_TM = 512
FFN_TF = CAST_CHUNK
FFN_STEPS = D_FF // FFN_TF
FFN_TILES = ROWS // FFN_TM
FFN_CTX_TILES = CTX_ROWS // FFN_TM


def _finished_tile(i, f):
    final = (i == FFN_TILES - 1) & (f == FFN_STEPS - 1)
    return jnp.where(final, FFN_TILES - 1, jnp.maximum(i - 1, 0))


def _ffn_pipeline(hn_ref, wg_ref, wu_ref, wd_ref, acc_a, acc_b, finish, store):
    i = pl.program_id(0)
    f = pl.program_id(1)

    @pl.when((i == 0) & (f == 0))
    def _():
        acc_b[...] = jnp.zeros_like(acc_b)

    def chunk():
        h = hn_ref[...]
        gate = jnp.dot(h, wg_ref[...], preferred_element_type=_F32)
        up = jnp.dot(h, wu_ref[...], preferred_element_type=_F32)
        act = (gate * _sigmoid(gate) * up).astype(_BF)
        return jnp.dot(act, wd_ref[...], preferred_element_type=_F32)

    def variant(cur, prev):
        @pl.when(f == 0)
        def _():
            tile = jnp.maximum(i - 1, 0)
            values = finish(prev, tile)
            cur[...] = chunk()
            store(values, tile)

        @pl.when(f > 0)
        def _():
            cur[...] += chunk()

        @pl.when((f == FFN_STEPS - 1) & (i == FFN_TILES - 1))
        def _():
            tile = jnp.int32(FFN_TILES - 1)
            store(finish(cur, tile), tile)

    @pl.when(i % 2 == 0)
    def _():
        variant(acc_a, acc_b)

    @pl.when(i % 2 == 1)
    def _():
        variant(acc_b, acc_a)


def _ffn_mid_kernel(hn_ref, x1_ref, wg_ref, wu_ref, wd_ref, mod_ref, modn_ref, gn_ref, x2_ref, hnn_ref,
                    acc_a, acc_b):
    def finish(acc_ref, tile):
        grp = _row_group(tile, FFN_TM)
        x2 = x1_ref[...] + _mod_row(mod_ref, 5, grp) * acc_ref[...]
        y = _rms(x2) * gn_ref[0:1, :]
        return x2, (y * (1.0 + _mod_row(modn_ref, 1, grp)) + _mod_row(modn_ref, 0, grp)).astype(_BF)

    def store(values, tile):
        x2_ref[...], hnn_ref[...] = values

    _ffn_pipeline(hn_ref, wg_ref, wu_ref, wd_ref, acc_a, acc_b, finish, store)


def _ffn_last_kernel(hn_ref, x1_ref, wg_ref, wu_ref, wd_ref, mod_ref, fg_ref, yp_ref, ys_ref, acc_a, acc_b):
    def finish(acc_ref, tile):
        grp = _row_group(tile, FFN_TM)
        x2 = x1_ref[...] + _mod_row(mod_ref, 5, grp) * acc_ref[...]
        return _rms(x2) * fg_ref[...]

    def store(y, tile):
        @pl.when(tile < FFN_CTX_TILES)
        def _():
            yp_ref[...] = y

        @pl.when(tile >= FFN_CTX_TILES)
        def _():
            ys_ref[...] = y

    _ffn_pipeline(hn_ref, wg_ref, wu_ref, wd_ref, acc_a, acc_b, finish, store)


def _ffn_specs(layer):
    residual = lambda i, f: (jnp.where(f == FFN_STEPS - 1, i, jnp.maximum(i - 1, 0)), 0)
    return [
        pl.BlockSpec((FFN_TM, D_MODEL), lambda i, f: (i, 0)),
        pl.BlockSpec((FFN_TM, D_MODEL), residual),
        pl.BlockSpec((None, D_MODEL, FFN_TF), lambda i, f: (f, 0, 0)),
        pl.BlockSpec((None, D_MODEL, FFN_TF), lambda i, f: (f, 0, 0)),
        pl.BlockSpec((FFN_TF, D_MODEL), lambda i, f: (f, 0)),
        pl.BlockSpec((None, N_MOD, 8, D_MODEL), lambda i, f: (layer, 0, 0, 0)),
    ]


def _ffn_mid(layer, hn2, x1, wg_b, wu_b, wd_b, mod, norm_g):
    row = lambda i, f: (_finished_tile(i, f), 0)
    return pl.pallas_call(
        _ffn_mid_kernel,
        grid=(FFN_TILES, FFN_STEPS),
        in_specs=_ffn_specs(layer) + [
            pl.BlockSpec((None, N_MOD, 8, D_MODEL), lambda i, f: (layer + 1, 0, 0, 0)),
            pl.BlockSpec((None, 2, D_MODEL), lambda i, f: (layer + 1, 0, 0)),
        ],
        out_specs=[pl.BlockSpec((FFN_TM, D_MODEL), row), pl.BlockSpec((FFN_TM, D_MODEL), row)],
        out_shape=[
            jax.ShapeDtypeStruct((ROWS, D_MODEL), _F32),
            jax.ShapeDtypeStruct((ROWS, D_MODEL), _BF),
        ],
        scratch_shapes=[pltpu.VMEM((FFN_TM, D_MODEL), _F32)] * 2,
        compiler_params=_cparams(("arbitrary", "arbitrary")),
        name=f"ffn{layer}",
    )(hn2, x1, wg_b, wu_b, wd_b, mod, mod, norm_g)


def _ffn_last(layer, hn2, x1, wg_b, wu_b, wd_b, mod, final_g):
    return pl.pallas_call(
        _ffn_last_kernel,
        grid=(FFN_TILES, FFN_STEPS),
        in_specs=_ffn_specs(layer) + [pl.BlockSpec((1, D_MODEL), lambda i, f: (0, 0))],
        out_specs=[
            pl.BlockSpec((FFN_TM, D_MODEL), lambda i, f: (jnp.minimum(_finished_tile(i, f), FFN_CTX_TILES - 1), 0)),
            pl.BlockSpec((FFN_TM, D_MODEL), lambda i, f: (jnp.maximum(_finished_tile(i, f) - FFN_CTX_TILES, 0), 0)),
        ],
        out_shape=[
            jax.ShapeDtypeStruct((CTX_ROWS, D_MODEL), _F32),
            jax.ShapeDtypeStruct((LAT_ROWS, D_MODEL), _F32),
        ],
        scratch_shapes=[pltpu.VMEM((FFN_TM, D_MODEL), _F32)] * 2,
        compiler_params=_cparams(("arbitrary", "arbitrary")),
        name=f"ffn{layer}",
    )(hn2, x1, wg_b, wu_b, wd_b, mod, final_g.reshape(1, D_MODEL))


def _pos_table():
    rows = DEC_SEQ // GRID_W
    r = np.repeat(np.arange(rows, dtype=np.float32), GRID_W)
    col = np.tile(np.arange(GRID_W, dtype=np.float32), rows)
    n_freq = D_MODEL // 4
    freqs = (1.0 / (np.float32(POS_BASE) ** (np.arange(n_freq, dtype=np.float32) / np.float32(n_freq)))).astype(np.float32)
    er = r[:, None] * freqs
    ec = col[:, None] * freqs
    return np.concatenate([np.sin(er), np.cos(er), np.sin(ec), np.cos(ec)], axis=-1).astype(np.float32)


def kernel(x_prompt, x_sample, state_rglru, c, c_ctx, w_mod, b_mod, norm_g, w_in, lru_conv_w, lru_conv_b,
           lru_w_a, lru_b_a, lru_w_i, lru_b_i, lru_lambda, sc_conv_w, w_out, w_gate, w_up, w_down, final_g):
    c8 = jnp.concatenate([c_ctx[None, :], c, jnp.zeros((8 - 1 - DEC_BATCH, D_MODEL), _F32)], axis=0)
    mod = _modulation(c8, w_mod, b_mod)

    pos = jnp.asarray(_pos_table())
    xp = x_prompt.reshape(CTX_ROWS, D_MODEL)
    hn, x_lat = _prenorm(xp, x_sample.reshape(LAT_ROWS, D_MODEL), pos, mod, norm_g)
    x = (xp, x_lat)

    cb = lru_conv_b.reshape(DEPTH, 1, LRU_WIDTH)
    b_a = lru_b_a.reshape(DEPTH, 2, LRU_WIDTH)
    b_i = lru_b_i.reshape(DEPTH, 2, LRU_WIDTH)
    h0 = jnp.pad(jnp.transpose(state_rglru, (1, 2, 0, 3)), ((0, 0), (0, 0), (0, 8 - DEC_BATCH), (0, 0)))

    states = []
    for layer in range(DEPTH):
        z = _inproj(layer, hn, w_in)
        lru, sc, st, wg_b, wu_b, wd_b, wo_b = _mixer(layer, z, lru_conv_w, cb, lru_w_a, lru_w_i, b_a, b_i, lru_lambda,
                                                     sc_conv_w, h0[layer], w_gate, w_up, w_down, w_out)
        states.append(st[:, :BATCH, :])
        x1, hn2 = _outproj(layer, lru, sc, x, wo_b, mod, norm_g)
        if layer + 1 < DEPTH:
            x, hn = _ffn_mid(layer, hn2, x1, wg_b, wu_b, wd_b, mod, norm_g)
        else:
            y_p, y_s = _ffn_last(layer, hn2, x1, wg_b, wu_b, wd_b, mod, final_g)

    new_state = jnp.transpose(jnp.stack(states, axis=0), (2, 0, 1, 3))
    return (y_p.reshape(BATCH, SEQ, D_MODEL), y_s.reshape(DEC_BATCH, DEC_SEQ, D_MODEL), new_state)
```

```python
import math

import numpy as np
import jax
import jax.numpy as jnp
from jax import lax
from jax.experimental import pallas as pl
from jax.experimental.pallas import tpu as pltpu

D_MODEL = 2048
BATCH = 32
SEQ = 256
DEPTH = 2
DEC_BATCH = 2
DEC_SEQ = 2048
GRID_W = 64
LRU_WIDTH = 1024
LRU_HEADS = 8
HEAD_DIM = 128
LRU_CONV = 4
LRU_C = 8.0
CONV_WIDTH = 1024
SHORT_CONV = 3
D_IN = 2 * LRU_WIDTH + 3 * CONV_WIDTH
D_FF = 5632
N_MOD = 6
EPS = 1e-6
POS_BASE = 10000.0

CTX_ROWS = BATCH * SEQ
LAT_ROWS = DEC_BATCH * DEC_SEQ
ROWS = CTX_ROWS + LAT_ROWS
N_SLAB = D_IN // HEAD_DIM

GROUP_ROWS = 2048
CHUNK = 256
N_CHUNK = GROUP_ROWS // CHUNK
N_GROUPS = ROWS // GROUP_ROWS
N_CTX_GROUPS = CTX_ROWS // GROUP_ROWS
SUBLANES = 8
PITCH = CHUNK + SUBLANES
XPITCH = CHUNK + 2 * SUBLANES

V7X_VMEM_LIMIT = 56 * 1024 * 1024
F32_TINY = float(np.finfo(np.float32).tiny)
GELU_C = math.sqrt(2.0 / math.pi)
GELU_C3 = 0.044715 * GELU_C

_BF = jnp.bfloat16
_F32 = jnp.float32


def _cparams(sem):
    return pltpu.CompilerParams(dimension_semantics=sem, vmem_limit_bytes=V7X_VMEM_LIMIT)


def _sigmoid(x):
    return 0.5 * jnp.tanh(0.5 * x) + 0.5


def _row_group(tile, rows_per_tile):
    return jnp.maximum((tile * rows_per_tile) // DEC_SEQ - CTX_ROWS // DEC_SEQ + 1, 0)


def _mod_row(mod_ref, j, grp):
    return mod_ref[j, pl.ds(grp, 1), :]


def _rms(x):
    return x * lax.rsqrt(jnp.mean(x * x, axis=-1, keepdims=True) + EPS)


MOD_TN = 1024


def _mod_kernel(c_ref, w_ref, b_ref, o_ref):
    c = c_ref[...]
    s = (c * _sigmoid(c)).astype(_BF)
    o_ref[...] = jnp.dot(s, w_ref[...].astype(_BF), preferred_element_type=_F32) + b_ref[...]


def _modulation(c8, w_mod, b_mod):
    per = D_MODEL // MOD_TN
    b4 = b_mod.reshape(DEPTH, N_MOD, 1, D_MODEL)
    return pl.pallas_call(
        _mod_kernel,
        grid=(DEPTH, N_MOD * per),
        in_specs=[
            pl.BlockSpec((8, D_MODEL), lambda l, n: (0, 0)),
            pl.BlockSpec((None, D_MODEL, MOD_TN), lambda l, n: (l, 0, n)),
            pl.BlockSpec((None, None, 1, MOD_TN), lambda l, n: (l, n // per, 0, n % per)),
        ],
        out_specs=pl.BlockSpec((None, None, 8, MOD_TN), lambda l, n: (l, n // per, 0, n % per)),
        out_shape=jax.ShapeDtypeStruct((DEPTH, N_MOD, 8, D_MODEL), _F32),
        compiler_params=_cparams(("arbitrary", "arbitrary")),
        name="modulation",
    )(c8, w_mod, b4)


PRE_TM = 512


def _prenorm_kernel(xp_ref, xs_ref, pos_ref, mod_ref, g_ref, hn_ref, x0_ref):
    i = pl.program_id(0)
    is_lat = i >= CTX_ROWS // PRE_TM
    grp = _row_group(i, PRE_TM)
    x = jnp.where(is_lat, xs_ref[...] + pos_ref[...], xp_ref[...])

    @pl.when(is_lat)
    def _():
        x0_ref[...] = x

    y = _rms(x) * g_ref[0:1, :]
    hn = y * (1.0 + _mod_row(mod_ref, 1, grp)) + _mod_row(mod_ref, 0, grp)
    hn_ref[...] = hn.astype(_BF)


def _prenorm(xp, xs, pos, mod, norm_g):
    nct = CTX_ROWS // PRE_TM
    npos = DEC_SEQ // PRE_TM
    return pl.pallas_call(
        _prenorm_kernel,
        grid=(ROWS // PRE_TM,),
        in_specs=[
            pl.BlockSpec((PRE_TM, D_MODEL), lambda i: (jnp.minimum(i, nct - 1), 0)),
            pl.BlockSpec((PRE_TM, D_MODEL), lambda i: (jnp.maximum(i - nct, 0), 0)),
            pl.BlockSpec((PRE_TM, D_MODEL), lambda i: (jnp.maximum(i - nct, 0) % npos, 0)),
            pl.BlockSpec((None, N_MOD, 8, D_MODEL), lambda i: (0, 0, 0, 0)),
            pl.BlockSpec((None, 2, D_MODEL), lambda i: (0, 0, 0)),
        ],
        out_specs=[
            pl.BlockSpec((PRE_TM, D_MODEL), lambda i: (i, 0)),
            pl.BlockSpec((PRE_TM, D_MODEL), lambda i: (jnp.maximum(i - nct, 0), 0)),
        ],
        out_shape=[
            jax.ShapeDtypeStruct((ROWS, D_MODEL), _BF),
            jax.ShapeDtypeStruct((LAT_ROWS, D_MODEL), _F32),
        ],
        compiler_params=_cparams(("arbitrary",)),
        name="prenorm",
    )(xp, xs, pos, mod, norm_g)


IN_TM = 1024
IN_TN = 1024


def _inproj_kernel(hn_ref, w_ref, z_ref, wb_scr):
    @pl.when(pl.program_id(1) == 0)
    def _():
        wb_scr[...] = w_ref[...].astype(_BF)

    acc = jnp.dot(hn_ref[...], wb_scr[...], preferred_element_type=_F32)
    for j in range(IN_TN // HEAD_DIM):
        z_ref[j] = acc[:, j * HEAD_DIM:(j + 1) * HEAD_DIM]


def _inproj(layer, hn, w_in):
    return pl.pallas_call(
        _inproj_kernel,
        grid=(D_IN // IN_TN, ROWS // IN_TM),
        in_specs=[
            pl.BlockSpec((IN_TM, D_MODEL), lambda n, i: (i, 0)),
            pl.BlockSpec((None, D_MODEL, IN_TN), lambda n, i: (layer, 0, n)),
        ],
        out_specs=pl.BlockSpec((IN_TN // HEAD_DIM, IN_TM, HEAD_DIM), lambda n, i: (n, i, 0)),
        out_shape=jax.ShapeDtypeStruct((N_SLAB, ROWS, HEAD_DIM), _F32),
        scratch_shapes=[pltpu.VMEM((D_MODEL, IN_TN), _BF)],
        compiler_params=_cparams(("arbitrary", "arbitrary")),
        name=f"inproj{layer}",
    )(hn, w_in)


MIX_STEPS = N_GROUPS * LRU_HEADS
CAST_STEPS = 32
CAST_SPLIT = CAST_STEPS // 2
CAST_CHUNK = 512
CAST_ROWS_IN = D_MODEL // CAST_STEPS
CAST_ROWS_FF = D_FF // CAST_STEPS


def _cast_hidden_in(w_ref, wb_ref):
    for c in range(D_FF // CAST_CHUNK):
        wb_ref[c] = w_ref[:, c * CAST_CHUNK:(c + 1) * CAST_CHUNK].astype(_BF)


def _mixer_kernel(xl_ref, yg_ref, bg_ref, cg_ref, v_ref, cw_ref, cb_ref, wa_ref, wi_ref, ba_ref, bi_ref,
                  lam_ref, scw_ref, h0_ref, wg_ref, wu_ref, wd_ref, wo_ref,
                  lru_ref, sc_ref, st_ref, wgb_ref, wub_ref, wdb_ref, wob_ref,
                  a_scr, u_scr, h_scr, pad_scr, pe_scr, hc_scr):
    g = pl.program_id(0)
    is_lat = g >= N_CTX_GROUPS

    step = g * LRU_HEADS + pl.program_id(1)

    @pl.when(step < CAST_STEPS)
    def _():
        _cast_hidden_in(wg_ref, wgb_ref)

    @pl.when(step >= MIX_STEPS - CAST_STEPS)
    def _():
        _cast_hidden_in(wu_ref, wub_ref)
        wob_ref[...] = wo_ref[...].astype(_BF)

    @pl.when((step < CAST_SPLIT) | (step >= MIX_STEPS - CAST_SPLIT))
    def _():
        wdb_ref[...] = wd_ref[...].astype(_BF)

    def fill_padded(rows):
        zeros = jnp.zeros((SUBLANES, HEAD_DIM), _F32)
        for s in range(N_CHUNK):
            base = s * XPITCH
            lo = s * CHUNK
            front = jnp.where(is_lat, rows(lo - SUBLANES, SUBLANES), 0.0) if s > 0 else zeros
            back = jnp.where(is_lat, rows(lo + CHUNK, SUBLANES), 0.0) if s + 1 < N_CHUNK else zeros
            pad_scr[base:base + SUBLANES, :] = front
            pad_scr[base + SUBLANES:base + SUBLANES + CHUNK, :] = rows(lo, CHUNK)
            pad_scr[base + SUBLANES + CHUNK:base + XPITCH, :] = back

    def tap(s, k):
        start = s * XPITCH + SUBLANES + k
        return pad_scr[start:start + CHUNK, :]

    fill_padded(lambda lo, n: xl_ref[0, lo:lo + n, :])
    wcat = (0.5 * jnp.concatenate([wa_ref[0, 0], wi_ref[0, 0], wa_ref[1, 0], wi_ref[1, 0]], axis=1)).astype(_BF)
    half_ba = 0.5 * ba_ref[...]
    half_bi = 0.5 * bi_ref[...]
    nlam = -lam_ref[...]
    softplus = jnp.maximum(nlam, 0.0) + jnp.log1p(jnp.exp(-jnp.abs(nlam)))
    half_rate = (-0.5 * LRU_C) * softplus

    for s in range(N_CHUNK):
        xc = (cw_ref[2:3, :] * xl_ref[0, s * CHUNK:(s + 1) * CHUNK, :] + cb_ref[...]
              + cw_ref[0:1, :] * tap(s, -2) + cw_ref[1:2, :] * tap(s, -1) + cw_ref[3:4, :] * tap(s, 1))
        gates = jnp.dot(xc.astype(_BF), wcat, preferred_element_type=_F32)
        for d in range(2):
            half_ra = gates[:, d * 256:d * 256 + HEAD_DIM] + half_ba[d:d + 1, :]
            half_ia = gates[:, d * 256 + HEAD_DIM:(d + 1) * 256] + half_bi[d:d + 1, :]
            log_a = half_rate[d:d + 1, :] * (1.0 + jnp.tanh(half_ra))
            a = jnp.exp(log_a)
            th = jnp.tanh(log_a)
            q = (-0.5 * th) / (1.0 - th)
            root = jnp.maximum(q, 0.0) * lax.rsqrt(jnp.maximum(q, F32_TINY))
            a_scr[d, s * PITCH:s * PITCH + CHUNK, :] = a
            u_scr[d, s * PITCH:s * PITCH + CHUNK, :] = root * ((1.0 + jnp.tanh(half_ia)) * xc)

    def strided(t):
        return pl.ds(t, N_CHUNK, stride=PITCH)

    @pl.when(jnp.logical_not(is_lat))
    def _():
        hc_scr[...] = jnp.zeros_like(hc_scr)

    @pl.when(is_lat)
    def _():
        def body(t, c):
            pf, ef, pb, eb = c
            af = a_scr[0, strided(t), :]
            ef = af * ef + u_scr[0, strided(t), :]
            pf = pf * af
            tb = CHUNK - 1 - t
            ab = a_scr[1, strided(tb), :]
            eb = ab * eb + u_scr[1, strided(tb), :]
            pb = pb * ab
            return pf, ef, pb, eb

        one = jnp.ones((N_CHUNK, HEAD_DIM), _F32)
        zero = jnp.zeros((N_CHUNK, HEAD_DIM), _F32)
        pf, ef, pb, eb = lax.fori_loop(0, CHUNK, body, (one, zero, one, zero), unroll=8)
        pe_scr[0] = pf
        pe_scr[1] = ef
        pe_scr[2] = pb
        pe_scr[3] = eb
        seq = g - N_CTX_GROUPS
        h = h0_ref[0, pl.ds(seq, 1), :]
        for c in range(N_CHUNK):
            hc_scr[0, c:c + 1, :] = h
            h = pe_scr[0, c:c + 1, :] * h + pe_scr[1, c:c + 1, :]
        h = h0_ref[1, pl.ds(seq, 1), :]
        for c in reversed(range(N_CHUNK)):
            hc_scr[1, c:c + 1, :] = h
            h = pe_scr[2, c:c + 1, :] * h + pe_scr[3, c:c + 1, :]

    def advance_two(d, h, first, second):
        a1 = a_scr[d, strided(first), :]
        u1 = u_scr[d, strided(first), :]
        a2 = a_scr[d, strided(second), :]
        u2 = u_scr[d, strided(second), :]
        h_scr[d, strided(first), :] = a1 * h + u1
        h = (a1 * a2) * h + (a2 * u1 + u2)
        h_scr[d, strided(second), :] = h
        return h

    hf = hc_scr[0]
    hb = hc_scr[1]
    for t in range(0, CHUNK, 2):
        hf = advance_two(0, hf, t, t + 1)
        hb = advance_two(1, hb, CHUNK - 1 - t, CHUNK - 2 - t)
    st_ref[0] = hf
    st_ref[1] = hb

    for s in range(N_CHUNK):
        y = yg_ref[0, s * CHUNK:(s + 1) * CHUNK, :]
        gelu = (0.5 * y) * (1.0 + jnp.tanh(y * (GELU_C + GELU_C3 * (y * y))))
        o = h_scr[0, s * PITCH:s * PITCH + CHUNK, :] + h_scr[1, s * PITCH:s * PITCH + CHUNK, :]
        lru_ref[s * CHUNK:(s + 1) * CHUNK, :] = (o * gelu).astype(_BF)

    fill_padded(lambda lo, n: cg_ref[0, lo:lo + n, :] * v_ref[0, lo:lo + n, :])
    for s in range(N_CHUNK):
        conv = scw_ref[1:2, :] * tap(s, 0) + scw_ref[0:1, :] * tap(s, -1) + scw_ref[2:3, :] * tap(s, 1)
        sc_ref[s * CHUNK:(s + 1) * CHUNK, :] = (bg_ref[0, s * CHUNK:(s + 1) * CHUNK, :] * conv).astype(_BF)


def _mixer(layer, z, cw, cb, w_a, w_i, b_a, b_i, lam, scw, h0, w_gate, w_up, w_down, w_out):
    assert 2 * CAST_SPLIT == CAST_STEPS and CAST_STEPS + CAST_SPLIT == MIX_STEPS

    def zspec(off):
        return pl.BlockSpec((1, GROUP_ROWS, HEAD_DIM), lambda g, h: (off + h, g, 0))

    def vec(rows):
        return pl.BlockSpec((None, rows, HEAD_DIM), lambda g, h: (layer, 0, h))

    step = lambda g, h: g * LRU_HEADS + h
    early = lambda g, h: (jnp.minimum(step(g, h), CAST_STEPS - 1), 0)
    late = lambda g, h: (jnp.maximum(step(g, h) - (MIX_STEPS - CAST_STEPS), 0), 0)
    ends = lambda g, h: (jnp.where(step(g, h) < CAST_SPLIT, step(g, h),
                                   jnp.maximum(step(g, h) - CAST_SPLIT, CAST_SPLIT - 1)), 0)
    hidden = lambda rows: pl.BlockSpec((None, CAST_ROWS_IN, D_FF), lambda g, h: (layer,) + rows(g, h))
    hidden_bf = lambda rows: pl.BlockSpec((D_FF // CAST_CHUNK, CAST_ROWS_IN, CAST_CHUNK),
                                          lambda g, h: (0,) + rows(g, h))
    hidden_b = jax.ShapeDtypeStruct((D_FF // CAST_CHUNK, D_MODEL, CAST_CHUNK), _BF)
    wspec = pl.BlockSpec((None, 2, 1, HEAD_DIM, HEAD_DIM), lambda g, h: (layer, 0, h, 0, 0))
    scan_buf = pltpu.VMEM((2, N_CHUNK * PITCH, HEAD_DIM), _F32)
    return pl.pallas_call(
        _mixer_kernel,
        grid=(N_GROUPS, LRU_HEADS),
        in_specs=[
            zspec(0), zspec(8), zspec(16), zspec(24), zspec(32),
            vec(LRU_CONV), vec(1), wspec, wspec, vec(2), vec(2), vec(2), vec(SHORT_CONV),
            pl.BlockSpec((2, 8, HEAD_DIM), lambda g, h: (0, 0, h)),
            hidden(early), hidden(late),
            pl.BlockSpec((None, CAST_ROWS_FF, D_MODEL), lambda g, h: (layer,) + ends(g, h)),
            pl.BlockSpec((None, CAST_ROWS_IN, D_MODEL), lambda g, h: (layer,) + late(g, h)),
        ],
        out_specs=[
            pl.BlockSpec((GROUP_ROWS, HEAD_DIM), lambda g, h: (g, h)),
            pl.BlockSpec((GROUP_ROWS, HEAD_DIM), lambda g, h: (g, h)),
            pl.BlockSpec((2, N_CHUNK, HEAD_DIM), lambda g, h: (0, g, h)),
            hidden_bf(early), hidden_bf(late),
            pl.BlockSpec((CAST_ROWS_FF, D_MODEL), ends),
            pl.BlockSpec((CAST_ROWS_IN, D_MODEL), late),
        ],
        out_shape=[
            jax.ShapeDtypeStruct((ROWS, LRU_WIDTH), _BF),
            jax.ShapeDtypeStruct((ROWS, CONV_WIDTH), _BF),
            jax.ShapeDtypeStruct((2, N_GROUPS * N_CHUNK, LRU_WIDTH), _F32),
            hidden_b, hidden_b,
            jax.ShapeDtypeStruct((D_FF, D_MODEL), _BF),
            jax.ShapeDtypeStruct((D_MODEL, D_MODEL), _BF),
        ],
        scratch_shapes=[
            scan_buf, scan_buf, scan_buf,
            pltpu.VMEM((N_CHUNK * XPITCH, HEAD_DIM), _F32),
            pltpu.VMEM((4, N_CHUNK, HEAD_DIM), _F32),
            pltpu.VMEM((2, N_CHUNK, HEAD_DIM), _F32),
        ],
        compiler_params=_cparams(("arbitrary", "arbitrary")),
        name=f"mixer{layer}",
    )(z, z, z, z, z, cw, cb, w_a, w_i, b_a, b_i, lam, scw, h0, w_gate, w_up, w_down, w_out)


OUT_TM = 512


def _outproj_body(lru_ref, sc_ref, residual, w_ref, mod_ref, g_ref, x1_ref, hn2_ref):
    grp = _row_group(pl.program_id(0), OUT_TM)
    gate = _mod_row(mod_ref, 2, grp)
    scale = g_ref[1:2, :] * (1.0 + _mod_row(mod_ref, 4, grp))
    shift = _mod_row(mod_ref, 3, grp)
    for r0 in range(0, OUT_TM, OUT_TM // 2):
        rows = slice(r0, r0 + OUT_TM // 2)
        m = jnp.dot(lru_ref[rows, :], w_ref[0:LRU_WIDTH, :], preferred_element_type=_F32)
        m = m + jnp.dot(sc_ref[rows, :], w_ref[LRU_WIDTH:, :], preferred_element_type=_F32)
        x1 = residual(rows) + gate * m
        x1_ref[rows, :] = x1
        hn2_ref[rows, :] = (_rms(x1) * scale + shift).astype(_BF)


def _outproj_kernel(lru_ref, sc_ref, x_ref, w_ref, mod_ref, g_ref, x1_ref, hn2_ref):
    _outproj_body(lru_ref, sc_ref, lambda rows: x_ref[rows, :], w_ref, mod_ref, g_ref, x1_ref, hn2_ref)


def _outproj_split_kernel(lru_ref, sc_ref, xc_ref, xl_ref, w_ref, mod_ref, g_ref, x1_ref, hn2_ref):
    is_lat = pl.program_id(0) >= CTX_ROWS // OUT_TM
    residual = lambda rows: jnp.where(is_lat, xl_ref[rows, :], xc_ref[rows, :])
    _outproj_body(lru_ref, sc_ref, residual, w_ref, mod_ref, g_ref, x1_ref, hn2_ref)


def _outproj(layer, lru, sc, x, w_out_b, mod, norm_g):
    row = lambda i: (i, 0)
    nct = CTX_ROWS // OUT_TM
    if isinstance(x, tuple):
        body = _outproj_split_kernel
        x_specs = [pl.BlockSpec((OUT_TM, D_MODEL), lambda i: (jnp.minimum(i, nct - 1), 0)),
                   pl.BlockSpec((OUT_TM, D_MODEL), lambda i: (jnp.maximum(i - nct, 0), 0))]
    else:
        body = _outproj_kernel
        x_specs = [pl.BlockSpec((OUT_TM, D_MODEL), row)]
        x = (x,)
    return pl.pallas_call(
        body,
        grid=(ROWS // OUT_TM,),
        in_specs=[pl.BlockSpec((OUT_TM, LRU_WIDTH), row), pl.BlockSpec((OUT_TM, CONV_WIDTH), row)] + x_specs + [
            pl.BlockSpec((D_MODEL, D_MODEL), lambda i: (0, 0), pipeline_mode=pl.Buffered(1)),
            pl.BlockSpec((None, N_MOD, 8, D_MODEL), lambda i: (layer, 0, 0, 0)),
            pl.BlockSpec((None, 2, D_MODEL), lambda i: (layer, 0, 0)),
        ],
        out_specs=[pl.BlockSpec((OUT_TM, D_MODEL), row), pl.BlockSpec((OUT_TM, D_MODEL), row)],
        out_shape=[
            jax.ShapeDtypeStruct((ROWS, D_MODEL), _F32),
            jax.ShapeDtypeStruct((ROWS, D_MODEL), _BF),
        ],
        compiler_params=_cparams(("arbitrary",)),
        name=f"outproj{layer}",
    )(lru, sc, *x, w_out_b, mod, norm_g)


FFN_TM = 512
FFN_TF = CAST_CHUNK
FFN_STEPS = D_FF // FFN_TF
FFN_TILES = ROWS // FFN_TM
FFN_CTX_TILES = CTX_ROWS // FFN_TM
FFN_COLS = 256


def _finished_tile(i, f):
    final = (i == FFN_TILES - 1) & (f == FFN_STEPS - 1)
    return jnp.where(final, FFN_TILES - 1, jnp.maximum(i - 1, 0))


def _ffn_pipeline(hn_ref, wg_ref, wu_ref, wd_ref, acc_a, acc_b, finish, store):
    i = pl.program_id(0)
    f = pl.program_id(1)

    @pl.when((i == 0) & (f == 0))
    def _():
        acc_b[...] = jnp.zeros_like(acc_b)

    def chunk(cur, accumulate):
        h = hn_ref[...]
        acts = []
        for c0 in range(0, FFN_TF, FFN_COLS):
            gate = jnp.dot(h, wg_ref[:, c0:c0 + FFN_COLS], preferred_element_type=_F32)
            up = jnp.dot(h, wu_ref[:, c0:c0 + FFN_COLS], preferred_element_type=_F32)
            acts.append((gate * _sigmoid(gate) * up).astype(_BF))
        act = jnp.concatenate(acts, axis=1)
        for n0 in range(0, D_MODEL, FFN_COLS):
            part = jnp.dot(act, wd_ref[:, n0:n0 + FFN_COLS], preferred_element_type=_F32)
            if accumulate:
                cur[:, n0:n0 + FFN_COLS] += part
            else:
                cur[:, n0:n0 + FFN_COLS] = part

    def variant(cur, prev):
        @pl.when(f == 0)
        def _():
            tile = jnp.maximum(i - 1, 0)
            values = finish(prev, tile)
            chunk(cur, accumulate=False)
            store(values, tile)

        @pl.when(f > 0)
        def _():
            chunk(cur, accumulate=True)

        @pl.when((f == FFN_STEPS - 1) & (i == FFN_TILES - 1))
        def _():
            tile = jnp.int32(FFN_TILES - 1)
            store(finish(cur, tile), tile)

    @pl.when(i % 2 == 0)
    def _():
        variant(acc_a, acc_b)

    @pl.when(i % 2 == 1)
    def _():
        variant(acc_b, acc_a)


def _ffn_mid_kernel(hn_ref, x1_ref, wg_ref, wu_ref, wd_ref, mod_ref, modn_ref, gn_ref, x2_ref, hnn_ref,
                    acc_a, acc_b):
    def finish(acc_ref, tile):
        grp = _row_group(tile, FFN_TM)
        x2 = x1_ref[...] + _mod_row(mod_ref, 5, grp) * acc_ref[...]
        y = _rms(x2) * gn_ref[0:1, :]
        return x2, (y * (1.0 + _mod_row(modn_ref, 1, grp)) + _mod_row(modn_ref, 0, grp)).astype(_BF)

    def store(values, tile):
        x2_ref[...], hnn_ref[...] = values

    _ffn_pipeline(hn_ref, wg_ref, wu_ref, wd_ref, acc_a, acc_b, finish, store)


def _ffn_last_kernel(hn_ref, x1_ref, wg_ref, wu_ref, wd_ref, mod_ref, fg_ref, yp_ref, ys_ref, acc_a, acc_b):
    def finish(acc_ref, tile):
        grp = _row_group(tile, FFN_TM)
        x2 = x1_ref[...] + _mod_row(mod_ref, 5, grp) * acc_ref[...]
        return _rms(x2) * fg_ref[...]

    def store(y, tile):
        @pl.when(tile < FFN_CTX_TILES)
        def _():
            yp_ref[...] = y

        @pl.when(tile >= FFN_CTX_TILES)
        def _():
            ys_ref[...] = y

    _ffn_pipeline(hn_ref, wg_ref, wu_ref, wd_ref, acc_a, acc_b, finish, store)


def _ffn_specs(layer):
    residual = lambda i, f: (jnp.where(f == FFN_STEPS - 1, i, jnp.maximum(i - 1, 0)), 0)
    return [
        pl.BlockSpec((FFN_TM, D_MODEL), lambda i, f: (i, 0)),
        pl.BlockSpec((FFN_TM, D_MODEL), residual),
        pl.BlockSpec((None, D_MODEL, FFN_TF), lambda i, f: (f, 0, 0)),
        pl.BlockSpec((None, D_MODEL, FFN_TF), lambda i, f: (f, 0, 0)),
        pl.BlockSpec((FFN_TF, D_MODEL), lambda i, f: (f, 0)),
        pl.BlockSpec((None, N_MOD, 8, D_MODEL), lambda i, f: (layer, 0, 0, 0)),
    ]


def _ffn_mid(layer, hn2, x1, wg_b, wu_b, wd_b, mod, norm_g):
    row = lambda i, f: (_finished_tile(i, f), 0)
    return pl.pallas_call(
        _ffn_mid_kernel,
        grid=(FFN_TILES, FFN_STEPS),
        in_specs=_ffn_specs(layer) + [
            pl.BlockSpec((None, N_MOD, 8, D_MODEL), lambda i, f: (layer + 1, 0, 0, 0)),
            pl.BlockSpec((None, 2, D_MODEL), lambda i, f: (layer + 1, 0, 0)),
        ],
        out_specs=[pl.BlockSpec((FFN_TM, D_MODEL), row), pl.BlockSpec((FFN_TM, D_MODEL), row)],
        out_shape=[
            jax.ShapeDtypeStruct((ROWS, D_MODEL), _F32),
            jax.ShapeDtypeStruct((ROWS, D_MODEL), _BF),
        ],
        scratch_shapes=[pltpu.VMEM((FFN_TM, D_MODEL), _F32)] * 2,
        compiler_params=_cparams(("arbitrary", "arbitrary")),
        name=f"ffn{layer}",
    )(hn2, x1, wg_b, wu_b, wd_b, mod, mod, norm_g)


def _ffn_last(layer, hn2, x1, wg_b, wu_b, wd_b, mod, final_g):
    return pl.pallas_call(
        _ffn_last_kernel,
        grid=(FFN_TILES, FFN_STEPS),
        in_specs=_ffn_specs(layer) + [pl.BlockSpec((1, D_MODEL), lambda i, f: (0, 0))],
        out_specs=[
            pl.BlockSpec((FFN_TM, D_MODEL), lambda i, f: (jnp.minimum(_finished_tile(i, f), FFN_CTX_TILES - 1), 0)),
            pl.BlockSpec((FFN_TM, D_MODEL), lambda i, f: (jnp.maximum(_finished_tile(i, f) - FFN_CTX_TILES, 0), 0)),
        ],
        out_shape=[
            jax.ShapeDtypeStruct((CTX_ROWS, D_MODEL), _F32),
            jax.ShapeDtypeStruct((LAT_ROWS, D_MODEL), _F32),
        ],
        scratch_shapes=[pltpu.VMEM((FFN_TM, D_MODEL), _F32)] * 2,
        compiler_params=_cparams(("arbitrary", "arbitrary")),
        name=f"ffn{layer}",
    )(hn2, x1, wg_b, wu_b, wd_b, mod, final_g.reshape(1, D_MODEL))


def _pos_table():
    rows = DEC_SEQ // GRID_W
    r = np.repeat(np.arange(rows, dtype=np.float32), GRID_W)
    col = np.tile(np.arange(GRID_W, dtype=np.float32), rows)
    n_freq = D_MODEL // 4
    freqs = (1.0 / (np.float32(POS_BASE) ** (np.arange(n_freq, dtype=np.float32) / np.float32(n_freq)))).astype(np.float32)
    er = r[:, None] * freqs
    ec = col[:, None] * freqs
    return np.concatenate([np.sin(er), np.cos(er), np.sin(ec), np.cos(ec)], axis=-1).astype(np.float32)


def kernel(x_prompt, x_sample, state_rglru, c, c_ctx, w_mod, b_mod, norm_g, w_in, lru_conv_w, lru_conv_b,
           lru_w_a, lru_b_a, lru_w_i, lru_b_i, lru_lambda, sc_conv_w, w_out, w_gate, w_up, w_down, final_g):
    c8 = jnp.concatenate([c_ctx[None, :], c, jnp.zeros((8 - 1 - DEC_BATCH, D_MODEL), _F32)], axis=0)
    mod = _modulation(c8, w_mod, b_mod)

    pos = jnp.asarray(_pos_table())
    xp = x_prompt.reshape(CTX_ROWS, D_MODEL)
    hn, x_lat = _prenorm(xp, x_sample.reshape(LAT_ROWS, D_MODEL), pos, mod, norm_g)
    x = (xp, x_lat)

    cb = lru_conv_b.reshape(DEPTH, 1, LRU_WIDTH)
    b_a = lru_b_a.reshape(DEPTH, 2, LRU_WIDTH)
    b_i = lru_b_i.reshape(DEPTH, 2, LRU_WIDTH)
    h0 = jnp.pad(jnp.transpose(state_rglru, (1, 2, 0, 3)), ((0, 0), (0, 0), (0, 8 - DEC_BATCH), (0, 0)))

    states = []
    for layer in range(DEPTH):
        z = _inproj(layer, hn, w_in)
        lru, sc, st, wg_b, wu_b, wd_b, wo_b = _mixer(layer, z, lru_conv_w, cb, lru_w_a, lru_w_i, b_a, b_i, lru_lambda,
                                                     sc_conv_w, h0[layer], w_gate, w_up, w_down, w_out)
        states.append(st[:, :BATCH, :])
        x1, hn2 = _outproj(layer, lru, sc, x, wo_b, mod, norm_g)
        if layer + 1 < DEPTH:
            x, hn = _ffn_mid(layer, hn2, x1, wg_b, wu_b, wd_b, mod, norm_g)
        else:
            y_p, y_s = _ffn_last(layer, hn2, x1, wg_b, wu_b, wd_b, mod, final_g)

    new_state = jnp.transpose(jnp.stack(states, axis=0), (2, 0, 1, 3))
    return (y_p.reshape(BATCH, SEQ, D_MODEL), y_s.reshape(DEC_BATCH, DEC_SEQ, D_MODEL), new_state)
```

```python
import math

import numpy as np
import jax
import jax.numpy as jnp
from jax import lax
from jax.experimental import pallas as pl
from jax.experimental.pallas import tpu as pltpu

D_MODEL = 2048
BATCH = 32
SEQ = 256
DEPTH = 2
DEC_BATCH = 2
DEC_SEQ = 2048
GRID_W = 64
LRU_WIDTH = 1024
LRU_HEADS = 8
HEAD_DIM = 128
LRU_CONV = 4
LRU_C = 8.0
CONV_WIDTH = 1024
SHORT_CONV = 3
D_IN = 2 * LRU_WIDTH + 3 * CONV_WIDTH
D_FF = 5632
N_MOD = 6
EPS = 1e-6
POS_BASE = 10000.0

CTX_ROWS = BATCH * SEQ
LAT_ROWS = DEC_BATCH * DEC_SEQ
ROWS = CTX_ROWS + LAT_ROWS
N_SLAB = D_IN // HEAD_DIM

GROUP_ROWS = 2048
CHUNK = 256
N_CHUNK = GROUP_ROWS // CHUNK
N_GROUPS = ROWS // GROUP_ROWS
N_CTX_GROUPS = CTX_ROWS // GROUP_ROWS
SUBLANES = 8
PITCH = CHUNK + SUBLANES
XPITCH = CHUNK + 2 * SUBLANES

V7X_VMEM_LIMIT = 56 * 1024 * 1024
F32_TINY = float(np.finfo(np.float32).tiny)
GELU_C = math.sqrt(2.0 / math.pi)
GELU_C3 = 0.044715 * GELU_C

_BF = jnp.bfloat16
_F32 = jnp.float32


def _cparams(sem):
    return pltpu.CompilerParams(dimension_semantics=sem, vmem_limit_bytes=V7X_VMEM_LIMIT)


def _sigmoid(x):
    return 0.5 * jnp.tanh(0.5 * x) + 0.5


def _row_group(tile, rows_per_tile):
    return jnp.maximum((tile * rows_per_tile) // DEC_SEQ - CTX_ROWS // DEC_SEQ + 1, 0)


def _mod_row(mod_ref, j, grp):
    return mod_ref[j, pl.ds(grp, 1), :]


def _rms(x):
    return x * lax.rsqrt(jnp.mean(x * x, axis=-1, keepdims=True) + EPS)


MOD_TN = 1024


def _mod_kernel(c_ref, w_ref, b_ref, o_ref):
    c = c_ref[...]
    s = (c * _sigmoid(c)).astype(_BF)
    o_ref[...] = jnp.dot(s, w_ref[...].astype(_BF), preferred_element_type=_F32) + b_ref[...]


def _modulation(c8, w_mod, b_mod):
    per = D_MODEL // MOD_TN
    b4 = b_mod.reshape(DEPTH, N_MOD, 1, D_MODEL)
    return pl.pallas_call(
        _mod_kernel,
        grid=(DEPTH, N_MOD * per),
        in_specs=[
            pl.BlockSpec((8, D_MODEL), lambda l, n: (0, 0)),
            pl.BlockSpec((None, D_MODEL, MOD_TN), lambda l, n: (l, 0, n)),
            pl.BlockSpec((None, None, 1, MOD_TN), lambda l, n: (l, n // per, 0, n % per)),
        ],
        out_specs=pl.BlockSpec((None, None, 8, MOD_TN), lambda l, n: (l, n // per, 0, n % per)),
        out_shape=jax.ShapeDtypeStruct((DEPTH, N_MOD, 8, D_MODEL), _F32),
        compiler_params=_cparams(("arbitrary", "arbitrary")),
        name="modulation",
    )(c8, w_mod, b4)


PRE_TM = 512
PRE_RB = 32


def _prenorm_kernel(xp_ref, xs_ref, pos_ref, mod_ref, g_ref, hn_ref, x0_ref):
    i = pl.program_id(0)
    is_lat = i >= CTX_ROWS // PRE_TM
    grp = _row_group(i, PRE_TM)
    scale = g_ref[0:1, :] * (1.0 + _mod_row(mod_ref, 1, grp))
    shift = _mod_row(mod_ref, 0, grp)
    for r0 in range(0, PRE_TM, PRE_RB):
        rows = slice(r0, r0 + PRE_RB)
        x = jnp.where(is_lat, xs_ref[rows, :] + pos_ref[rows, :], xp_ref[rows, :])
        x0_ref[rows, :] = x
        hn_ref[rows, :] = (_rms(x) * scale + shift).astype(_BF)


def _prenorm(xp, xs, pos, mod, norm_g):
    nct = CTX_ROWS // PRE_TM
    npos = DEC_SEQ // PRE_TM
    return pl.pallas_call(
        _prenorm_kernel,
        grid=(ROWS // PRE_TM,),
        in_specs=[
            pl.BlockSpec((PRE_TM, D_MODEL), lambda i: (jnp.minimum(i, nct - 1), 0)),
            pl.BlockSpec((PRE_TM, D_MODEL), lambda i: (jnp.maximum(i - nct, 0), 0)),
            pl.BlockSpec((PRE_TM, D_MODEL), lambda i: (jnp.maximum(i - nct, 0) % npos, 0)),
            pl.BlockSpec((None, N_MOD, 8, D_MODEL), lambda i: (0, 0, 0, 0)),
            pl.BlockSpec((None, 2, D_MODEL), lambda i: (0, 0, 0)),
        ],
        out_specs=[
            pl.BlockSpec((PRE_TM, D_MODEL), lambda i: (i, 0)),
            pl.BlockSpec((PRE_TM, D_MODEL), lambda i: (jnp.maximum(i - nct, 0), 0)),
        ],
        out_shape=[
            jax.ShapeDtypeStruct((ROWS, D_MODEL), _BF),
            jax.ShapeDtypeStruct((LAT_ROWS, D_MODEL), _F32),
        ],
        compiler_params=_cparams(("arbitrary",)),
        name="prenorm",
    )(xp, xs, pos, mod, norm_g)


IN_TM = 1024
IN_TN = 1024


def _inproj_kernel(hn_ref, w_ref, z_ref, wb_scr):
    @pl.when(pl.program_id(1) == 0)
    def _():
        wb_scr[...] = w_ref[...].astype(_BF)

    acc = jnp.dot(hn_ref[...], wb_scr[...], preferred_element_type=_F32)
    for j in range(IN_TN // HEAD_DIM):
        z_ref[j] = acc[:, j * HEAD_DIM:(j + 1) * HEAD_DIM]


def _inproj(layer, hn, w_in):
    return pl.pallas_call(
        _inproj_kernel,
        grid=(D_IN // IN_TN, ROWS // IN_TM),
        in_specs=[
            pl.BlockSpec((IN_TM, D_MODEL), lambda n, i: (i, 0)),
            pl.BlockSpec((None, D_MODEL, IN_TN), lambda n, i: (layer, 0, n)),
        ],
        out_specs=pl.BlockSpec((IN_TN // HEAD_DIM, IN_TM, HEAD_DIM), lambda n, i: (n, i, 0)),
        out_shape=jax.ShapeDtypeStruct((N_SLAB, ROWS, HEAD_DIM), _F32),
        scratch_shapes=[pltpu.VMEM((D_MODEL, IN_TN), _BF)],
        compiler_params=_cparams(("arbitrary", "arbitrary")),
        name=f"inproj{layer}",
    )(hn, w_in)


MIX_STEPS = N_GROUPS * LRU_HEADS
CAST_STEPS = 32
CAST_SPLIT = CAST_STEPS // 2
CAST_CHUNK = 512
CAST_ROWS_IN = D_MODEL // CAST_STEPS
CAST_ROWS_FF = D_FF // CAST_STEPS


def _cast_hidden_in(w_ref, wb_ref):
    for c in range(D_FF // CAST_CHUNK):
        wb_ref[c] = w_ref[:, c * CAST_CHUNK:(c + 1) * CAST_CHUNK].astype(_BF)


def _mixer_kernel(xl_ref, yg_ref, bg_ref, cg_ref, v_ref, cw_ref, cb_ref, wa_ref, wi_ref, ba_ref, bi_ref,
                  lam_ref, scw_ref, h0_ref, wg_ref, wu_ref, wd_ref, wo_ref,
                  lru_ref, sc_ref, st_ref, wgb_ref, wub_ref, wdb_ref, wob_ref,
                  a_scr, u_scr, h_scr, pad_scr, pe_scr, hc_scr):
    g = pl.program_id(0)
    is_lat = g >= N_CTX_GROUPS

    step = g * LRU_HEADS + pl.program_id(1)

    @pl.when(step < CAST_STEPS)
    def _():
        _cast_hidden_in(wg_ref, wgb_ref)

    @pl.when(step >= MIX_STEPS - CAST_STEPS)
    def _():
        _cast_hidden_in(wu_ref, wub_ref)
        wob_ref[...] = wo_ref[...].astype(_BF)

    @pl.when((step < CAST_SPLIT) | (step >= MIX_STEPS - CAST_SPLIT))
    def _():
        wdb_ref[...] = wd_ref[...].astype(_BF)

    def fill_padded(rows):
        zeros = jnp.zeros((SUBLANES, HEAD_DIM), _F32)
        for s in range(N_CHUNK):
            base = s * XPITCH
            lo = s * CHUNK
            front = jnp.where(is_lat, rows(lo - SUBLANES, SUBLANES), 0.0) if s > 0 else zeros
            back = jnp.where(is_lat, rows(lo + CHUNK, SUBLANES), 0.0) if s + 1 < N_CHUNK else zeros
            pad_scr[base:base + SUBLANES, :] = front
            pad_scr[base + SUBLANES:base + SUBLANES + CHUNK, :] = rows(lo, CHUNK)
            pad_scr[base + SUBLANES + CHUNK:base + XPITCH, :] = back

    def tap(s, k):
        start = s * XPITCH + SUBLANES + k
        return pad_scr[start:start + CHUNK, :]

    fill_padded(lambda lo, n: xl_ref[0, lo:lo + n, :])
    wcat = (0.5 * jnp.concatenate([wa_ref[0, 0], wi_ref[0, 0], wa_ref[1, 0], wi_ref[1, 0]], axis=1)).astype(_BF)
    half_ba = 0.5 * ba_ref[...]
    half_bi = 0.5 * bi_ref[...]
    nlam = -lam_ref[...]
    softplus = jnp.maximum(nlam, 0.0) + jnp.log1p(jnp.exp(-jnp.abs(nlam)))
    half_rate = (-0.5 * LRU_C) * softplus

    for s in range(N_CHUNK):
        xc = (cw_ref[2:3, :] * xl_ref[0, s * CHUNK:(s + 1) * CHUNK, :] + cb_ref[...]
              + cw_ref[0:1, :] * tap(s, -2) + cw_ref[1:2, :] * tap(s, -1) + cw_ref[3:4, :] * tap(s, 1))
        gates = jnp.dot(xc.astype(_BF), wcat, preferred_element_type=_F32)
        for d in range(2):
            half_ra = gates[:, d * 256:d * 256 + HEAD_DIM] + half_ba[d:d + 1, :]
            half_ia = gates[:, d * 256 + HEAD_DIM:(d + 1) * 256] + half_bi[d:d + 1, :]
            log_a = half_rate[d:d + 1, :] * (1.0 + jnp.tanh(half_ra))
            a = jnp.exp(log_a)
            th = jnp.tanh(log_a)
            q = (-0.5 * th) / (1.0 - th)
            root = jnp.maximum(q, 0.0) * lax.rsqrt(jnp.maximum(q, F32_TINY))
            a_scr[d, s * PITCH:s * PITCH + CHUNK, :] = a
            u_scr[d, s * PITCH:s * PITCH + CHUNK, :] = root * ((1.0 + jnp.tanh(half_ia)) * xc)

    def strided(t):
        return pl.ds(t, N_CHUNK, stride=PITCH)

    @pl.when(jnp.logical_not(is_lat))
    def _():
        hc_scr[...] = jnp.zeros_like(hc_scr)

    @pl.when(is_lat)
    def _():
        def body(t, c):
            pf, ef, pb, eb = c
            af = a_scr[0, strided(t), :]
            ef = af * ef + u_scr[0, strided(t), :]
            pf = pf * af
            tb = CHUNK - 1 - t
            ab = a_scr[1, strided(tb), :]
            eb = ab * eb + u_scr[1, strided(tb), :]
            pb = pb * ab
            return pf, ef, pb, eb

        one = jnp.ones((N_CHUNK, HEAD_DIM), _F32)
        zero = jnp.zeros((N_CHUNK, HEAD_DIM), _F32)
        pf, ef, pb, eb = lax.fori_loop(0, CHUNK, body, (one, zero, one, zero), unroll=8)
        pe_scr[0] = pf
        pe_scr[1] = ef
        pe_scr[2] = pb
        pe_scr[3] = eb
        seq = g - N_CTX_GROUPS
        h = h0_ref[0, pl.ds(seq, 1), :]
        for c in range(N_CHUNK):
            hc_scr[0, c:c + 1, :] = h
            h = pe_scr[0, c:c + 1, :] * h + pe_scr[1, c:c + 1, :]
        h = h0_ref[1, pl.ds(seq, 1), :]
        for c in reversed(range(N_CHUNK)):
            hc_scr[1, c:c + 1, :] = h
            h = pe_scr[2, c:c + 1, :] * h + pe_scr[3, c:c + 1, :]

    def advance_two(d, h, first, second):
        a1 = a_scr[d, strided(first), :]
        u1 = u_scr[d, strided(first), :]
        a2 = a_scr[d, strided(second), :]
        u2 = u_scr[d, strided(second), :]
        h_scr[d, strided(first), :] = a1 * h + u1
        h = (a1 * a2) * h + (a2 * u1 + u2)
        h_scr[d, strided(second), :] = h
        return h

    hf = hc_scr[0]
    hb = hc_scr[1]
    for t in range(0, CHUNK, 2):
        hf = advance_two(0, hf, t, t + 1)
        hb = advance_two(1, hb, CHUNK - 1 - t, CHUNK - 2 - t)
    st_ref[0] = hf
    st_ref[1] = hb

    for s in range(N_CHUNK):
        y = yg_ref[0, s * CHUNK:(s + 1) * CHUNK, :]
        gelu = (0.5 * y) * (1.0 + jnp.tanh(y * (GELU_C + GELU_C3 * (y * y))))
        o = h_scr[0, s * PITCH:s * PITCH + CHUNK, :] + h_scr[1, s * PITCH:s * PITCH + CHUNK, :]
        lru_ref[s * CHUNK:(s + 1) * CHUNK, :] = (o * gelu).astype(_BF)

    fill_padded(lambda lo, n: cg_ref[0, lo:lo + n, :] * v_ref[0, lo:lo + n, :])
    for s in range(N_CHUNK):
        conv = scw_ref[1:2, :] * tap(s, 0) + scw_ref[0:1, :] * tap(s, -1) + scw_ref[2:3, :] * tap(s, 1)
        sc_ref[s * CHUNK:(s + 1) * CHUNK, :] = (bg_ref[0, s * CHUNK:(s + 1) * CHUNK, :] * conv).astype(_BF)


def _mixer(layer, z, cw, cb, w_a, w_i, b_a, b_i, lam, scw, h0, w_gate, w_up, w_down, w_out):
    assert 2 * CAST_SPLIT == CAST_STEPS and CAST_STEPS + CAST_SPLIT == MIX_STEPS

    def zspec(off):
        return pl.BlockSpec((1, GROUP_ROWS, HEAD_DIM), lambda g, h: (off + h, g, 0))

    def vec(rows):
        return pl.BlockSpec((None, rows, HEAD_DIM), lambda g, h: (layer, 0, h))

    step = lambda g, h: g * LRU_HEADS + h
    early = lambda g, h: (jnp.minimum(step(g, h), CAST_STEPS - 1), 0)
    late = lambda g, h: (jnp.maximum(step(g, h) - (MIX_STEPS - CAST_STEPS), 0), 0)
    ends = lambda g, h: (jnp.where(step(g, h) < CAST_SPLIT, step(g, h),
                                   jnp.maximum(step(g, h) - CAST_SPLIT, CAST_SPLIT - 1)), 0)
    hidden = lambda rows: pl.BlockSpec((None, CAST_ROWS_IN, D_FF), lambda g, h: (layer,) + rows(g, h))
    hidden_bf = lambda rows: pl.BlockSpec((D_FF // CAST_CHUNK, CAST_ROWS_IN, CAST_CHUNK),
                                          lambda g, h: (0,) + rows(g, h))
    hidden_b = jax.ShapeDtypeStruct((D_FF // CAST_CHUNK, D_MODEL, CAST_CHUNK), _BF)
    wspec = pl.BlockSpec((None, 2, 1, HEAD_DIM, HEAD_DIM), lambda g, h: (layer, 0, h, 0, 0))
    scan_buf = pltpu.VMEM((2, N_CHUNK * PITCH, HEAD_DIM), _F32)
    return pl.pallas_call(
        _mixer_kernel,
        grid=(N_GROUPS, LRU_HEADS),
        in_specs=[
            zspec(0), zspec(8), zspec(16), zspec(24), zspec(32),
            vec(LRU_CONV), vec(1), wspec, wspec, vec(2), vec(2), vec(2), vec(SHORT_CONV),
            pl.BlockSpec((2, 8, HEAD_DIM), lambda g, h: (0, 0, h)),
            hidden(early), hidden(late),
            pl.BlockSpec((None, CAST_ROWS_FF, D_MODEL), lambda g, h: (layer,) + ends(g, h)),
            pl.BlockSpec((None, CAST_ROWS_IN, D_MODEL), lambda g, h: (layer,) + late(g, h)),
        ],
        out_specs=[
            pl.BlockSpec((GROUP_ROWS, HEAD_DIM), lambda g, h: (g, h)),
            pl.BlockSpec((GROUP_ROWS, HEAD_DIM), lambda g, h: (g, h)),
            pl.BlockSpec((2, N_CHUNK, HEAD_DIM), lambda g, h: (0, g, h)),
            hidden_bf(early), hidden_bf(late),
            pl.BlockSpec((CAST_ROWS_FF, D_MODEL), ends),
            pl.BlockSpec((CAST_ROWS_IN, D_MODEL), late),
        ],
        out_shape=[
            jax.ShapeDtypeStruct((ROWS, LRU_WIDTH), _BF),
            jax.ShapeDtypeStruct((ROWS, CONV_WIDTH), _BF),
            jax.ShapeDtypeStruct((2, N_GROUPS * N_CHUNK, LRU_WIDTH), _F32),
            hidden_b, hidden_b,
            jax.ShapeDtypeStruct((D_FF, D_MODEL), _BF),
            jax.ShapeDtypeStruct((D_MODEL, D_MODEL), _BF),
        ],
        scratch_shapes=[
            scan_buf, scan_buf, scan_buf,
            pltpu.VMEM((N_CHUNK * XPITCH, HEAD_DIM), _F32),
            pltpu.VMEM((4, N_CHUNK, HEAD_DIM), _F32),
            pltpu.VMEM((2, N_CHUNK, HEAD_DIM), _F32),
        ],
        compiler_params=_cparams(("arbitrary", "arbitrary")),
        name=f"mixer{layer}",
    )(z, z, z, z, z, cw, cb, w_a, w_i, b_a, b_i, lam, scw, h0, w_gate, w_up, w_down, w_out)


OUT_TM = 512


def _outproj_body(lru_ref, sc_ref, residual, w_ref, mod_ref, g_ref, x1_ref, hn2_ref):
    grp = _row_group(pl.program_id(0), OUT_TM)
    gate = _mod_row(mod_ref, 2, grp)
    scale = g_ref[1:2, :] * (1.0 + _mod_row(mod_ref, 4, grp))
    shift = _mod_row(mod_ref, 3, grp)
    for r0 in range(0, OUT_TM, OUT_TM // 2):
        rows = slice(r0, r0 + OUT_TM // 2)
        m = jnp.dot(lru_ref[rows, :], w_ref[0:LRU_WIDTH, :], preferred_element_type=_F32)
        m = m + jnp.dot(sc_ref[rows, :], w_ref[LRU_WIDTH:, :], preferred_element_type=_F32)
        x1 = residual(rows) + gate * m
        x1_ref[rows, :] = x1
        hn2_ref[rows, :] = (_rms(x1) * scale + shift).astype(_BF)


def _outproj_kernel(lru_ref, sc_ref, x_ref, w_ref, mod_ref, g_ref, x1_ref, hn2_ref):
    _outproj_body(lru_ref, sc_ref, lambda rows: x_ref[rows, :], w_ref, mod_ref, g_ref, x1_ref, hn2_ref)


def _outproj_split_kernel(lru_ref, sc_ref, xc_ref, xl_ref, w_ref, mod_ref, g_ref, x1_ref, hn2_ref):
    is_lat = pl.program_id(0) >= CTX_ROWS // OUT_TM
    residual = lambda rows: jnp.where(is_lat, xl_ref[rows, :], xc_ref[rows, :])
    _outproj_body(lru_ref, sc_ref, residual, w_ref, mod_ref, g_ref, x1_ref, hn2_ref)


def _outproj(layer, lru, sc, x, w_out_b, mod, norm_g):
    row = lambda i: (i, 0)
    nct = CTX_ROWS // OUT_TM
    if isinstance(x, tuple):
        body = _outproj_split_kernel
        x_specs = [pl.BlockSpec((OUT_TM, D_MODEL), lambda i: (jnp.minimum(i, nct - 1), 0)),
                   pl.BlockSpec((OUT_TM, D_MODEL), lambda i: (jnp.maximum(i - nct, 0), 0))]
    else:
        body = _outproj_kernel
        x_specs = [pl.BlockSpec((OUT_TM, D_MODEL), row)]
        x = (x,)
    return pl.pallas_call(
        body,
        grid=(ROWS // OUT_TM,),
        in_specs=[pl.BlockSpec((OUT_TM, LRU_WIDTH), row), pl.BlockSpec((OUT_TM, CONV_WIDTH), row)] + x_specs + [
            pl.BlockSpec((D_MODEL, D_MODEL), lambda i: (0, 0), pipeline_mode=pl.Buffered(1)),
            pl.BlockSpec((None, N_MOD, 8, D_MODEL), lambda i: (layer, 0, 0, 0)),
            pl.BlockSpec((None, 2, D_MODEL), lambda i: (layer, 0, 0)),
        ],
        out_specs=[pl.BlockSpec((OUT_TM, D_MODEL), row), pl.BlockSpec((OUT_TM, D_MODEL), row)],
        out_shape=[
            jax.ShapeDtypeStruct((ROWS, D_MODEL), _F32),
            jax.ShapeDtypeStruct((ROWS, D_MODEL), _BF),
        ],
        compiler_params=_cparams(("arbitrary",)),
        name=f"outproj{layer}",
    )(lru, sc, *x, w_out_b, mod, norm_g)


FFN_TM = 512
FFN_TF = CAST_CHUNK
FFN_STEPS = D_FF // FFN_TF
FFN_TILES = ROWS // FFN_TM
FFN_CTX_TILES = CTX_ROWS // FFN_TM
FFN_COLS = 256


def _finished_tile(i, f):
    final = (i == FFN_TILES - 1) & (f == FFN_STEPS - 1)
    return jnp.where(final, FFN_TILES - 1, jnp.maximum(i - 1, 0))


def _ffn_pipeline(hn_ref, wg_ref, wu_ref, wd_ref, acc_a, acc_b, finish, store):
    i = pl.program_id(0)
    f = pl.program_id(1)

    @pl.when((i == 0) & (f == 0))
    def _():
        acc_b[...] = jnp.zeros_like(acc_b)

    def chunk(cur, accumulate):
        h = hn_ref[...]
        acts = []
        for c0 in range(0, FFN_TF, FFN_COLS):
            gate = jnp.dot(h, wg_ref[:, c0:c0 + FFN_COLS], preferred_element_type=_F32)
            up = jnp.dot(h, wu_ref[:, c0:c0 + FFN_COLS], preferred_element_type=_F32)
            acts.append((gate * _sigmoid(gate) * up).astype(_BF))
        act = jnp.concatenate(acts, axis=1)
        for n0 in range(0, D_MODEL, FFN_COLS):
            part = jnp.dot(act, wd_ref[:, n0:n0 + FFN_COLS], preferred_element_type=_F32)
            if accumulate:
                cur[:, n0:n0 + FFN_COLS] += part
            else:
                cur[:, n0:n0 + FFN_COLS] = part

    def variant(cur, prev):
        @pl.when(f == 0)
        def _():
            tile = jnp.maximum(i - 1, 0)
            values = finish(prev, tile)
            chunk(cur, accumulate=False)
            store(values, tile)

        @pl.when(f > 0)
        def _():
            chunk(cur, accumulate=True)

        @pl.when((f == FFN_STEPS - 1) & (i == FFN_TILES - 1))
        def _():
            tile = jnp.int32(FFN_TILES - 1)
            store(finish(cur, tile), tile)

    @pl.when(i % 2 == 0)
    def _():
        variant(acc_a, acc_b)

    @pl.when(i % 2 == 1)
    def _():
        variant(acc_b, acc_a)


def _ffn_mid_kernel(hn_ref, x1_ref, wg_ref, wu_ref, wd_ref, mod_ref, modn_ref, gn_ref, x2_ref, hnn_ref,
                    acc_a, acc_b):
    def finish(acc_ref, tile):
        grp = _row_group(tile, FFN_TM)
        x2 = x1_ref[...] + _mod_row(mod_ref, 5, grp) * acc_ref[...]
        y = _rms(x2) * gn_ref[0:1, :]
        return x2, (y * (1.0 + _mod_row(modn_ref, 1, grp)) + _mod_row(modn_ref, 0, grp)).astype(_BF)

    def store(values, tile):
        x2_ref[...], hnn_ref[...] = values

    _ffn_pipeline(hn_ref, wg_ref, wu_ref, wd_ref, acc_a, acc_b, finish, store)


def _ffn_last_kernel(hn_ref, x1_ref, wg_ref, wu_ref, wd_ref, mod_ref, fg_ref, yp_ref, ys_ref, acc_a, acc_b):
    def finish(acc_ref, tile):
        grp = _row_group(tile, FFN_TM)
        x2 = x1_ref[...] + _mod_row(mod_ref, 5, grp) * acc_ref[...]
        return _rms(x2) * fg_ref[...]

    def store(y, tile):
        @pl.when(tile < FFN_CTX_TILES)
        def _():
            yp_ref[...] = y

        @pl.when(tile >= FFN_CTX_TILES)
        def _():
            ys_ref[...] = y

    _ffn_pipeline(hn_ref, wg_ref, wu_ref, wd_ref, acc_a, acc_b, finish, store)


def _ffn_specs(layer):
    residual = lambda i, f: (jnp.where(f == FFN_STEPS - 1, i, jnp.maximum(i - 1, 0)), 0)
    return [
        pl.BlockSpec((FFN_TM, D_MODEL), lambda i, f: (i, 0)),
        pl.BlockSpec((FFN_TM, D_MODEL), residual),
        pl.BlockSpec((None, D_MODEL, FFN_TF), lambda i, f: (f, 0, 0)),
        pl.BlockSpec((None, D_MODEL, FFN_TF), lambda i, f: (f, 0, 0)),
        pl.BlockSpec((FFN_TF, D_MODEL), lambda i, f: (f, 0)),
        pl.BlockSpec((None, N_MOD, 8, D_MODEL), lambda i, f: (layer, 0, 0, 0)),
    ]


def _ffn_mid(layer, hn2, x1, wg_b, wu_b, wd_b, mod, norm_g):
    row = lambda i, f: (_finished_tile(i, f), 0)
    return pl.pallas_call(
        _ffn_mid_kernel,
        grid=(FFN_TILES, FFN_STEPS),
        in_specs=_ffn_specs(layer) + [
            pl.BlockSpec((None, N_MOD, 8, D_MODEL), lambda i, f: (layer + 1, 0, 0, 0)),
            pl.BlockSpec((None, 2, D_MODEL), lambda i, f: (layer + 1, 0, 0)),
        ],
        out_specs=[pl.BlockSpec((FFN_TM, D_MODEL), row), pl.BlockSpec((FFN_TM, D_MODEL), row)],
        out_shape=[
            jax.ShapeDtypeStruct((ROWS, D_MODEL), _F32),
            jax.ShapeDtypeStruct((ROWS, D_MODEL), _BF),
        ],
        scratch_shapes=[pltpu.VMEM((FFN_TM, D_MODEL), _F32)] * 2,
        compiler_params=_cparams(("arbitrary", "arbitrary")),
        name=f"ffn{layer}",
    )(hn2, x1, wg_b, wu_b, wd_b, mod, mod, norm_g)


def _ffn_last(layer, hn2, x1, wg_b, wu_b, wd_b, mod, final_g):
    return pl.pallas_call(
        _ffn_last_kernel,
        grid=(FFN_TILES, FFN_STEPS),
        in_specs=_ffn_specs(layer) + [pl.BlockSpec((1, D_MODEL), lambda i, f: (0, 0))],
        out_specs=[
            pl.BlockSpec((FFN_TM, D_MODEL), lambda i, f: (jnp.minimum(_finished_tile(i, f), FFN_CTX_TILES - 1), 0)),
            pl.BlockSpec((FFN_TM, D_MODEL), lambda i, f: (jnp.maximum(_finished_tile(i, f) - FFN_CTX_TILES, 0), 0)),
        ],
        out_shape=[
            jax.ShapeDtypeStruct((CTX_ROWS, D_MODEL), _F32),
            jax.ShapeDtypeStruct((LAT_ROWS, D_MODEL), _F32),
        ],
        scratch_shapes=[pltpu.VMEM((FFN_TM, D_MODEL), _F32)] * 2,
        compiler_params=_cparams(("arbitrary", "arbitrary")),
        name=f"ffn{layer}",
    )(hn2, x1, wg_b, wu_b, wd_b, mod, final_g.reshape(1, D_MODEL))


def _pos_table():
    rows = DEC_SEQ // GRID_W
    r = np.repeat(np.arange(rows, dtype=np.float32), GRID_W)
    col = np.tile(np.arange(GRID_W, dtype=np.float32), rows)
    n_freq = D_MODEL // 4
    freqs = (1.0 / (np.float32(POS_BASE) ** (np.arange(n_freq, dtype=np.float32) / np.float32(n_freq)))).astype(np.float32)
    er = r[:, None] * freqs
    ec = col[:, None] * freqs
    return np.concatenate([np.sin(er), np.cos(er), np.sin(ec), np.cos(ec)], axis=-1).astype(np.float32)


def kernel(x_prompt, x_sample, state_rglru, c, c_ctx, w_mod, b_mod, norm_g, w_in, lru_conv_w, lru_conv_b,
           lru_w_a, lru_b_a, lru_w_i, lru_b_i, lru_lambda, sc_conv_w, w_out, w_gate, w_up, w_down, final_g):
    c8 = jnp.concatenate([c_ctx[None, :], c, jnp.zeros((8 - 1 - DEC_BATCH, D_MODEL), _F32)], axis=0)
    mod = _modulation(c8, w_mod, b_mod)

    pos = jnp.asarray(_pos_table())
    xp = x_prompt.reshape(CTX_ROWS, D_MODEL)
    hn, x_lat = _prenorm(xp, x_sample.reshape(LAT_ROWS, D_MODEL), pos, mod, norm_g)
    x = (xp, x_lat)

    cb = lru_conv_b.reshape(DEPTH, 1, LRU_WIDTH)
    b_a = lru_b_a.reshape(DEPTH, 2, LRU_WIDTH)
    b_i = lru_b_i.reshape(DEPTH, 2, LRU_WIDTH)
    h0 = jnp.pad(jnp.transpose(state_rglru, (1, 2, 0, 3)), ((0, 0), (0, 0), (0, 8 - DEC_BATCH), (0, 0)))

    states = []
    for layer in range(DEPTH):
        z = _inproj(layer, hn, w_in)
        lru, sc, st, wg_b, wu_b, wd_b, wo_b = _mixer(layer, z, lru_conv_w, cb, lru_w_a, lru_w_i, b_a, b_i, lru_lambda,
                                                     sc_conv_w, h0[layer], w_gate, w_up, w_down, w_out)
        states.append(st[:, :BATCH, :])
        x1, hn2 = _outproj(layer, lru, sc, x, wo_b, mod, norm_g)
        if layer + 1 < DEPTH:
            x, hn = _ffn_mid(layer, hn2, x1, wg_b, wu_b, wd_b, mod, norm_g)
        else:
            y_p, y_s = _ffn_last(layer, hn2, x1, wg_b, wu_b, wd_b, mod, final_g)

    new_state = jnp.transpose(jnp.stack(states, axis=0), (2, 0, 1, 3))
    return (y_p.reshape(BATCH, SEQ, D_MODEL), y_s.reshape(DEC_BATCH, DEC_SEQ, D_MODEL), new_state)
```

```python
import math

import numpy as np
import jax
import jax.numpy as jnp
from jax import lax
from jax.experimental import pallas as pl
from jax.experimental.pallas import tpu as pltpu

D_MODEL = 2048
BATCH = 32
SEQ = 256
DEPTH = 2
DEC_BATCH = 2
DEC_SEQ = 2048
GRID_W = 64
LRU_WIDTH = 1024
LRU_HEADS = 8
HEAD_DIM = 128
LRU_CONV = 4
LRU_C = 8.0
CONV_WIDTH = 1024
SHORT_CONV = 3
D_IN = 2 * LRU_WIDTH + 3 * CONV_WIDTH
D_FF = 5632
N_MOD = 6
EPS = 1e-6
POS_BASE = 10000.0

CTX_ROWS = BATCH * SEQ
LAT_ROWS = DEC_BATCH * DEC_SEQ
ROWS = CTX_ROWS + LAT_ROWS
N_SLAB = D_IN // HEAD_DIM

GROUP_ROWS = 2048
CHUNK = 256
N_CHUNK = GROUP_ROWS // CHUNK
N_GROUPS = ROWS // GROUP_ROWS
N_CTX_GROUPS = CTX_ROWS // GROUP_ROWS
SUBLANES = 8
PITCH = CHUNK + SUBLANES
XPITCH = CHUNK + 2 * SUBLANES

V7X_VMEM_LIMIT = 56 * 1024 * 1024
F32_TINY = float(np.finfo(np.float32).tiny)
GELU_C = math.sqrt(2.0 / math.pi)
GELU_C3 = 0.044715 * GELU_C

_BF = jnp.bfloat16
_F32 = jnp.float32


def _cparams(sem):
    return pltpu.CompilerParams(dimension_semantics=sem, vmem_limit_bytes=V7X_VMEM_LIMIT)


def _sigmoid(x):
    return 0.5 * jnp.tanh(0.5 * x) + 0.5


def _row_group(tile, rows_per_tile):
    return jnp.maximum((tile * rows_per_tile) // DEC_SEQ - CTX_ROWS // DEC_SEQ + 1, 0)


def _mod_row(mod_ref, j, grp):
    return mod_ref[j, pl.ds(grp, 1), :]


def _rms(x):
    return x * lax.rsqrt(jnp.mean(x * x, axis=-1, keepdims=True) + EPS)


MOD_TN = 1024


def _mod_kernel(c_ref, w_ref, b_ref, o_ref):
    c = c_ref[...]
    s = (c * _sigmoid(c)).astype(_BF)
    o_ref[...] = jnp.dot(s, w_ref[...].astype(_BF), preferred_element_type=_F32) + b_ref[...]


def _modulation(c8, w_mod, b_mod):
    per = D_MODEL // MOD_TN
    b4 = b_mod.reshape(DEPTH, N_MOD, 1, D_MODEL)
    return pl.pallas_call(
        _mod_kernel,
        grid=(DEPTH, N_MOD * per),
        in_specs=[
            pl.BlockSpec((8, D_MODEL), lambda l, n: (0, 0)),
            pl.BlockSpec((None, D_MODEL, MOD_TN), lambda l, n: (l, 0, n)),
            pl.BlockSpec((None, None, 1, MOD_TN), lambda l, n: (l, n // per, 0, n % per)),
        ],
        out_specs=pl.BlockSpec((None, None, 8, MOD_TN), lambda l, n: (l, n // per, 0, n % per)),
        out_shape=jax.ShapeDtypeStruct((DEPTH, N_MOD, 8, D_MODEL), _F32),
        compiler_params=_cparams(("arbitrary", "arbitrary")),
        name="modulation",
    )(c8, w_mod, b4)


PRE_TM = 512
PRE_RB = 32


def _prenorm_kernel(xp_ref, xs_ref, pos_ref, mod_ref, g_ref, hn_ref, x0_ref):
    i = pl.program_id(0)
    is_lat = i >= CTX_ROWS // PRE_TM
    grp = _row_group(i, PRE_TM)
    scale = g_ref[0:1, :] * (1.0 + _mod_row(mod_ref, 1, grp))
    shift = _mod_row(mod_ref, 0, grp)
    for r0 in range(0, PRE_TM, PRE_RB):
        rows = slice(r0, r0 + PRE_RB)
        x = jnp.where(is_lat, xs_ref[rows, :] + pos_ref[rows, :], xp_ref[rows, :])
        x0_ref[rows, :] = x
        hn_ref[rows, :] = (_rms(x) * scale + shift).astype(_BF)


def _prenorm(xp, xs, pos, mod, norm_g):
    nct = CTX_ROWS // PRE_TM
    npos = DEC_SEQ // PRE_TM
    return pl.pallas_call(
        _prenorm_kernel,
        grid=(ROWS // PRE_TM,),
        in_specs=[
            pl.BlockSpec((PRE_TM, D_MODEL), lambda i: (jnp.minimum(i, nct - 1), 0)),
            pl.BlockSpec((PRE_TM, D_MODEL), lambda i: (jnp.maximum(i - nct, 0), 0)),
            pl.BlockSpec((PRE_TM, D_MODEL), lambda i: (jnp.maximum(i - nct, 0) % npos, 0)),
            pl.BlockSpec((None, N_MOD, 8, D_MODEL), lambda i: (0, 0, 0, 0)),
            pl.BlockSpec((None, 2, D_MODEL), lambda i: (0, 0, 0)),
        ],
        out_specs=[
            pl.BlockSpec((PRE_TM, D_MODEL), lambda i: (i, 0)),
            pl.BlockSpec((PRE_TM, D_MODEL), lambda i: (jnp.maximum(i - nct, 0), 0)),
        ],
        out_shape=[
            jax.ShapeDtypeStruct((ROWS, D_MODEL), _BF),
            jax.ShapeDtypeStruct((LAT_ROWS, D_MODEL), _F32),
        ],
        compiler_params=_cparams(("arbitrary",)),
        name="prenorm",
    )(xp, xs, pos, mod, norm_g)


IN_TM = 1024
IN_TN = 1024


def _inproj_kernel(hn_ref, w_ref, z_ref, wb_scr):
    @pl.when(pl.program_id(1) == 0)
    def _():
        wb_scr[...] = w_ref[...].astype(_BF)

    acc = jnp.dot(hn_ref[...], wb_scr[...], preferred_element_type=_F32)
    for j in range(IN_TN // HEAD_DIM):
        z_ref[j] = acc[:, j * HEAD_DIM:(j + 1) * HEAD_DIM].astype(_BF)


def _inproj(layer, hn, w_in):
    return pl.pallas_call(
        _inproj_kernel,
        grid=(D_IN // IN_TN, ROWS // IN_TM),
        in_specs=[
            pl.BlockSpec((IN_TM, D_MODEL), lambda n, i: (i, 0)),
            pl.BlockSpec((None, D_MODEL, IN_TN), lambda n, i: (layer, 0, n)),
        ],
        out_specs=pl.BlockSpec((IN_TN // HEAD_DIM, IN_TM, HEAD_DIM), lambda n, i: (n, i, 0)),
        out_shape=jax.ShapeDtypeStruct((N_SLAB, ROWS, HEAD_DIM), _BF),
        scratch_shapes=[pltpu.VMEM((D_MODEL, IN_TN), _BF)],
        compiler_params=_cparams(("arbitrary", "arbitrary")),
        name=f"inproj{layer}",
    )(hn, w_in)


MIX_STEPS = N_GROUPS * LRU_HEADS
CAST_STEPS = 32
CAST_SPLIT = CAST_STEPS // 2
CAST_CHUNK = 512
CAST_ROWS_IN = D_MODEL // CAST_STEPS
CAST_ROWS_FF = D_FF // CAST_STEPS


BF16_ROWS = 16


def _z_rows(z_ref, lo, n):
    if n % BF16_ROWS == 0:
        return z_ref[0, lo:lo + n, :].astype(_F32)
    base = (lo // BF16_ROWS) * BF16_ROWS
    tile = z_ref[0, base:base + BF16_ROWS, :].astype(_F32)
    return tile[lo - base:lo - base + n]


def _cast_hidden_in(w_ref, wb_ref):
    for c in range(D_FF // CAST_CHUNK):
        wb_ref[c] = w_ref[:, c * CAST_CHUNK:(c + 1) * CAST_CHUNK].astype(_BF)


def _mixer_kernel(xl_ref, yg_ref, bg_ref, cg_ref, v_ref, cw_ref, cb_ref, wa_ref, wi_ref, ba_ref, bi_ref,
                  lam_ref, scw_ref, h0_ref, wg_ref, wu_ref, wd_ref, wo_ref,
                  lru_ref, sc_ref, st_ref, wgb_ref, wub_ref, wdb_ref, wob_ref,
                  a_scr, u_scr, h_scr, pad_scr, pe_scr, hc_scr):
    g = pl.program_id(0)
    is_lat = g >= N_CTX_GROUPS

    step = g * LRU_HEADS + pl.program_id(1)

    @pl.when(step < CAST_STEPS)
    def _():
        _cast_hidden_in(wg_ref, wgb_ref)

    @pl.when(step >= MIX_STEPS - CAST_STEPS)
    def _():
        _cast_hidden_in(wu_ref, wub_ref)
        wob_ref[...] = wo_ref[...].astype(_BF)

    @pl.when((step < CAST_SPLIT) | (step >= MIX_STEPS - CAST_SPLIT))
    def _():
        wdb_ref[...] = wd_ref[...].astype(_BF)

    def fill_padded(rows):
        zeros = jnp.zeros((SUBLANES, HEAD_DIM), _F32)
        for s in range(N_CHUNK):
            base = s * XPITCH
            lo = s * CHUNK
            front = jnp.where(is_lat, rows(lo - SUBLANES, SUBLANES), 0.0) if s > 0 else zeros
            back = jnp.where(is_lat, rows(lo + CHUNK, SUBLANES), 0.0) if s + 1 < N_CHUNK else zeros
            pad_scr[base:base + SUBLANES, :] = front
            pad_scr[base + SUBLANES:base + SUBLANES + CHUNK, :] = rows(lo, CHUNK)
            pad_scr[base + SUBLANES + CHUNK:base + XPITCH, :] = back

    def tap(s, k):
        start = s * XPITCH + SUBLANES + k
        return pad_scr[start:start + CHUNK, :]

    fill_padded(lambda lo, n: _z_rows(xl_ref, lo, n))
    wcat = (0.5 * jnp.concatenate([wa_ref[0, 0], wi_ref[0, 0], wa_ref[1, 0], wi_ref[1, 0]], axis=1)).astype(_BF)
    half_ba = 0.5 * ba_ref[...]
    half_bi = 0.5 * bi_ref[...]
    nlam = -lam_ref[...]
    softplus = jnp.maximum(nlam, 0.0) + jnp.log1p(jnp.exp(-jnp.abs(nlam)))
    half_rate = (-0.5 * LRU_C) * softplus

    for s in range(N_CHUNK):
        xc = (cw_ref[2:3, :] * _z_rows(xl_ref, s * CHUNK, CHUNK) + cb_ref[...]
              + cw_ref[0:1, :] * tap(s, -2) + cw_ref[1:2, :] * tap(s, -1) + cw_ref[3:4, :] * tap(s, 1))
        gates = jnp.dot(xc.astype(_BF), wcat, preferred_element_type=_F32)
        for d in range(2):
            half_ra = gates[:, d * 256:d * 256 + HEAD_DIM] + half_ba[d:d + 1, :]
            half_ia = gates[:, d * 256 + HEAD_DIM:(d + 1) * 256] + half_bi[d:d + 1, :]
            log_a = half_rate[d:d + 1, :] * (1.0 + jnp.tanh(half_ra))
            a = jnp.exp(log_a)
            th = jnp.tanh(log_a)
            q = (-0.5 * th) / (1.0 - th)
            root = jnp.maximum(q, 0.0) * lax.rsqrt(jnp.maximum(q, F32_TINY))
            a_scr[d, s * PITCH:s * PITCH + CHUNK, :] = a
            u_scr[d, s * PITCH:s * PITCH + CHUNK, :] = root * ((1.0 + jnp.tanh(half_ia)) * xc)

    def strided(t):
        return pl.ds(t, N_CHUNK, stride=PITCH)

    @pl.when(jnp.logical_not(is_lat))
    def _():
        hc_scr[...] = jnp.zeros_like(hc_scr)

    @pl.when(is_lat)
    def _():
        def body(t, c):
            pf, ef, pb, eb = c
            af = a_scr[0, strided(t), :]
            ef = af * ef + u_scr[0, strided(t), :]
            pf = pf * af
            tb = CHUNK - 1 - t
            ab = a_scr[1, strided(tb), :]
            eb = ab * eb + u_scr[1, strided(tb), :]
            pb = pb * ab
            return pf, ef, pb, eb

        one = jnp.ones((N_CHUNK, HEAD_DIM), _F32)
        zero = jnp.zeros((N_CHUNK, HEAD_DIM), _F32)
        pf, ef, pb, eb = lax.fori_loop(0, CHUNK, body, (one, zero, one, zero), unroll=8)
        pe_scr[0] = pf
        pe_scr[1] = ef
        pe_scr[2] = pb
        pe_scr[3] = eb
        seq = g - N_CTX_GROUPS
        h = h0_ref[0, pl.ds(seq, 1), :]
        for c in range(N_CHUNK):
            hc_scr[0, c:c + 1, :] = h
            h = pe_scr[0, c:c + 1, :] * h + pe_scr[1, c:c + 1, :]
        h = h0_ref[1, pl.ds(seq, 1), :]
        for c in reversed(range(N_CHUNK)):
            hc_scr[1, c:c + 1, :] = h
            h = pe_scr[2, c:c + 1, :] * h + pe_scr[3, c:c + 1, :]

    def advance_two(d, h, first, second):
        a1 = a_scr[d, strided(first), :]
        u1 = u_scr[d, strided(first), :]
        a2 = a_scr[d, strided(second), :]
        u2 = u_scr[d, strided(second), :]
        h_scr[d, strided(first), :] = a1 * h + u1
        h = (a1 * a2) * h + (a2 * u1 + u2)
        h_scr[d, strided(second), :] = h
        return h

    hf = hc_scr[0]
    hb = hc_scr[1]
    for t in range(0, CHUNK, 2):
        hf = advance_two(0, hf, t, t + 1)
        hb = advance_two(1, hb, CHUNK - 1 - t, CHUNK - 2 - t)
    st_ref[0] = hf
    st_ref[1] = hb

    for s in range(N_CHUNK):
        y = _z_rows(yg_ref, s * CHUNK, CHUNK)
        gelu = (0.5 * y) * (1.0 + jnp.tanh(y * (GELU_C + GELU_C3 * (y * y))))
        o = h_scr[0, s * PITCH:s * PITCH + CHUNK, :] + h_scr[1, s * PITCH:s * PITCH + CHUNK, :]
        lru_ref[s * CHUNK:(s + 1) * CHUNK, :] = (o * gelu).astype(_BF)

    fill_padded(lambda lo, n: _z_rows(cg_ref, lo, n) * _z_rows(v_ref, lo, n))
    for s in range(N_CHUNK):
        conv = scw_ref[1:2, :] * tap(s, 0) + scw_ref[0:1, :] * tap(s, -1) + scw_ref[2:3, :] * tap(s, 1)
        sc_ref[s * CHUNK:(s + 1) * CHUNK, :] = (_z_rows(bg_ref, s * CHUNK, CHUNK) * conv).astype(_BF)


def _mixer(layer, z, cw, cb, w_a, w_i, b_a, b_i, lam, scw, h0, w_gate, w_up, w_down, w_out):
    assert 2 * CAST_SPLIT == CAST_STEPS and CAST_STEPS + CAST_SPLIT == MIX_STEPS

    def zspec(off):
        return pl.BlockSpec((1, GROUP_ROWS, HEAD_DIM), lambda g, h: (off + h, g, 0))

    def vec(rows):
        return pl.BlockSpec((None, rows, HEAD_DIM), lambda g, h: (layer, 0, h))

    step = lambda g, h: g * LRU_HEADS + h
    early = lambda g, h: (jnp.minimum(step(g, h), CAST_STEPS - 1), 0)
    late = lambda g, h: (jnp.maximum(step(g, h) - (MIX_STEPS - CAST_STEPS), 0), 0)
    ends = lambda g, h: (jnp.where(step(g, h) < CAST_SPLIT, step(g, h),
                                   jnp.maximum(step(g, h) - CAST_SPLIT, CAST_SPLIT - 1)), 0)
    hidden = lambda rows: pl.BlockSpec((None, CAST_ROWS_IN, D_FF), lambda g, h: (layer,) + rows(g, h))
    hidden_bf = lambda rows: pl.BlockSpec((D_FF // CAST_CHUNK, CAST_ROWS_IN, CAST_CHUNK),
                                          lambda g, h: (0,) + rows(g, h))
    hidden_b = jax.ShapeDtypeStruct((D_FF // CAST_CHUNK, D_MODEL, CAST_CHUNK), _BF)
    wspec = pl.BlockSpec((None, 2, 1, HEAD_DIM, HEAD_DIM), lambda g, h: (layer, 0, h, 0, 0))
    scan_buf = pltpu.VMEM((2, N_CHUNK * PITCH, HEAD_DIM), _F32)
    return pl.pallas_call(
        _mixer_kernel,
        grid=(N_GROUPS, LRU_HEADS),
        in_specs=[
            zspec(0), zspec(8), zspec(16), zspec(24), zspec(32),
            vec(LRU_CONV), vec(1), wspec, wspec, vec(2), vec(2), vec(2), vec(SHORT_CONV),
            pl.BlockSpec((2, 8, HEAD_DIM), lambda g, h: (0, 0, h)),
            hidden(early), hidden(late),
            pl.BlockSpec((None, CAST_ROWS_FF, D_MODEL), lambda g, h: (layer,) + ends(g, h)),
            pl.BlockSpec((None, CAST_ROWS_IN, D_MODEL), lambda g, h: (layer,) + late(g, h)),
        ],
        out_specs=[
            pl.BlockSpec((GROUP_ROWS, HEAD_DIM), lambda g, h: (g, h)),
            pl.BlockSpec((GROUP_ROWS, HEAD_DIM), lambda g, h: (g, h)),
            pl.BlockSpec((2, N_CHUNK, HEAD_DIM), lambda g, h: (0, g, h)),
            hidden_bf(early), hidden_bf(late),
            pl.BlockSpec((CAST_ROWS_FF, D_MODEL), ends),
            pl.BlockSpec((CAST_ROWS_IN, D_MODEL), late),
        ],
        out_shape=[
            jax.ShapeDtypeStruct((ROWS, LRU_WIDTH), _BF),
            jax.ShapeDtypeStruct((ROWS, CONV_WIDTH), _BF),
            jax.ShapeDtypeStruct((2, N_GROUPS * N_CHUNK, LRU_WIDTH), _F32),
            hidden_b, hidden_b,
            jax.ShapeDtypeStruct((D_FF, D_MODEL), _BF),
            jax.ShapeDtypeStruct((D_MODEL, D_MODEL), _BF),
        ],
        scratch_shapes=[
            scan_buf, scan_buf, scan_buf,
            pltpu.VMEM((N_CHUNK * XPITCH, HEAD_DIM), _F32),
            pltpu.VMEM((4, N_CHUNK, HEAD_DIM), _F32),
            pltpu.VMEM((2, N_CHUNK, HEAD_DIM), _F32),
        ],
        compiler_params=_cparams(("arbitrary", "arbitrary")),
        name=f"mixer{layer}",
    )(z, z, z, z, z, cw, cb, w_a, w_i, b_a, b_i, lam, scw, h0, w_gate, w_up, w_down, w_out)


OUT_TM = 512


def _outproj_body(lru_ref, sc_ref, residual, w_ref, mod_ref, g_ref, x1_ref, hn2_ref):
    grp = _row_group(pl.program_id(0), OUT_TM)
    gate = _mod_row(mod_ref, 2, grp)
    scale = g_ref[1:2, :] * (1.0 + _mod_row(mod_ref, 4, grp))
    shift = _mod_row(mod_ref, 3, grp)
    for r0 in range(0, OUT_TM, OUT_TM // 2):
        rows = slice(r0, r0 + OUT_TM // 2)
        m = jnp.dot(lru_ref[rows, :], w_ref[0:LRU_WIDTH, :], preferred_element_type=_F32)
        m = m + jnp.dot(sc_ref[rows, :], w_ref[LRU_WIDTH:, :], preferred_element_type=_F32)
        x1 = residual(rows) + gate * m
        x1_ref[rows, :] = x1
        hn2_ref[rows, :] = (_rms(x1) * scale + shift).astype(_BF)


def _outproj_kernel(lru_ref, sc_ref, x_ref, w_ref, mod_ref, g_ref, x1_ref, hn2_ref):
    _outproj_body(lru_ref, sc_ref, lambda rows: x_ref[rows, :], w_ref, mod_ref, g_ref, x1_ref, hn2_ref)


def _outproj_split_kernel(lru_ref, sc_ref, xc_ref, xl_ref, w_ref, mod_ref, g_ref, x1_ref, hn2_ref):
    is_lat = pl.program_id(0) >= CTX_ROWS // OUT_TM
    residual = lambda rows: jnp.where(is_lat, xl_ref[rows, :], xc_ref[rows, :])
    _outproj_body(lru_ref, sc_ref, residual, w_ref, mod_ref, g_ref, x1_ref, hn2_ref)


def _outproj(layer, lru, sc, x, w_out_b, mod, norm_g):
    row = lambda i: (i, 0)
    nct = CTX_ROWS // OUT_TM
    if isinstance(x, tuple):
        body = _outproj_split_kernel
        x_specs = [pl.BlockSpec((OUT_TM, D_MODEL), lambda i: (jnp.minimum(i, nct - 1), 0)),
                   pl.BlockSpec((OUT_TM, D_MODEL), lambda i: (jnp.maximum(i - nct, 0), 0))]
    else:
        body = _outproj_kernel
        x_specs = [pl.BlockSpec((OUT_TM, D_MODEL), row)]
        x = (x,)
    return pl.pallas_call(
        body,
        grid=(ROWS // OUT_TM,),
        in_specs=[pl.BlockSpec((OUT_TM, LRU_WIDTH), row), pl.BlockSpec((OUT_TM, CONV_WIDTH), row)] + x_specs + [
            pl.BlockSpec((D_MODEL, D_MODEL), lambda i: (0, 0), pipeline_mode=pl.Buffered(1)),
            pl.BlockSpec((None, N_MOD, 8, D_MODEL), lambda i: (layer, 0, 0, 0)),
            pl.BlockSpec((None, 2, D_MODEL), lambda i: (layer, 0, 0)),
        ],
        out_specs=[pl.BlockSpec((OUT_TM, D_MODEL), row), pl.BlockSpec((OUT_TM, D_MODEL), row)],
        out_shape=[
            jax.ShapeDtypeStruct((ROWS, D_MODEL), _F32),
            jax.ShapeDtypeStruct((ROWS, D_MODEL), _BF),
        ],
        compiler_params=_cparams(("arbitrary",)),
        name=f"outproj{layer}",
    )(lru, sc, *x, w_out_b, mod, norm_g)


FFN_TM = 512
FFN_TF = CAST_CHUNK
FFN_STEPS = D_FF // FFN_TF
FFN_TILES = ROWS // FFN_TM
FFN_CTX_TILES = CTX_ROWS // FFN_TM
FFN_COLS = 256


def _finished_tile(i, f):
    final = (i == FFN_TILES - 1) & (f == FFN_STEPS - 1)
    return jnp.where(final, FFN_TILES - 1, jnp.maximum(i - 1, 0))


def _ffn_pipeline(hn_ref, wg_ref, wu_ref, wd_ref, acc_a, acc_b, finish, store):
    i = pl.program_id(0)
    f = pl.program_id(1)

    @pl.when((i == 0) & (f == 0))
    def _():
        acc_b[...] = jnp.zeros_like(acc_b)

    def chunk(cur, accumulate):
        h = hn_ref[...]
        acts = []
        for c0 in range(0, FFN_TF, FFN_COLS):
            gate = jnp.dot(h, wg_ref[:, c0:c0 + FFN_COLS], preferred_element_type=_F32)
            up = jnp.dot(h, wu_ref[:, c0:c0 + FFN_COLS], preferred_element_type=_F32)
            acts.append((gate * _sigmoid(gate) * up).astype(_BF))
        act = jnp.concatenate(acts, axis=1)
        for n0 in range(0, D_MODEL, FFN_COLS):
            part = jnp.dot(act, wd_ref[:, n0:n0 + FFN_COLS], preferred_element_type=_F32)
            if accumulate:
                cur[:, n0:n0 + FFN_COLS] += part
            else:
                cur[:, n0:n0 + FFN_COLS] = part

    def variant(cur, prev):
        @pl.when(f == 0)
        def _():
            tile = jnp.maximum(i - 1, 0)
            values = finish(prev, tile)
            chunk(cur, accumulate=False)
            store(values, tile)

        @pl.when(f > 0)
        def _():
            chunk(cur, accumulate=True)

        @pl.when((f == FFN_STEPS - 1) & (i == FFN_TILES - 1))
        def _():
            tile = jnp.int32(FFN_TILES - 1)
            store(finish(cur, tile), tile)

    @pl.when(i % 2 == 0)
    def _():
        variant(acc_a, acc_b)

    @pl.when(i % 2 == 1)
    def _():
        variant(acc_b, acc_a)


def _ffn_mid_kernel(hn_ref, x1_ref, wg_ref, wu_ref, wd_ref, mod_ref, modn_ref, gn_ref, x2_ref, hnn_ref,
                    acc_a, acc_b):
    def finish(acc_ref, tile):
        grp = _row_group(tile, FFN_TM)
        x2 = x1_ref[...] + _mod_row(mod_ref, 5, grp) * acc_ref[...]
        y = _rms(x2) * gn_ref[0:1, :]
        return x2, (y * (1.0 + _mod_row(modn_ref, 1, grp)) + _mod_row(modn_ref, 0, grp)).astype(_BF)

    def store(values, tile):
        x2_ref[...], hnn_ref[...] = values

    _ffn_pipeline(hn_ref, wg_ref, wu_ref, wd_ref, acc_a, acc_b, finish, store)


def _ffn_last_kernel(hn_ref, x1_ref, wg_ref, wu_ref, wd_ref, mod_ref, fg_ref, yp_ref, ys_ref, acc_a, acc_b):
    def finish(acc_ref, tile):
        grp = _row_group(tile, FFN_TM)
        x2 = x1_ref[...] + _mod_row(mod_ref, 5, grp) * acc_ref[...]
        return _rms(x2) * fg_ref[...]

    def store(y, tile):
        @pl.when(tile < FFN_CTX_TILES)
        def _():
            yp_ref[...] = y

        @pl.when(tile >= FFN_CTX_TILES)
        def _():
            ys_ref[...] = y

    _ffn_pipeline(hn_ref, wg_ref, wu_ref, wd_ref, acc_a, acc_b, finish, store)


def _ffn_specs(layer):
    residual = lambda i, f: (jnp.where(f == FFN_STEPS - 1, i, jnp.maximum(i - 1, 0)), 0)
    return [
        pl.BlockSpec((FFN_TM, D_MODEL), lambda i, f: (i, 0)),
        pl.BlockSpec((FFN_TM, D_MODEL), residual),
        pl.BlockSpec((None, D_MODEL, FFN_TF), lambda i, f: (f, 0, 0)),
        pl.BlockSpec((None, D_MODEL, FFN_TF), lambda i, f: (f, 0, 0)),
        pl.BlockSpec((FFN_TF, D_MODEL), lambda i, f: (f, 0)),
        pl.BlockSpec((None, N_MOD, 8, D_MODEL), lambda i, f: (layer, 0, 0, 0)),
    ]


def _ffn_mid(layer, hn2, x1, wg_b, wu_b, wd_b, mod, norm_g):
    row = lambda i, f: (_finished_tile(i, f), 0)
    return pl.pallas_call(
        _ffn_mid_kernel,
        grid=(FFN_TILES, FFN_STEPS),
        in_specs=_ffn_specs(layer) + [
            pl.BlockSpec((None, N_MOD, 8, D_MODEL), lambda i, f: (layer + 1, 0, 0, 0)),
            pl.BlockSpec((None, 2, D_MODEL), lambda i, f: (layer + 1, 0, 0)),
        ],
        out_specs=[pl.BlockSpec((FFN_TM, D_MODEL), row), pl.BlockSpec((FFN_TM, D_MODEL), row)],
        out_shape=[
            jax.ShapeDtypeStruct((ROWS, D_MODEL), _F32),
            jax.ShapeDtypeStruct((ROWS, D_MODEL), _BF),
        ],
        scratch_shapes=[pltpu.VMEM((FFN_TM, D_MODEL), _F32)] * 2,
        compiler_params=_cparams(("arbitrary", "arbitrary")),
        name=f"ffn{layer}",
    )(hn2, x1, wg_b, wu_b, wd_b, mod, mod, norm_g)


def _ffn_last(layer, hn2, x1, wg_b, wu_b, wd_b, mod, final_g):
    return pl.pallas_call(
        _ffn_last_kernel,
        grid=(FFN_TILES, FFN_STEPS),
        in_specs=_ffn_specs(layer) + [pl.BlockSpec((1, D_MODEL), lambda i, f: (0, 0))],
        out_specs=[
            pl.BlockSpec((FFN_TM, D_MODEL), lambda i, f: (jnp.minimum(_finished_tile(i, f), FFN_CTX_TILES - 1), 0)),
            pl.BlockSpec((FFN_TM, D_MODEL), lambda i, f: (jnp.maximum(_finished_tile(i, f) - FFN_CTX_TILES, 0), 0)),
        ],
        out_shape=[
            jax.ShapeDtypeStruct((CTX_ROWS, D_MODEL), _F32),
            jax.ShapeDtypeStruct((LAT_ROWS, D_MODEL), _F32),
        ],
        scratch_shapes=[pltpu.VMEM((FFN_TM, D_MODEL), _F32)] * 2,
        compiler_params=_cparams(("arbitrary", "arbitrary")),
        name=f"ffn{layer}",
    )(hn2, x1, wg_b, wu_b, wd_b, mod, final_g.reshape(1, D_MODEL))


def _pos_table():
    rows = DEC_SEQ // GRID_W
    r = np.repeat(np.arange(rows, dtype=np.float32), GRID_W)
    col = np.tile(np.arange(GRID_W, dtype=np.float32), rows)
    n_freq = D_MODEL // 4
    freqs = (1.0 / (np.float32(POS_BASE) ** (np.arange(n_freq, dtype=np.float32) / np.float32(n_freq)))).astype(np.float32)
    er = r[:, None] * freqs
    ec = col[:, None] * freqs
    return np.concatenate([np.sin(er), np.cos(er), np.sin(ec), np.cos(ec)], axis=-1).astype(np.float32)


def kernel(x_prompt, x_sample, state_rglru, c, c_ctx, w_mod, b_mod, norm_g, w_in, lru_conv_w, lru_conv_b,
           lru_w_a, lru_b_a, lru_w_i, lru_b_i, lru_lambda, sc_conv_w, w_out, w_gate, w_up, w_down, final_g):
    c8 = jnp.concatenate([c_ctx[None, :], c, jnp.zeros((8 - 1 - DEC_BATCH, D_MODEL), _F32)], axis=0)
    mod = _modulation(c8, w_mod, b_mod)

    pos = jnp.asarray(_pos_table())
    xp = x_prompt.reshape(CTX_ROWS, D_MODEL)
    hn, x_lat = _prenorm(xp, x_sample.reshape(LAT_ROWS, D_MODEL), pos, mod, norm_g)
    x = (xp, x_lat)

    cb = lru_conv_b.reshape(DEPTH, 1, LRU_WIDTH)
    b_a = lru_b_a.reshape(DEPTH, 2, LRU_WIDTH)
    b_i = lru_b_i.reshape(DEPTH, 2, LRU_WIDTH)
    h0 = jnp.pad(jnp.transpose(state_rglru, (1, 2, 0, 3)), ((0, 0), (0, 0), (0, 8 - DEC_BATCH), (0, 0)))

    states = []
    for layer in range(DEPTH):
        z = _inproj(layer, hn, w_in)
        lru, sc, st, wg_b, wu_b, wd_b, wo_b = _mixer(layer, z, lru_conv_w, cb, lru_w_a, lru_w_i, b_a, b_i, lru_lambda,
                                                     sc_conv_w, h0[layer], w_gate, w_up, w_down, w_out)
        states.append(st[:, :BATCH, :])
        x1, hn2 = _outproj(layer, lru, sc, x, wo_b, mod, norm_g)
        if layer + 1 < DEPTH:
            x, hn = _ffn_mid(layer, hn2, x1, wg_b, wu_b, wd_b, mod, norm_g)
        else:
            y_p, y_s = _ffn_last(layer, hn2, x1, wg_b, wu_b, wd_b, mod, final_g)

    new_state = jnp.transpose(jnp.stack(states, axis=0), (2, 0, 1, 3))
    return (y_p.reshape(BATCH, SEQ, D_MODEL), y_s.reshape(DEC_BATCH, DEC_SEQ, D_MODEL), new_state)
```
